```python
import jax, jax.numpy as jnp
from jax import lax
import numpy as np

D_MODEL = 1024
BATCH = 4
SEQ = 8192
DEPTH = 4

N_A_LAYERS = DEPTH // 2
N_B_LAYERS = DEPTH - N_A_LAYERS
CHUNK = 128
GMLP_WIDTH = 2 * D_MODEL
GMLP_GROUPS = 16
GMLP_GROUP_DIM = GMLP_WIDTH // GMLP_GROUPS
N_HEADS = 16
HEAD_DIM = D_MODEL // N_HEADS
Q_BLOCK = 128
FFN_HIDDEN = -(-8 * D_MODEL // (3 * 256)) * 256
EPS = 1e-6

kernel_name = "yoco_gmlp_fox_adaln_sandwich"


def rms_norm(x, g):
    xf = x.astype(jnp.float32)
    y = xf * lax.rsqrt(jnp.mean(xf * xf, axis=-1, keepdims=True) + EPS)
    return (y * g.astype(jnp.float32)).astype(x.dtype)


def layer_norm(x, g, b):
    xf = x.astype(jnp.float32)
    mu = jnp.mean(xf, axis=-1, keepdims=True)
    xc = xf - mu
    y = xc * lax.rsqrt(jnp.mean(xc * xc, axis=-1, keepdims=True) + EPS)
    return (y * g.astype(jnp.float32) + b.astype(jnp.float32)).astype(x.dtype)


def modulate(h, shift, scale):
    return h * (1 + scale[:, None, :]) + shift[:, None, :]


def swiglu(h, w_gu, w_down):
    gu = h @ w_gu
    g, u = jnp.split(gu, 2, axis=-1)
    return (jax.nn.silu(g) * u) @ w_down


def gmlp_mixer(h, w_in, b_in, ln_g, ln_b, w_s, b_s, w_out):
    B, S, _ = h.shape
    z = jax.nn.gelu(h @ w_in + b_in)
    u, v = jnp.split(z, 2, axis=-1)
    v = layer_norm(v, ln_g, ln_b)
    v = v.reshape(B, S // CHUNK, CHUNK, GMLP_GROUPS, GMLP_GROUP_DIM)
    causal = jnp.tril(jnp.ones((CHUNK, CHUNK), dtype=w_s.dtype))
    ws = w_s * causal[None]
    v = jnp.einsum('gts,bnsgc->bntgc', ws, v) + b_s.T[:, :, None]
    y = u * v.reshape(B, S, GMLP_WIDTH)
    return y @ w_out


def shared_kv(x, mod_kv, kv_norm_g, kv_w, kv_b_f, k_norm_g):
    B, S, _ = x.shape
    shift, scale = jnp.split(mod_kv, 2, axis=-1)
    h = modulate(rms_norm(x, kv_norm_g), shift, scale)
    kvf = h @ kv_w
    k = kvf[..., :D_MODEL].reshape(B, S, N_HEADS, HEAD_DIM)
    k = rms_norm(k, k_norm_g).transpose(0, 2, 1, 3)
    v = kvf[..., D_MODEL:2 * D_MODEL].reshape(B, S, N_HEADS, HEAD_DIM).transpose(0, 2, 1, 3)
    f_logit = kvf[..., 2 * D_MODEL:].astype(jnp.float32) + kv_b_f.astype(jnp.float32)
    dcum = jnp.cumsum(jax.nn.log_sigmoid(f_logit), axis=1).transpose(0, 2, 1)
    return k, v, dcum


def fox_attention(q, k, v, dcum):
    B, H, S, Dh = q.shape
    nb = S // Q_BLOCK
    qb = q.reshape(B, H, nb, Q_BLOCK, Dh).transpose(2, 0, 1, 3, 4)
    db = dcum.reshape(B, H, nb, Q_BLOCK).transpose(2, 0, 1, 3)
    kpos = jnp.arange(S)
    scale = HEAD_DIM ** -0.5

    def block(args):
        qi, di, i = args
        qpos = i * Q_BLOCK + jnp.arange(Q_BLOCK)
        logits = jnp.einsum('bhqd,bhkd->bhqk', qi, k).astype(jnp.float32) * scale
        logits = logits + di[..., :, None] - dcum[..., None, :]
        logits = jnp.where(kpos[None, :] <= qpos[:, None], logits, -jnp.inf)
        p = jax.nn.softmax(logits, axis=-1)
        return jnp.einsum('bhqk,bhkd->bhqd', p.astype(v.dtype), v)

    o = lax.map(block, (qb, db, jnp.arange(nb)))
    return o.transpose(1, 2, 0, 3, 4).reshape(B, H, S, Dh)


def fox_mixer(h, w_qg, q_norm_g, w_o, k, v, dcum):
    B, S, _ = h.shape
    qg = h @ w_qg
    q = qg[..., :D_MODEL].reshape(B, S, N_HEADS, HEAD_DIM)
    q = rms_norm(q, q_norm_g).transpose(0, 2, 1, 3)
    gate = jax.nn.sigmoid(qg[..., D_MODEL:])
    o = fox_attention(q, k, v, dcum).transpose(0, 2, 1, 3).reshape(B, S, D_MODEL)
    return (o * gate) @ w_o


def setup_inputs(seed: int = 0) -> dict:
    key = jax.random.key(seed)
    ks = jax.random.split(key, 26)
    D, F, GW = D_MODEL, FFN_HIDDEN, GMLP_WIDTH

    def nrm(k, shape, scale):
        return jax.random.normal(k, shape, jnp.float32) * scale

    def gain(k, shape):
        return 1.0 + nrm(k, shape, 0.05)

    return {
        "x": nrm(ks[0], (BATCH, SEQ, D), 1.0),
        "c": nrm(ks[1], (BATCH, D), 1.0),
        "ada_w": nrm(ks[2], (DEPTH, D, 6 * D), 0.5 * D ** -0.5),
        "ada_b": nrm(ks[3], (DEPTH, 6 * D), 0.01),
        "pre_mix_g": gain(ks[4], (DEPTH, D)),
        "post_mix_g": gain(ks[5], (DEPTH, D)),
        "pre_ffn_g": gain(ks[6], (DEPTH, D)),
        "post_ffn_g": gain(ks[7], (DEPTH, D)),
        "ffn_w_gu": nrm(ks[8], (DEPTH, D, 2 * F), D ** -0.5),
        "ffn_w_down": nrm(ks[9], (DEPTH, F, D), F ** -0.5),
        "a_w_in": nrm(ks[10], (N_A_LAYERS, D, 2 * GW), D ** -0.5),
        "a_b_in": nrm(ks[11], (N_A_LAYERS, 2 * GW), 0.01),
        "a_ln_g": gain(ks[12], (N_A_LAYERS, GW)),
        "a_ln_b": nrm(ks[13], (N_A_LAYERS, GW), 0.01),
        "a_w_s": nrm(ks[14], (N_A_LAYERS, GMLP_GROUPS, CHUNK, CHUNK), 0.5 * CHUNK ** -0.5),
        "a_b_s": 1.0 + nrm(ks[15], (N_A_LAYERS, GMLP_GROUPS, CHUNK), 0.1),
        "a_w_out": nrm(ks[16], (N_A_LAYERS, GW, D), GW ** -0.5),
        "kv_ada_w": nrm(ks[17], (D, 2 * D), 0.5 * D ** -0.5),
        "kv_ada_b": nrm(ks[18], (2 * D,), 0.01),
        "kv_norm_g": gain(ks[19], (D,)),
        "kv_w": nrm(ks[20], (D, 2 * D + N_HEADS), D ** -0.5),
        "kv_b_f": jax.random.uniform(ks[21], (N_HEADS,), jnp.float32, 1.0, 5.0),
        "k_norm_g": gain(ks[22], (HEAD_DIM,)),
        "b_w_qg": nrm(ks[23], (N_B_LAYERS, D, 2 * D), D ** -0.5),
        "b_q_norm_g": gain(ks[24], (N_B_LAYERS, HEAD_DIM)),
        "b_w_o": nrm(ks[25], (N_B_LAYERS, D, D), D ** -0.5),
    }


def reference(x, c, ada_w, ada_b, pre_mix_g, post_mix_g, pre_ffn_g, post_ffn_g,
              ffn_w_gu, ffn_w_down, a_w_in, a_b_in, a_ln_g, a_ln_b, a_w_s, a_b_s,
              a_w_out, kv_ada_w, kv_ada_b, kv_norm_g, kv_w, kv_b_f, k_norm_g,
              b_w_qg, b_q_norm_g, b_w_o):
    c_act = jax.nn.silu(c)
    k = v = dcum = None
    for layer in range(DEPTH):
        mod = c_act @ ada_w[layer] + ada_b[layer]
        sh_m, sc_m, g_m, sh_f, sc_f, g_f = jnp.split(mod, 6, axis=-1)
        h = modulate(rms_norm(x, pre_mix_g[layer]), sh_m, sc_m)
        if layer < N_A_LAYERS:
            i = layer
            y = gmlp_mixer(h, a_w_in[i], a_b_in[i], a_ln_g[i], a_ln_b[i],
                           a_w_s[i], a_b_s[i], a_w_out[i])
        else:
            j = layer - N_A_LAYERS
            y = fox_mixer(h, b_w_qg[j], b_q_norm_g[j], b_w_o[j], k, v, dcum)
        x = x + g_m[:, None, :] * rms_norm(y, post_mix_g[layer])
        h = modulate(rms_norm(x, pre_ffn_g[layer]), sh_f, sc_f)
        y = swiglu(h, ffn_w_gu[layer], ffn_w_down[layer])
        x = x + g_f[:, None, :] * rms_norm(y, post_ffn_g[layer])
        if layer == N_A_LAYERS - 1:
            k, v, dcum = shared_kv(x, c_act @ kv_ada_w + kv_ada_b, kv_norm_g,
                                   kv_w, kv_b_f, k_norm_g)
    return x
```

```python
import functools

import jax
import jax.numpy as jnp
from jax import lax
from jax.experimental import pallas as pl
from jax.experimental.pallas import tpu as pltpu

F32 = jnp.float32
BF16 = jnp.bfloat16

EPS = 1e-6
N_HEADS = 16
HEAD_DIM = 64
CHUNK = 128
GROUPS = 16
LOG2E = 1.4426950408889634
LANES = 128
MXU_N = 256
TOKEN_TILE = 512
VMEM_LIMIT = 56 * 1024 * 1024


def _sigmoid(x):
    return 1.0 / (1.0 + jnp.exp(-x))


def _rms(x, g):
    return x * lax.rsqrt(jnp.mean(x * x, axis=-1, keepdims=True) + EPS) * g


def _prenorm(x, g, shift, scale):
    return _rms(x, g) * (1.0 + scale) + shift


def _split3(x):
    hi = x.astype(BF16).astype(F32)
    r = x - hi
    mid = r.astype(BF16).astype(F32)
    lo = (r - mid).astype(BF16).astype(F32)
    return hi, mid, lo


def _const_spec(shape):
    n = len(shape)
    return pl.BlockSpec(shape, lambda *_: (0,) * n, pipeline_mode=pl.Buffered(1))


def _params(n_grid):
    return pltpu.CompilerParams(
        dimension_semantics=("arbitrary",) * n_grid, vmem_limit_bytes=VMEM_LIMIT)


def _mod_body(c_ref, w_ref, b_ref, o_ref):
    c = c_ref[...]
    ca = (c * _sigmoid(c)).astype(BF16)
    o_ref[...] = jnp.dot(ca, w_ref[...].astype(BF16), preferred_element_type=F32) + b_ref[...]


def _modulation(c8, w, b):
    n_l, d, n = w.shape
    nb = 1024
    return pl.pallas_call(
        _mod_body,
        grid=(n_l, n // nb),
        in_specs=[
            pl.BlockSpec((8, d), lambda l, j: (0, 0)),
            pl.BlockSpec((None, d, nb), lambda l, j: (l, 0, j)),
            pl.BlockSpec((None, 1, nb), lambda l, j: (l, 0, j)),
        ],
        out_specs=pl.BlockSpec((None, 8, nb), lambda l, j: (l, 0, j)),
        out_shape=jax.ShapeDtypeStruct((n_l, 8, n), F32),
        compiler_params=_params(2),
        name="modulation",
    )(c8, w, b.reshape(n_l, 1, n))


def _gmlp_body(x_ref, mod_ref, pre_ref, post_ref, win_ref, bin_ref, lng_ref, lnb_ref, ws_ref,
               bs_ref, wout_ref, o_ref, h_scr, u_scr, v_scr, vn_scr, y_scr):
    tm, d = x_ref.shape
    gw = u_scr.shape[1]
    x = x_ref[...]
    h_scr[...] = _prenorm(x, pre_ref[...], mod_ref[0:1, :], mod_ref[1:2, :]).astype(BF16)

    k_gelu = 0.7978845608028654
    nc = 512
    for c0 in range(0, 2 * gw, nc):
        z = jnp.dot(h_scr[...], win_ref[:, c0:c0 + nc], preferred_element_type=F32)
        z = z + bin_ref[:, c0:c0 + nc]
        z = z * (0.5 * (1.0 + jnp.tanh(k_gelu * (z + 0.044715 * (z * z * z)))))
        if c0 < gw:
            u_scr[:, c0:c0 + nc] = z
        else:
            v_scr[:, c0 - gw:c0 - gw + nc] = z

    v = v_scr[...]
    mu = jnp.mean(v, axis=-1, keepdims=True)
    vc = v - mu
    rstd = lax.rsqrt(jnp.mean(vc * vc, axis=-1, keepdims=True) + EPS)
    vn_scr[...] = (vc * rstd * lng_ref[...] + lnb_ref[...]).astype(BF16)

    n_chunks = tm // CHUNK
    row = lax.broadcasted_iota(jnp.int32, (CHUNK, CHUNK), 0)
    col = lax.broadcasted_iota(jnp.int32, (CHUNK, CHUNK), 1)
    causal = col <= row
    for g in range(GROUPS):
        gs = slice(g * CHUNK, (g + 1) * CHUNK)
        ws = jnp.where(causal, ws_ref[g], 0.0).astype(BF16)
        rhs = jnp.concatenate(
            [vn_scr[c * CHUNK:(c + 1) * CHUNK, gs] for c in range(n_chunks)], axis=1)
        sp = jnp.dot(ws, rhs, preferred_element_type=F32)
        for c in range(n_chunks):
            cs = slice(c * CHUNK, (c + 1) * CHUNK)
            y_scr[cs, gs] = (u_scr[cs, gs] * (sp[:, cs] + bs_ref[:, gs])).astype(BF16)

    y = jnp.dot(y_scr[...], wout_ref[...], preferred_element_type=F32)
    o_ref[...] = x + mod_ref[2:3, :] * _rms(y, post_ref[...])


def _gmlp_layer(x, mod, pre_g, post_g, w_in, b_in, ln_g, ln_b, w_s, bs_full, w_out):
    b, s, d = x.shape
    gw = w_out.shape[0]
    tm = TOKEN_TILE
    tok = pl.BlockSpec((None, tm, d), lambda i, j: (i, j, 0))
    return pl.pallas_call(
        _gmlp_body,
        grid=(b, s // tm),
        in_specs=[
            tok,
            pl.BlockSpec((None, 6, d), lambda i, j: (i, 0, 0)),
            _const_spec((1, d)), _const_spec((1, d)),
            _const_spec((d, 2 * gw)), _const_spec((1, 2 * gw)),
            _const_spec((1, gw)), _const_spec((1, gw)),
            _const_spec((GROUPS, CHUNK, CHUNK)), _const_spec((CHUNK, gw)),
            _const_spec((gw, d)),
        ],
        out_specs=tok,
        out_shape=jax.ShapeDtypeStruct((b, s, d), F32),
        scratch_shapes=[
            pltpu.VMEM((tm, d), BF16), pltpu.VMEM((tm, gw), F32), pltpu.VMEM((tm, gw), F32),
            pltpu.VMEM((tm, gw), BF16), pltpu.VMEM((tm, gw), BF16),
        ],
        compiler_params=_params(2),
        name="gmlp_mixer",
    )(x, mod, pre_g, post_g, w_in, b_in, ln_g, ln_b, w_s, bs_full, w_out)


def _ffn_body(x_ref, mod_ref, pre_ref, post_ref, wgu_ref, wd_ref, o_ref, h_scr, a_scr):
    f = a_scr.shape[1]
    x = x_ref[...]
    h_scr[...] = _prenorm(x, pre_ref[...], mod_ref[3:4, :], mod_ref[4:5, :]).astype(BF16)
    for c0 in range(0, f, MXU_N):
        g = jnp.dot(h_scr[...], wgu_ref[:, c0:c0 + MXU_N], preferred_element_type=F32)
        u = jnp.dot(h_scr[...], wgu_ref[:, f + c0:f + c0 + MXU_N], preferred_element_type=F32)
        a_scr[:, c0:c0 + MXU_N] = (g * _sigmoid(g) * u).astype(BF16)
    y = jnp.dot(a_scr[...], wd_ref[...], preferred_element_type=F32)
    o_ref[...] = x + mod_ref[5:6, :] * _rms(y, post_ref[...])


def _ffn_layer(x, mod, pre_g, post_g, w_gu, w_down):
    b, s, d = x.shape
    f = w_down.shape[0]
    tm = TOKEN_TILE
    tok = pl.BlockSpec((None, tm, d), lambda i, j: (i, j, 0))
    return pl.pallas_call(
        _ffn_body,
        grid=(b, s // tm),
        in_specs=[
            tok,
            pl.BlockSpec((None, 6, d), lambda i, j: (i, 0, 0)),
            _const_spec((1, d)), _const_spec((1, d)),
            _const_spec((d, 2 * f)), _const_spec((f, d)),
        ],
        out_specs=tok,
        out_shape=jax.ShapeDtypeStruct((b, s, d), F32),
        scratch_shapes=[pltpu.VMEM((tm, d), BF16), pltpu.VMEM((tm, f), BF16)],
        compiler_params=_params(2),
        name="swiglu_ffn",
    )(x, mod, pre_g, post_g, w_gu, w_down)


def _head_pair_norm(t, gain2):
    lane = lax.broadcasted_iota(jnp.int32, t.shape, 1)
    sq = t * t
    first = lane < HEAD_DIM
    s0 = jnp.sum(jnp.where(first, sq, 0.0), axis=-1, keepdims=True)
    s1 = jnp.sum(jnp.where(first, 0.0, sq), axis=-1, keepdims=True)
    rs = jnp.where(first, lax.rsqrt(s0 / HEAD_DIM + EPS), lax.rsqrt(s1 / HEAD_DIM + EPS))
    return t * rs * gain2


def _augment(t, dcol, sign_first):
    lane = lax.broadcasted_iota(jnp.int32, t.shape, 1)
    hi, mid, lo = _split3(jnp.broadcast_to(dcol, t.shape))
    one = jnp.ones_like(t)
    if sign_first:
        a, b, c, e, f, g = hi, mid, lo, one, one, one
    else:
        a, b, c, e, f, g = one, one, one, -hi, -mid, -lo
    tail = jnp.where(lane == HEAD_DIM, a, jnp.where(lane == HEAD_DIM + 1, b, jnp.where(
        lane == HEAD_DIM + 2, c, jnp.where(lane == HEAD_DIM + 3, e, jnp.where(
            lane == HEAD_DIM + 4, f, jnp.where(lane == HEAD_DIM + 5, g, 0.0))))))
    return jnp.where(lane < HEAD_DIM, t, tail).astype(BF16)


def _kv_body(x_ref, mod_ref, g_ref, wk_ref, wv_ref, wf_ref, bf_ref, kg_ref,
             k_out, vt_out, d_out, h_scr, carry_scr):
    tm, d = x_ref.shape

    @pl.when(pl.program_id(1) == 0)
    def _():
        carry_scr[...] = jnp.zeros_like(carry_scr)

    h_scr[...] = _prenorm(x_ref[...], g_ref[...], mod_ref[0:1, :], mod_ref[1:2, :]).astype(BF16)

    lane = lax.broadcasted_iota(jnp.int32, (tm, LANES), 1)
    fl = jnp.dot(h_scr[...], wf_ref[...], preferred_element_type=F32) + bf_ref[...]
    z = -fl
    ls = -(jnp.maximum(z, 0.0) + jnp.log1p(jnp.exp(-jnp.abs(z))))
    hi, mid, lo = _split3(ls)
    pieces = jnp.where(lane < N_HEADS, hi, jnp.where(lane < 2 * N_HEADS, mid, jnp.where(
        lane < 3 * N_HEADS, lo, 0.0))).astype(BF16)
    r = lax.broadcasted_iota(jnp.int32, (tm, tm), 0)
    c = lax.broadcasted_iota(jnp.int32, (tm, tm), 1)
    tri = jnp.where(c <= r, 1.0, 0.0).astype(BF16)
    cs = jnp.dot(tri, pieces, preferred_element_type=F32)
    tot = cs + pltpu.roll(cs, LANES - N_HEADS, axis=1) + pltpu.roll(cs, LANES - 2 * N_HEADS, axis=1)
    dc = jnp.where(lane < N_HEADS, tot + carry_scr[...], 0.0)
    carry_scr[...] = dc[tm - 1:tm, :]
    dsc = dc * LOG2E
    d_out[...] = dsc

    kg2 = kg_ref[...]
    ones_rows = jnp.where(lax.broadcasted_iota(jnp.int32, (HEAD_DIM, tm), 0) == 0, 1.0, 0.0)
    for c4 in range(d // MXU_N):
        kc = jnp.dot(h_scr[...], wk_ref[:, c4 * MXU_N:(c4 + 1) * MXU_N], preferred_element_type=F32)
        vc = jnp.dot(h_scr[...], wv_ref[:, c4 * MXU_N:(c4 + 1) * MXU_N], preferred_element_type=F32)
        for half in range(2):
            pair = 2 * c4 + half
            kn = _head_pair_norm(kc[:, half * LANES:(half + 1) * LANES], kg2)
            vt = vc[:, half * LANES:(half + 1) * LANES].T
            for sub, src in ((0, kn), (1, pltpu.roll(kn, HEAD_DIM, axis=1))):
                hd = 2 * pair + sub
                k_out[hd] = _augment(src, dsc[:, hd:hd + 1], sign_first=False)
                vt_out[hd] = jnp.concatenate(
                    [vt[sub * HEAD_DIM:(sub + 1) * HEAD_DIM, :], ones_rows], axis=0).astype(BF16)


def _kv_project(x, kvmod, norm_g, w_k, w_v, w_f3, b_f3, kg2):
    b, s, d = x.shape
    tm = TOKEN_TILE
    return pl.pallas_call(
        _kv_body,
        grid=(b, s // tm),
        in_specs=[
            pl.BlockSpec((None, tm, d), lambda i, j: (i, j, 0)),
            pl.BlockSpec((None, 2, d), lambda i, j: (i, 0, 0)),
            _const_spec((1, d)),
            _const_spec((d, d)), _const_spec((d, d)), _const_spec((d, LANES)),
            _const_spec((1, LANES)), _const_spec((1, LANES)),
        ],
        out_specs=[
            pl.BlockSpec((None, N_HEADS, tm, LANES), lambda i, j: (i, 0, j, 0)),
            pl.BlockSpec((None, N_HEADS, None, LANES, tm), lambda i, j: (i, 0, j, 0, 0)),
            pl.BlockSpec((None, tm, LANES), lambda i, j: (i, j, 0)),
        ],
        out_shape=[
            jax.ShapeDtypeStruct((b, N_HEADS, s, LANES), BF16),
            jax.ShapeDtypeStruct((b, N_HEADS, s // tm, LANES, tm), BF16),
            jax.ShapeDtypeStruct((b, s, LANES), F32),
        ],
        scratch_shapes=[pltpu.VMEM((tm, d), BF16), pltpu.VMEM((1, LANES), F32)],
        compiler_params=_params(2),
        name="kv_project",
    )(x, kvmod, norm_g, w_k, w_v, w_f3, b_f3, kg2)


def _q_body(x_ref, mod_ref, pre_ref, wq_ref, qg_ref, d_ref, q_out, h_scr):
    tm, d = x_ref.shape
    h_scr[...] = _prenorm(x_ref[...], pre_ref[...], mod_ref[0:1, :], mod_ref[1:2, :]).astype(BF16)
    dsc = d_ref[...]
    qg2 = qg_ref[...]
    for c4 in range(d // MXU_N):
        qc = jnp.dot(h_scr[...], wq_ref[:, c4 * MXU_N:(c4 + 1) * MXU_N], preferred_element_type=F32)
        for half in range(2):
            pair = 2 * c4 + half
            qn = _head_pair_norm(qc[:, half * LANES:(half + 1) * LANES], qg2)
            for sub, src in ((0, qn), (1, pltpu.roll(qn, HEAD_DIM, axis=1))):
                hd = 2 * pair + sub
                q_out[hd] = _augment(src, dsc[:, hd:hd + 1], sign_first=True)


def _q_project(x, mod, pre_g, w_q, qg2, dsc):
    b, s, d = x.shape
    tm = TOKEN_TILE
    return pl.pallas_call(
        _q_body,
        grid=(b, s // tm),
        in_specs=[
            pl.BlockSpec((None, tm, d), lambda i, j: (i, j, 0)),
            pl.BlockSpec((None, 6, d), lambda i, j: (i, 0, 0)),
            _const_spec((1, d)), _const_spec((d, d)), _const_spec((1, LANES)),
            pl.BlockSpec((None, tm, LANES), lambda i, j: (i, j, 0)),
        ],
        out_specs=pl.BlockSpec((None, N_HEADS, tm, LANES), lambda i, j: (i, 0, j, 0)),
        out_shape=jax.ShapeDtypeStruct((b, N_HEADS, s, LANES), BF16),
        scratch_shapes=[pltpu.VMEM((tm, d), BF16)],
        compiler_params=_params(2),
        name="q_project",
    )(x, mod, pre_g, w_q, qg2, dsc)


def _attn_body(q_ref, k_ref, vt_ref, o_ref, acc_scr):
    t = q_ref.shape[0]
    qi = pl.program_id(2)
    q = q_ref[...]

    def step(j, m, l, diagonal):
        k = k_ref[pl.ds(pl.multiple_of(j * t, t), t), :]
        s = lax.dot_general(k, q, (((1,), (1,)), ((), ())), preferred_element_type=F32)
        if diagonal:
            kpos = lax.broadcasted_iota(jnp.int32, (t, t), 0)
            qpos = lax.broadcasted_iota(jnp.int32, (t, t), 1)
            s = jnp.where(kpos <= qpos, s, -jnp.inf)
        m_new = jnp.maximum(m, jnp.max(s, axis=0, keepdims=True))
        alpha = jnp.exp2(m - m_new)
        p = jnp.exp2(s - m_new)
        l_new = alpha * l + jnp.sum(p, axis=0, keepdims=True)
        pv = jnp.dot(vt_ref[j], p.astype(BF16), preferred_element_type=F32)
        acc_scr[...] = alpha * acc_scr[...] + pv
        return m_new, l_new

    acc_scr[...] = jnp.zeros_like(acc_scr)
    m0 = jnp.full((1, t), -jnp.inf, F32)
    l0 = jnp.zeros((1, t), F32)
    m, l = lax.fori_loop(0, qi, lambda j, c: step(j, c[0], c[1], False), (m0, l0))
    m, l = step(qi, m, l, True)
    o_ref[...] = acc_scr[0:HEAD_DIM, :] / l


def _fox_attention(q_aug, k_aug, vt_aug):
    b, h, s, _ = q_aug.shape
    t = TOKEN_TILE
    nq = s // t
    return pl.pallas_call(
        _attn_body,
        grid=(b, h, nq),
        in_specs=[
            pl.BlockSpec((None, None, t, LANES), lambda i, j, q: (i, j, q, 0)),
            pl.BlockSpec((None, None, s, LANES), lambda i, j, q: (i, j, 0, 0)),
            pl.BlockSpec((None, None, nq, LANES, t), lambda i, j, q: (i, j, 0, 0, 0)),
        ],
        out_specs=pl.BlockSpec((None, None, None, HEAD_DIM, t), lambda i, j, q: (i, j, q, 0, 0)),
        out_shape=jax.ShapeDtypeStruct((b, h, nq, HEAD_DIM, t), F32),
        scratch_shapes=[pltpu.VMEM((LANES, t), F32)],
        compiler_params=_params(3),
        name="fox_attention",
    )(q_aug, k_aug, vt_aug)


def _out_body(x_ref, mod_ref, pre_ref, post_ref, ot_ref, wg_ref, wo_ref, o_ref):
    tm, d = x_ref.shape
    x = x_ref[...]
    h = _prenorm(x, pre_ref[...], mod_ref[0:1, :], mod_ref[1:2, :]).astype(BF16)
    gate = _sigmoid(jnp.dot(h, wg_ref[...], preferred_element_type=F32))
    o = ot_ref[...].reshape(d, tm).T
    y = jnp.dot((o * gate).astype(BF16), wo_ref[...], preferred_element_type=F32)
    o_ref[...] = x + mod_ref[2:3, :] * _rms(y, post_ref[...])


def _fox_output(x, mod, pre_g, post_g, o_t, w_g, w_o):
    b, s, d = x.shape
    tm = TOKEN_TILE
    tok = pl.BlockSpec((None, tm, d), lambda i, j: (i, j, 0))
    return pl.pallas_call(
        _out_body,
        grid=(b, s // tm),
        in_specs=[
            tok,
            pl.BlockSpec((None, 6, d), lambda i, j: (i, 0, 0)),
            _const_spec((1, d)), _const_spec((1, d)),
            pl.BlockSpec((None, N_HEADS, None, HEAD_DIM, tm), lambda i, j: (i, 0, j, 0, 0)),
            _const_spec((d, d)), _const_spec((d, d)),
        ],
        out_specs=tok,
        out_shape=jax.ShapeDtypeStruct((b, s, d), F32),
        compiler_params=_params(2),
        name="fox_output",
    )(x, mod, pre_g, post_g, o_t, w_g, w_o)


def kernel(x, c, ada_w, ada_b, pre_mix_g, post_mix_g, pre_ffn_g, post_ffn_g, ffn_w_gu, ffn_w_down,
           a_w_in, a_b_in, a_ln_g, a_ln_b, a_w_s, a_b_s, a_w_out, kv_ada_w, kv_ada_b, kv_norm_g,
           kv_w, kv_b_f, k_norm_g, b_w_qg, b_q_norm_g, b_w_o):
    b, s, d = x.shape
    depth = ada_w.shape[0]
    n_a = a_w_in.shape[0]
    assert d == N_HEADS * HEAD_DIM and s % TOKEN_TILE == 0

    c8 = jnp.pad(c, ((0, 8 - b), (0, 0)))
    mods = _modulation(c8, ada_w, ada_b)[:, :b].reshape(depth, b, 6, d)
    kvmod = _modulation(c8, kv_ada_w[None], kv_ada_b[None])[0, :b].reshape(b, 2, d)

    row = lambda v: v.reshape(1, -1)
    q_scale = LOG2E * HEAD_DIM ** -0.5

    for layer in range(depth):
        mod = mods[layer]
        if layer < n_a:
            i = layer
            bs_full = jnp.repeat(a_b_s[i].T, CHUNK, axis=1)
            x = _gmlp_layer(x, mod, row(pre_mix_g[layer]), row(post_mix_g[layer]),
                            a_w_in[i].astype(BF16), row(a_b_in[i]), row(a_ln_g[i]), row(a_ln_b[i]),
                            a_w_s[i], bs_full, a_w_out[i].astype(BF16))
        else:
            j = layer - n_a
            qg2 = row(jnp.tile(b_q_norm_g[j] * q_scale, 2))
            q_aug = _q_project(x, mod, row(pre_mix_g[layer]), b_w_qg[j][:, :d].astype(BF16), qg2, dsc)
            o_t = _fox_attention(q_aug, k_aug, vt_aug)
            x = _fox_output(x, mod, row(pre_mix_g[layer]), row(post_mix_g[layer]), o_t,
                            b_w_qg[j][:, d:].astype(BF16), b_w_o[j].astype(BF16))
        x = _ffn_layer(x, mod, row(pre_ffn_g[layer]), row(post_ffn_g[layer]),
                       ffn_w_gu[layer].astype(BF16), ffn_w_down[layer].astype(BF16))
        if layer == n_a - 1:
            w_f = kv_w[:, 2 * d:]
            pad = jnp.zeros((d, LANES - 3 * N_HEADS), F32)
            w_f3 = jnp.concatenate([w_f, w_f, w_f, pad], axis=1).astype(BF16)
            b_f3 = row(jnp.concatenate([kv_b_f, kv_b_f, kv_b_f, jnp.zeros((LANES - 3 * N_HEADS,), F32)]))
            k_aug, vt_aug, dsc = _kv_project(
                x, kvmod, row(kv_norm_g), kv_w[:, :d].astype(BF16), kv_w[:, d:2 * d].astype(BF16),
                w_f3, b_f3, row(jnp.tile(k_norm_g, 2)))
    return x
```

```python
import functools

import jax
import jax.numpy as jnp
from jax import lax
from jax.experimental import pallas as pl
from jax.experimental.pallas import tpu as pltpu

F32 = jnp.float32
BF16 = jnp.bfloat16

EPS = 1e-6
N_HEADS = 16
HEAD_DIM = 64
CHUNK = 128
GROUPS = 16
LOG2E = 1.4426950408889634
LANES = 128
MXU_N = 256
TOKEN_TILE = 512
ATTN_Q_TILE = 2 * TOKEN_TILE
VMEM_LIMIT = 56 * 1024 * 1024


def _sigmoid(x):
    return 1.0 / (1.0 + jnp.exp(-x))


def _rms(x, g):
    return x * lax.rsqrt(jnp.mean(x * x, axis=-1, keepdims=True) + EPS) * g


def _prenorm(x, g, shift, scale):
    return _rms(x, g) * (1.0 + scale) + shift


def _split3(x):
    hi = x.astype(BF16).astype(F32)
    r = x - hi
    mid = r.astype(BF16).astype(F32)
    lo = (r - mid).astype(BF16).astype(F32)
    return hi, mid, lo


def _const_spec(shape):
    n = len(shape)
    return pl.BlockSpec(shape, lambda *_: (0,) * n, pipeline_mode=pl.Buffered(1))


def _params(n_grid):
    return pltpu.CompilerParams(
        dimension_semantics=("arbitrary",) * n_grid, vmem_limit_bytes=VMEM_LIMIT)


def _mod_body(c_ref, w_ref, b_ref, o_ref):
    c = c_ref[...]
    ca = (c * _sigmoid(c)).astype(BF16)
    o_ref[...] = jnp.dot(ca, w_ref[...].astype(BF16), preferred_element_type=F32) + b_ref[...]


def _modulation(c8, w, b):
    n_l, d, n = w.shape
    nb = 1024
    return pl.pallas_call(
        _mod_body,
        grid=(n_l, n // nb),
        in_specs=[
            pl.BlockSpec((8, d), lambda l, j: (0, 0)),
            pl.BlockSpec((None, d, nb), lambda l, j: (l, 0, j)),
            pl.BlockSpec((None, 1, nb), lambda l, j: (l, 0, j)),
        ],
        out_specs=pl.BlockSpec((None, 8, nb), lambda l, j: (l, 0, j)),
        out_shape=jax.ShapeDtypeStruct((n_l, 8, n), F32),
        compiler_params=_params(2),
        name="modulation",
    )(c8, w, b.reshape(n_l, 1, n))


def _gmlp_body(x_ref, mod_ref, pre_ref, post_ref, win_ref, bin_ref, lng_ref, lnb_ref, ws_ref,
               bs_ref, wout_ref, o_ref, h_scr, u_scr, v_scr, vn_scr, y_scr):
    tm, d = x_ref.shape
    gw = u_scr.shape[1]
    x = x_ref[...]
    h_scr[...] = _prenorm(x, pre_ref[...], mod_ref[0:1, :], mod_ref[1:2, :]).astype(BF16)

    k_gelu = 0.7978845608028654
    nc = 512
    for c0 in range(0, 2 * gw, nc):
        z = jnp.dot(h_scr[...], win_ref[:, c0:c0 + nc], preferred_element_type=F32)
        z = z + bin_ref[:, c0:c0 + nc]
        z = z * (0.5 * (1.0 + jnp.tanh(k_gelu * (z + 0.044715 * (z * z * z)))))
        if c0 < gw:
            u_scr[:, c0:c0 + nc] = z
        else:
            v_scr[:, c0 - gw:c0 - gw + nc] = z

    v = v_scr[...]
    mu = jnp.mean(v, axis=-1, keepdims=True)
    vc = v - mu
    rstd = lax.rsqrt(jnp.mean(vc * vc, axis=-1, keepdims=True) + EPS)
    vn_scr[...] = (vc * rstd * lng_ref[...] + lnb_ref[...]).astype(BF16)

    n_chunks = tm // CHUNK
    row = lax.broadcasted_iota(jnp.int32, (CHUNK, CHUNK), 0)
    col = lax.broadcasted_iota(jnp.int32, (CHUNK, CHUNK), 1)
    causal = col <= row
    for g in range(GROUPS):
        gs = slice(g * CHUNK, (g + 1) * CHUNK)
        ws = jnp.where(causal, ws_ref[g], 0.0).astype(BF16)
        rhs = jnp.concatenate(
            [vn_scr[c * CHUNK:(c + 1) * CHUNK, gs] for c in range(n_chunks)], axis=1)
        sp = jnp.dot(ws, rhs, preferred_element_type=F32)
        for c in range(n_chunks):
            cs = slice(c * CHUNK, (c + 1) * CHUNK)
            y_scr[cs, gs] = (u_scr[cs, gs] * (sp[:, cs] + bs_ref[:, gs])).astype(BF16)

    y = jnp.dot(y_scr[...], wout_ref[...], preferred_element_type=F32)
    o_ref[...] = x + mod_ref[2:3, :] * _rms(y, post_ref[...])


def _gmlp_layer(x, mod, pre_g, post_g, w_in, b_in, ln_g, ln_b, w_s, bs_full, w_out):
    b, s, d = x.shape
    gw = w_out.shape[0]
    tm = TOKEN_TILE
    tok = pl.BlockSpec((None, tm, d), lambda i, j: (i, j, 0))
    return pl.pallas_call(
        _gmlp_body,
        grid=(b, s // tm),
        in_specs=[
            tok,
            pl.BlockSpec((None, 6, d), lambda i, j: (i, 0, 0)),
            _const_spec((1, d)), _const_spec((1, d)),
            _const_spec((d, 2 * gw)), _const_spec((1, 2 * gw)),
            _const_spec((1, gw)), _const_spec((1, gw)),
            _const_spec((GROUPS, CHUNK, CHUNK)), _const_spec((CHUNK, gw)),
            _const_spec((gw, d)),
        ],
        out_specs=tok,
        out_shape=jax.ShapeDtypeStruct((b, s, d), F32),
        scratch_shapes=[
            pltpu.VMEM((tm, d), BF16), pltpu.VMEM((tm, gw), F32), pltpu.VMEM((tm, gw), F32),
            pltpu.VMEM((tm, gw), BF16), pltpu.VMEM((tm, gw), BF16),
        ],
        compiler_params=_params(2),
        name="gmlp_mixer",
    )(x, mod, pre_g, post_g, w_in, b_in, ln_g, ln_b, w_s, bs_full, w_out)


def _ffn_body(x_ref, mod_ref, pre_ref, post_ref, wgu_ref, wd_ref, o_ref, h_scr, a_scr):
    f = a_scr.shape[1]
    x = x_ref[...]
    h_scr[...] = _prenorm(x, pre_ref[...], mod_ref[3:4, :], mod_ref[4:5, :]).astype(BF16)
    for c0 in range(0, f, MXU_N):
        g = jnp.dot(h_scr[...], wgu_ref[:, c0:c0 + MXU_N], preferred_element_type=F32)
        u = jnp.dot(h_scr[...], wgu_ref[:, f + c0:f + c0 + MXU_N], preferred_element_type=F32)
        a_scr[:, c0:c0 + MXU_N] = (g * _sigmoid(g) * u).astype(BF16)
    y = jnp.dot(a_scr[...], wd_ref[...], preferred_element_type=F32)
    o_ref[...] = x + mod_ref[5:6, :] * _rms(y, post_ref[...])


def _ffn_layer(x, mod, pre_g, post_g, w_gu, w_down):
    b, s, d = x.shape
    f = w_down.shape[0]
    tm = TOKEN_TILE
    tok = pl.BlockSpec((None, tm, d), lambda i, j: (i, j, 0))
    return pl.pallas_call(
        _ffn_body,
        grid=(b, s // tm),
        in_specs=[
            tok,
            pl.BlockSpec((None, 6, d), lambda i, j: (i, 0, 0)),
            _const_spec((1, d)), _const_spec((1, d)),
            _const_spec((d, 2 * f)), _const_spec((f, d)),
        ],
        out_specs=tok,
        out_shape=jax.ShapeDtypeStruct((b, s, d), F32),
        scratch_shapes=[pltpu.VMEM((tm, d), BF16), pltpu.VMEM((tm, f), BF16)],
        compiler_params=_params(2),
        name="swiglu_ffn",
    )(x, mod, pre_g, post_g, w_gu, w_down)


def _head_pair_norm(t, gain2):
    lane = lax.broadcasted_iota(jnp.int32, t.shape, 1)
    sq = t * t
    first = lane < HEAD_DIM
    s0 = jnp.sum(jnp.where(first, sq, 0.0), axis=-1, keepdims=True)
    s1 = jnp.sum(jnp.where(first, 0.0, sq), axis=-1, keepdims=True)
    rs = jnp.where(first, lax.rsqrt(s0 / HEAD_DIM + EPS), lax.rsqrt(s1 / HEAD_DIM + EPS))
    return t * rs * gain2


def _augment(t, dcol, sign_first):
    lane = lax.broadcasted_iota(jnp.int32, t.shape, 1)
    hi, mid, lo = _split3(jnp.broadcast_to(dcol, t.shape))
    one = jnp.ones_like(t)
    if sign_first:
        a, b, c, e, f, g = hi, mid, lo, one, one, one
    else:
        a, b, c, e, f, g = one, one, one, -hi, -mid, -lo
    tail = jnp.where(lane == HEAD_DIM, a, jnp.where(lane == HEAD_DIM + 1, b, jnp.where(
        lane == HEAD_DIM + 2, c, jnp.where(lane == HEAD_DIM + 3, e, jnp.where(
            lane == HEAD_DIM + 4, f, jnp.where(lane == HEAD_DIM + 5, g, 0.0))))))
    return jnp.where(lane < HEAD_DIM, t, tail).astype(BF16)


def _kv_body(x_ref, mod_ref, g_ref, wk_ref, wv_ref, wf_ref, bf_ref, kg_ref,
             k_out, vt_out, d_out, h_scr, carry_scr):
    tm, d = x_ref.shape

    @pl.when(pl.program_id(1) == 0)
    def _():
        carry_scr[...] = jnp.zeros_like(carry_scr)

    h_scr[...] = _prenorm(x_ref[...], g_ref[...], mod_ref[0:1, :], mod_ref[1:2, :]).astype(BF16)

    lane = lax.broadcasted_iota(jnp.int32, (tm, LANES), 1)
    fl = jnp.dot(h_scr[...], wf_ref[...], preferred_element_type=F32) + bf_ref[...]
    z = -fl
    ls = -(jnp.maximum(z, 0.0) + jnp.log1p(jnp.exp(-jnp.abs(z))))
    hi, mid, lo = _split3(ls)
    pieces = jnp.where(lane < N_HEADS, hi, jnp.where(lane < 2 * N_HEADS, mid, jnp.where(
        lane < 3 * N_HEADS, lo, 0.0))).astype(BF16)
    r = lax.broadcasted_iota(jnp.int32, (tm, tm), 0)
    c = lax.broadcasted_iota(jnp.int32, (tm, tm), 1)
    tri = jnp.where(c <= r, 1.0, 0.0).astype(BF16)
    cs = jnp.dot(tri, pieces, preferred_element_type=F32)
    tot = cs + pltpu.roll(cs, LANES - N_HEADS, axis=1) + pltpu.roll(cs, LANES - 2 * N_HEADS, axis=1)
    dc = jnp.where(lane < N_HEADS, tot + carry_scr[...], 0.0)
    carry_scr[...] = dc[tm - 1:tm, :]
    dsc = dc * LOG2E
    d_out[...] = dsc

    kg2 = kg_ref[...]
    ones_rows = jnp.where(lax.broadcasted_iota(jnp.int32, (HEAD_DIM, tm), 0) == 0, 1.0, 0.0)
    for c4 in range(d // MXU_N):
        kc = jnp.dot(h_scr[...], wk_ref[:, c4 * MXU_N:(c4 + 1) * MXU_N], preferred_element_type=F32)
        vc = jnp.dot(h_scr[...], wv_ref[:, c4 * MXU_N:(c4 + 1) * MXU_N], preferred_element_type=F32)
        for half in range(2):
            pair = 2 * c4 + half
            kn = _head_pair_norm(kc[:, half * LANES:(half + 1) * LANES], kg2)
            vt = vc[:, half * LANES:(half + 1) * LANES].T
            for sub, src in ((0, kn), (1, pltpu.roll(kn, HEAD_DIM, axis=1))):
                hd = 2 * pair + sub
                k_out[hd] = _augment(src, dsc[:, hd:hd + 1], sign_first=False)
                vt_out[hd] = jnp.concatenate(
                    [vt[sub * HEAD_DIM:(sub + 1) * HEAD_DIM, :], ones_rows], axis=0).astype(BF16)


def _kv_project(x, kvmod, norm_g, w_k, w_v, w_f3, b_f3, kg2):
    b, s, d = x.shape
    tm = TOKEN_TILE
    return pl.pallas_call(
        _kv_body,
        grid=(b, s // tm),
        in_specs=[
            pl.BlockSpec((None, tm, d), lambda i, j: (i, j, 0)),
            pl.BlockSpec((None, 2, d), lambda i, j: (i, 0, 0)),
            _const_spec((1, d)),
            _const_spec((d, d)), _const_spec((d, d)), _const_spec((d, LANES)),
            _const_spec((1, LANES)), _const_spec((1, LANES)),
        ],
        out_specs=[
            pl.BlockSpec((None, N_HEADS, tm, LANES), lambda i, j: (i, 0, j, 0)),
            pl.BlockSpec((None, N_HEADS, None, LANES, tm), lambda i, j: (i, 0, j, 0, 0)),
            pl.BlockSpec((None, tm, LANES), lambda i, j: (i, j, 0)),
        ],
        out_shape=[
            jax.ShapeDtypeStruct((b, N_HEADS, s, LANES), BF16),
            jax.ShapeDtypeStruct((b, N_HEADS, s // tm, LANES, tm), BF16),
            jax.ShapeDtypeStruct((b, s, LANES), F32),
        ],
        scratch_shapes=[pltpu.VMEM((tm, d), BF16), pltpu.VMEM((1, LANES), F32)],
        compiler_params=_params(2),
        name="kv_project",
    )(x, kvmod, norm_g, w_k, w_v, w_f3, b_f3, kg2)


def _q_body(x_ref, mod_ref, pre_ref, wq_ref, qg_ref, d_ref, q_out, h_scr):
    tm, d = x_ref.shape
    h_scr[...] = _prenorm(x_ref[...], pre_ref[...], mod_ref[0:1, :], mod_ref[1:2, :]).astype(BF16)
    dsc = d_ref[...]
    qg2 = qg_ref[...]
    for c4 in range(d // MXU_N):
        qc = jnp.dot(h_scr[...], wq_ref[:, c4 * MXU_N:(c4 + 1) * MXU_N], preferred_element_type=F32)
        for half in range(2):
            pair = 2 * c4 + half
            qn = _head_pair_norm(qc[:, half * LANES:(half + 1) * LANES], qg2)
            for sub, src in ((0, qn), (1, pltpu.roll(qn, HEAD_DIM, axis=1))):
                hd = 2 * pair + sub
                q_out[hd] = _augment(src, dsc[:, hd:hd + 1], sign_first=True)


def _q_project(x, mod, pre_g, w_q, qg2, dsc):
    b, s, d = x.shape
    tm = TOKEN_TILE
    return pl.pallas_call(
        _q_body,
        grid=(b, s // tm),
        in_specs=[
            pl.BlockSpec((None, tm, d), lambda i, j: (i, j, 0)),
            pl.BlockSpec((None, 6, d), lambda i, j: (i, 0, 0)),
            _const_spec((1, d)), _const_spec((d, d)), _const_spec((1, LANES)),
            pl.BlockSpec((None, tm, LANES), lambda i, j: (i, j, 0)),
        ],
        out_specs=pl.BlockSpec((None, N_HEADS, tm, LANES), lambda i, j: (i, 0, j, 0)),
        out_shape=jax.ShapeDtypeStruct((b, N_HEADS, s, LANES), BF16),
        scratch_shapes=[pltpu.VMEM((tm, d), BF16)],
        compiler_params=_params(2),
        name="q_project",
    )(x, mod, pre_g, w_q, qg2, dsc)


def _attn_body(q_ref, k_ref, vt_ref, o_ref, acc_scr, s0_scr, s1_scr):
    tq = q_ref.shape[0]
    tk = vt_ref.shape[2]
    assert tq == 2 * tk
    qi = pl.program_id(2)
    q = q_ref[...]
    kpos = lax.broadcasted_iota(jnp.int32, (tk, tq), 0)
    qpos = lax.broadcasted_iota(jnp.int32, (tk, tq), 1)

    def scores(j, buf, valid):
        k = k_ref[pl.ds(pl.multiple_of(j * tk, tk), tk), :]
        s = lax.dot_general(k, q, (((1,), (1,)), ((), ())), preferred_element_type=F32)
        if valid is not None:
            s = jnp.where(valid, s, -jnp.inf)
        buf[...] = s
        return jnp.max(s, axis=0, keepdims=True)

    def accumulate(j, buf, cmax, m, l):
        m_new = jnp.maximum(m, cmax)
        alpha = jnp.exp2(m - m_new)
        p = jnp.exp2(buf[...] - m_new)
        l_new = alpha * l + jnp.sum(p, axis=0, keepdims=True)
        pv = jnp.dot(vt_ref[j], p.astype(BF16), preferred_element_type=F32)
        acc_scr[...] = alpha * acc_scr[...] + pv
        return m_new, l_new

    def pair(p, carry, second_valid):
        m, l, cm0 = carry
        cm1 = scores(2 * p + 1, s1_scr, None)
        m, l = accumulate(2 * p, s0_scr, cm0, m, l)
        cm0 = scores(2 * p + 2, s0_scr, second_valid)
        m, l = accumulate(2 * p + 1, s1_scr, cm1, m, l)
        return m, l, cm0

    acc_scr[...] = jnp.zeros_like(acc_scr)
    m0 = jnp.full((1, tq), -jnp.inf, F32)
    l0 = jnp.zeros((1, tq), F32)
    cm0 = scores(0, s0_scr, kpos - qpos <= qi * tq)
    carry = lax.fori_loop(0, qi - 1, lambda p, c: pair(p, c, None), (m0, l0, cm0))
    carry = lax.cond(qi > 0, lambda c: pair(qi - 1, c, kpos <= qpos), lambda c: c, carry)
    m, l, cm0 = carry
    cm1 = scores(2 * qi + 1, s1_scr, kpos + tk <= qpos)
    m, l = accumulate(2 * qi, s0_scr, cm0, m, l)
    m, l = accumulate(2 * qi + 1, s1_scr, cm1, m, l)
    o = acc_scr[0:HEAD_DIM, :] / l
    o_ref[0] = o[:, :tk]
    o_ref[1] = o[:, tk:]


def _fox_attention(q_aug, k_aug, vt_aug):
    b, h, s, _ = q_aug.shape
    tk = TOKEN_TILE
    tq = ATTN_Q_TILE
    nk = s // tk
    return pl.pallas_call(
        _attn_body,
        grid=(b, h, s // tq),
        in_specs=[
            pl.BlockSpec((None, None, tq, LANES), lambda i, j, q: (i, j, q, 0)),
            pl.BlockSpec((None, None, s, LANES), lambda i, j, q: (i, j, 0, 0)),
            pl.BlockSpec((None, None, nk, LANES, tk), lambda i, j, q: (i, j, 0, 0, 0)),
        ],
        out_specs=pl.BlockSpec((None, None, tq // tk, HEAD_DIM, tk), lambda i, j, q: (i, j, q, 0, 0)),
        out_shape=jax.ShapeDtypeStruct((b, h, nk, HEAD_DIM, tk), F32),
        scratch_shapes=[pltpu.VMEM((LANES, tq), F32), pltpu.VMEM((tk, tq), F32),
                        pltpu.VMEM((tk, tq), F32)],
        compiler_params=_params(3),
        name="fox_attention",
    )(q_aug, k_aug, vt_aug)


def _out_body(x_ref, mod_ref, pre_ref, post_ref, ot_ref, wg_ref, wo_ref, o_ref):
    tm, d = x_ref.shape
    x = x_ref[...]
    h = _prenorm(x, pre_ref[...], mod_ref[0:1, :], mod_ref[1:2, :]).astype(BF16)
    gate = _sigmoid(jnp.dot(h, wg_ref[...], preferred_element_type=F32))
    o = ot_ref[...].reshape(d, tm).T
    y = jnp.dot((o * gate).astype(BF16), wo_ref[...], preferred_element_type=F32)
    o_ref[...] = x + mod_ref[2:3, :] * _rms(y, post_ref[...])


def _fox_output(x, mod, pre_g, post_g, o_t, w_g, w_o):
    b, s, d = x.shape
    tm = TOKEN_TILE
    tok = pl.BlockSpec((None, tm, d), lambda i, j: (i, j, 0))
    return pl.pallas_call(
        _out_body,
        grid=(b, s // tm),
        in_specs=[
            tok,
            pl.BlockSpec((None, 6, d), lambda i, j: (i, 0, 0)),
            _const_spec((1, d)), _const_spec((1, d)),
            pl.BlockSpec((None, N_HEADS, None, HEAD_DIM, tm), lambda i, j: (i, 0, j, 0, 0)),
            _const_spec((d, d)), _const_spec((d, d)),
        ],
        out_specs=tok,
        out_shape=jax.ShapeDtypeStruct((b, s, d), F32),
        compiler_params=_params(2),
        name="fox_output",
    )(x, mod, pre_g, post_g, o_t, w_g, w_o)


def kernel(x, c, ada_w, ada_b, pre_mix_g, post_mix_g, pre_ffn_g, post_ffn_g, ffn_w_gu, ffn_w_down,
           a_w_in, a_b_in, a_ln_g, a_ln_b, a_w_s, a_b_s, a_w_out, kv_ada_w, kv_ada_b, kv_norm_g,
           kv_w, kv_b_f, k_norm_g, b_w_qg, b_q_norm_g, b_w_o):
    b, s, d = x.shape
    depth = ada_w.shape[0]
    n_a = a_w_in.shape[0]
    assert d == N_HEADS * HEAD_DIM and s % ATTN_Q_TILE == 0

    c8 = jnp.pad(c, ((0, 8 - b), (0, 0)))
    mods = _modulation(c8, ada_w, ada_b)[:, :b].reshape(depth, b, 6, d)
    kvmod = _modulation(c8, kv_ada_w[None], kv_ada_b[None])[0, :b].reshape(b, 2, d)

    row = lambda v: v.reshape(1, -1)
    q_scale = LOG2E * HEAD_DIM ** -0.5

    for layer in range(depth):
        mod = mods[layer]
        if layer < n_a:
            i = layer
            bs_full = jnp.repeat(a_b_s[i].T, CHUNK, axis=1)
            x = _gmlp_layer(x, mod, row(pre_mix_g[layer]), row(post_mix_g[layer]),
                            a_w_in[i].astype(BF16), row(a_b_in[i]), row(a_ln_g[i]), row(a_ln_b[i]),
                            a_w_s[i], bs_full, a_w_out[i].astype(BF16))
        else:
            j = layer - n_a
            qg2 = row(jnp.tile(b_q_norm_g[j] * q_scale, 2))
            q_aug = _q_project(x, mod, row(pre_mix_g[layer]), b_w_qg[j][:, :d].astype(BF16), qg2, dsc)
            o_t = _fox_attention(q_aug, k_aug, vt_aug)
            x = _fox_output(x, mod, row(pre_mix_g[layer]), row(post_mix_g[layer]), o_t,
                            b_w_qg[j][:, d:].astype(BF16), b_w_o[j].astype(BF16))
        x = _ffn_layer(x, mod, row(pre_ffn_g[layer]), row(post_ffn_g[layer]),
                       ffn_w_gu[layer].astype(BF16), ffn_w_down[layer].astype(BF16))
        if layer == n_a - 1:
            w_f = kv_w[:, 2 * d:]
            pad = jnp.zeros((d, LANES - 3 * N_HEADS), F32)
            w_f3 = jnp.concatenate([w_f, w_f, w_f, pad], axis=1).astype(BF16)
            b_f3 = row(jnp.concatenate([kv_b_f, kv_b_f, kv_b_f, jnp.zeros((LANES - 3 * N_HEADS,), F32)]))
            k_aug, vt_aug, dsc = _kv_project(
                x, kvmod, row(kv_norm_g), kv_w[:, :d].astype(BF16), kv_w[:, d:2 * d].astype(BF16),
                w_f3, b_f3, row(jnp.tile(k_norm_g, 2)))
    return x
```

```python
import functools

import jax
import jax.numpy as jnp
from jax import lax
from jax.experimental import pallas as pl
from jax.experimental.pallas import tpu as pltpu

F32 = jnp.float32
BF16 = jnp.bfloat16

EPS = 1e-6
N_HEADS = 16
HEAD_DIM = 64
CHUNK = 128
GROUPS = 16
LOG2E = 1.4426950408889634
LANES = 128
MXU_N = 256
TOKEN_TILE = 512
ATTN_Q_TILE = 2 * TOKEN_TILE
ATTN_HEADS_PER_STEP = 2
VMEM_LIMIT = 56 * 1024 * 1024


def _sigmoid(x):
    return 1.0 / (1.0 + jnp.exp(-x))


def _rms(x, g):
    return x * lax.rsqrt(jnp.mean(x * x, axis=-1, keepdims=True) + EPS) * g


def _prenorm(x, g, shift, scale):
    return _rms(x, g) * (1.0 + scale) + shift


def _split3(x):
    hi = x.astype(BF16).astype(F32)
    r = x - hi
    mid = r.astype(BF16).astype(F32)
    lo = (r - mid).astype(BF16).astype(F32)
    return hi, mid, lo


def _const_spec(shape):
    n = len(shape)
    return pl.BlockSpec(shape, lambda *_: (0,) * n, pipeline_mode=pl.Buffered(1))


def _params(n_grid):
    return pltpu.CompilerParams(
        dimension_semantics=("arbitrary",) * n_grid, vmem_limit_bytes=VMEM_LIMIT)


def _mod_body(c_ref, w_ref, b_ref, o_ref):
    c = c_ref[...]
    ca = (c * _sigmoid(c)).astype(BF16)
    o_ref[...] = jnp.dot(ca, w_ref[...].astype(BF16), preferred_element_type=F32) + b_ref[...]


def _modulation(c8, w, b):
    n_l, d, n = w.shape
    nb = 1024
    return pl.pallas_call(
        _mod_body,
        grid=(n_l, n // nb),
        in_specs=[
            pl.BlockSpec((8, d), lambda l, j: (0, 0)),
            pl.BlockSpec((None, d, nb), lambda l, j: (l, 0, j)),
            pl.BlockSpec((None, 1, nb), lambda l, j: (l, 0, j)),
        ],
        out_specs=pl.BlockSpec((None, 8, nb), lambda l, j: (l, 0, j)),
        out_shape=jax.ShapeDtypeStruct((n_l, 8, n), F32),
        compiler_params=_params(2),
        name="modulation",
    )(c8, w, b.reshape(n_l, 1, n))


def _gmlp_body(x_ref, mod_ref, pre_ref, post_ref, win_ref, bin_ref, lng_ref, lnb_ref, ws_ref,
               bs_ref, wout_ref, o_ref, h_scr, u_scr, v_scr, vn_scr, y_scr):
    tm, d = x_ref.shape
    gw = u_scr.shape[1]
    x = x_ref[...]
    h_scr[...] = _prenorm(x, pre_ref[...], mod_ref[0:1, :], mod_ref[1:2, :]).astype(BF16)

    k_gelu = 0.7978845608028654
    nc = 512
    for c0 in range(0, 2 * gw, nc):
        z = jnp.dot(h_scr[...], win_ref[:, c0:c0 + nc], preferred_element_type=F32)
        z = z + bin_ref[:, c0:c0 + nc]
        z = z * (0.5 * (1.0 + jnp.tanh(k_gelu * (z + 0.044715 * (z * z * z)))))
        if c0 < gw:
            u_scr[:, c0:c0 + nc] = z
        else:
            v_scr[:, c0 - gw:c0 - gw + nc] = z

    v = v_scr[...]
    mu = jnp.mean(v, axis=-1, keepdims=True)
    vc = v - mu
    rstd = lax.rsqrt(jnp.mean(vc * vc, axis=-1, keepdims=True) + EPS)
    vn_scr[...] = (vc * rstd * lng_ref[...] + lnb_ref[...]).astype(BF16)

    n_chunks = tm // CHUNK
    row = lax.broadcasted_iota(jnp.int32, (CHUNK, CHUNK), 0)
    col = lax.broadcasted_iota(jnp.int32, (CHUNK, CHUNK), 1)
    causal = col <= row
    for g in range(GROUPS):
        gs = slice(g * CHUNK, (g + 1) * CHUNK)
        ws = jnp.where(causal, ws_ref[g], 0.0).astype(BF16)
        rhs = jnp.concatenate(
            [vn_scr[c * CHUNK:(c + 1) * CHUNK, gs] for c in range(n_chunks)], axis=1)
        sp = jnp.dot(ws, rhs, preferred_element_type=F32)
        for c in range(n_chunks):
            cs = slice(c * CHUNK, (c + 1) * CHUNK)
            y_scr[cs, gs] = (u_scr[cs, gs] * (sp[:, cs] + bs_ref[:, gs])).astype(BF16)

    y = jnp.dot(y_scr[...], wout_ref[...], preferred_element_type=F32)
    o_ref[...] = x + mod_ref[2:3, :] * _rms(y, post_ref[...])


def _gmlp_layer(x, mod, pre_g, post_g, w_in, b_in, ln_g, ln_b, w_s, bs_full, w_out):
    b, s, d = x.shape
    gw = w_out.shape[0]
    tm = TOKEN_TILE
    tok = pl.BlockSpec((None, tm, d), lambda i, j: (i, j, 0))
    return pl.pallas_call(
        _gmlp_body,
        grid=(b, s // tm),
        in_specs=[
            tok,
            pl.BlockSpec((None, 6, d), lambda i, j: (i, 0, 0)),
            _const_spec((1, d)), _const_spec((1, d)),
            _const_spec((d, 2 * gw)), _const_spec((1, 2 * gw)),
            _const_spec((1, gw)), _const_spec((1, gw)),
            _const_spec((GROUPS, CHUNK, CHUNK)), _const_spec((CHUNK, gw)),
            _const_spec((gw, d)),
        ],
        out_specs=tok,
        out_shape=jax.ShapeDtypeStruct((b, s, d), F32),
        scratch_shapes=[
            pltpu.VMEM((tm, d), BF16), pltpu.VMEM((tm, gw), F32), pltpu.VMEM((tm, gw), F32),
            pltpu.VMEM((tm, gw), BF16), pltpu.VMEM((tm, gw), BF16),
        ],
        compiler_params=_params(2),
        name="gmlp_mixer",
    )(x, mod, pre_g, post_g, w_in, b_in, ln_g, ln_b, w_s, bs_full, w_out)


def _ffn_body(x_ref, mod_ref, pre_ref, post_ref, wgu_ref, wd_ref, o_ref, h_scr, a_scr):
    f = a_scr.shape[1]
    x = x_ref[...]
    h_scr[...] = _prenorm(x, pre_ref[...], mod_ref[3:4, :], mod_ref[4:5, :]).astype(BF16)
    for c0 in range(0, f, MXU_N):
        g = jnp.dot(h_scr[...], wgu_ref[:, c0:c0 + MXU_N], preferred_element_type=F32)
        u = jnp.dot(h_scr[...], wgu_ref[:, f + c0:f + c0 + MXU_N], preferred_element_type=F32)
        a_scr[:, c0:c0 + MXU_N] = (g * _sigmoid(g) * u).astype(BF16)
    y = jnp.dot(a_scr[...], wd_ref[...], preferred_element_type=F32)
    o_ref[...] = x + mod_ref[5:6, :] * _rms(y, post_ref[...])


def _ffn_layer(x, mod, pre_g, post_g, w_gu, w_down):
    b, s, d = x.shape
    f = w_down.shape[0]
    tm = TOKEN_TILE
    tok = pl.BlockSpec((None, tm, d), lambda i, j: (i, j, 0))
    return pl.pallas_call(
        _ffn_body,
        grid=(b, s // tm),
        in_specs=[
            tok,
            pl.BlockSpec((None, 6, d), lambda i, j: (i, 0, 0)),
            _const_spec((1, d)), _const_spec((1, d)),
            _const_spec((d, 2 * f)), _const_spec((f, d)),
        ],
        out_specs=tok,
        out_shape=jax.ShapeDtypeStruct((b, s, d), F32),
        scratch_shapes=[pltpu.VMEM((tm, d), BF16), pltpu.VMEM((tm, f), BF16)],
        compiler_params=_params(2),
        name="swiglu_ffn",
    )(x, mod, pre_g, post_g, w_gu, w_down)


def _head_pair_norm(t, gain2):
    lane = lax.broadcasted_iota(jnp.int32, t.shape, 1)
    sq = t * t
    first = lane < HEAD_DIM
    s0 = jnp.sum(jnp.where(first, sq, 0.0), axis=-1, keepdims=True)
    s1 = jnp.sum(jnp.where(first, 0.0, sq), axis=-1, keepdims=True)
    rs = jnp.where(first, lax.rsqrt(s0 / HEAD_DIM + EPS), lax.rsqrt(s1 / HEAD_DIM + EPS))
    return t * rs * gain2


def _augment(t, dcol, sign_first):
    lane = lax.broadcasted_iota(jnp.int32, t.shape, 1)
    hi, mid, lo = _split3(jnp.broadcast_to(dcol, t.shape))
    one = jnp.ones_like(t)
    if sign_first:
        a, b, c, e, f, g = hi, mid, lo, one, one, one
    else:
        a, b, c, e, f, g = one, one, one, -hi, -mid, -lo
    tail = jnp.where(lane == HEAD_DIM, a, jnp.where(lane == HEAD_DIM + 1, b, jnp.where(
        lane == HEAD_DIM + 2, c, jnp.where(lane == HEAD_DIM + 3, e, jnp.where(
            lane == HEAD_DIM + 4, f, jnp.where(lane == HEAD_DIM + 5, g, 0.0))))))
    return jnp.where(lane < HEAD_DIM, t, tail).astype(BF16)


def _kv_body(x_ref, mod_ref, g_ref, wk_ref, wv_ref, wf_ref, bf_ref, kg_ref,
             k_out, vt_out, d_out, h_scr, carry_scr):
    tm, d = x_ref.shape

    @pl.when(pl.program_id(1) == 0)
    def _():
        carry_scr[...] = jnp.zeros_like(carry_scr)

    h_scr[...] = _prenorm(x_ref[...], g_ref[...], mod_ref[0:1, :], mod_ref[1:2, :]).astype(BF16)

    lane = lax.broadcasted_iota(jnp.int32, (tm, LANES), 1)
    fl = jnp.dot(h_scr[...], wf_ref[...], preferred_element_type=F32) + bf_ref[...]
    z = -fl
    ls = -(jnp.maximum(z, 0.0) + jnp.log1p(jnp.exp(-jnp.abs(z))))
    hi, mid, lo = _split3(ls)
    pieces = jnp.where(lane < N_HEADS, hi, jnp.where(lane < 2 * N_HEADS, mid, jnp.where(
        lane < 3 * N_HEADS, lo, 0.0))).astype(BF16)
    r = lax.broadcasted_iota(jnp.int32, (tm, tm), 0)
    c = lax.broadcasted_iota(jnp.int32, (tm, tm), 1)
    tri = jnp.where(c <= r, 1.0, 0.0).astype(BF16)
    cs = jnp.dot(tri, pieces, preferred_element_type=F32)
    tot = cs + pltpu.roll(cs, LANES - N_HEADS, axis=1) + pltpu.roll(cs, LANES - 2 * N_HEADS, axis=1)
    dc = jnp.where(lane < N_HEADS, tot + carry_scr[...], 0.0)
    carry_scr[...] = dc[tm - 1:tm, :]
    dsc = dc * LOG2E
    d_out[...] = dsc

    kg2 = kg_ref[...]
    ones_rows = jnp.where(lax.broadcasted_iota(jnp.int32, (HEAD_DIM, tm), 0) == 0, 1.0, 0.0)
    for c4 in range(d // MXU_N):
        kc = jnp.dot(h_scr[...], wk_ref[:, c4 * MXU_N:(c4 + 1) * MXU_N], preferred_element_type=F32)
        vc = jnp.dot(h_scr[...], wv_ref[:, c4 * MXU_N:(c4 + 1) * MXU_N], preferred_element_type=F32)
        for half in range(2):
            pair = 2 * c4 + half
            kn = _head_pair_norm(kc[:, half * LANES:(half + 1) * LANES], kg2)
            vt = vc[:, half * LANES:(half + 1) * LANES].T
            for sub, src in ((0, kn), (1, pltpu.roll(kn, HEAD_DIM, axis=1))):
                hd = 2 * pair + sub
                k_out[hd] = _augment(src, dsc[:, hd:hd + 1], sign_first=False)
                vt_out[hd] = jnp.concatenate(
                    [vt[sub * HEAD_DIM:(sub + 1) * HEAD_DIM, :], ones_rows], axis=0).astype(BF16)


def _kv_project(x, kvmod, norm_g, w_k, w_v, w_f3, b_f3, kg2):
    b, s, d = x.shape
    tm = TOKEN_TILE
    return pl.pallas_call(
        _kv_body,
        grid=(b, s // tm),
        in_specs=[
            pl.BlockSpec((None, tm, d), lambda i, j: (i, j, 0)),
            pl.BlockSpec((None, 2, d), lambda i, j: (i, 0, 0)),
            _const_spec((1, d)),
            _const_spec((d, d)), _const_spec((d, d)), _const_spec((d, LANES)),
            _const_spec((1, LANES)), _const_spec((1, LANES)),
        ],
        out_specs=[
            pl.BlockSpec((None, N_HEADS, tm, LANES), lambda i, j: (i, 0, j, 0)),
            pl.BlockSpec((None, N_HEADS, None, LANES, tm), lambda i, j: (i, 0, j, 0, 0)),
            pl.BlockSpec((None, tm, LANES), lambda i, j: (i, j, 0)),
        ],
        out_shape=[
            jax.ShapeDtypeStruct((b, N_HEADS, s, LANES), BF16),
            jax.ShapeDtypeStruct((b, N_HEADS, s // tm, LANES, tm), BF16),
            jax.ShapeDtypeStruct((b, s, LANES), F32),
        ],
        scratch_shapes=[pltpu.VMEM((tm, d), BF16), pltpu.VMEM((1, LANES), F32)],
        compiler_params=_params(2),
        name="kv_project",
    )(x, kvmod, norm_g, w_k, w_v, w_f3, b_f3, kg2)


def _q_body(x_ref, mod_ref, pre_ref, wq_ref, qg_ref, d_ref, q_out, h_scr):
    tm, d = x_ref.shape
    h_scr[...] = _prenorm(x_ref[...], pre_ref[...], mod_ref[0:1, :], mod_ref[1:2, :]).astype(BF16)
    dsc = d_ref[...]
    qg2 = qg_ref[...]
    for c4 in range(d // MXU_N):
        qc = jnp.dot(h_scr[...], wq_ref[:, c4 * MXU_N:(c4 + 1) * MXU_N], preferred_element_type=F32)
        for half in range(2):
            pair = 2 * c4 + half
            qn = _head_pair_norm(qc[:, half * LANES:(half + 1) * LANES], qg2)
            for sub, src in ((0, qn), (1, pltpu.roll(qn, HEAD_DIM, axis=1))):
                hd = 2 * pair + sub
                q_out[hd] = _augment(src, dsc[:, hd:hd + 1], sign_first=True)


def _q_project(x, mod, pre_g, w_q, qg2, dsc):
    b, s, d = x.shape
    tm = TOKEN_TILE
    return pl.pallas_call(
        _q_body,
        grid=(b, s // tm),
        in_specs=[
            pl.BlockSpec((None, tm, d), lambda i, j: (i, j, 0)),
            pl.BlockSpec((None, 6, d), lambda i, j: (i, 0, 0)),
            _const_spec((1, d)), _const_spec((d, d)), _const_spec((1, LANES)),
            pl.BlockSpec((None, tm, LANES), lambda i, j: (i, j, 0)),
        ],
        out_specs=pl.BlockSpec((None, N_HEADS, tm, LANES), lambda i, j: (i, 0, j, 0)),
        out_shape=jax.ShapeDtypeStruct((b, N_HEADS, s, LANES), BF16),
        scratch_shapes=[pltpu.VMEM((tm, d), BF16)],
        compiler_params=_params(2),
        name="q_project",
    )(x, mod, pre_g, w_q, qg2, dsc)


def _attn_body(q_ref, k_ref, vt_ref, o_ref, acc_scr, *s_scr):
    n_h = q_ref.shape[0]
    tk = vt_ref.shape[3]
    tq = 2 * tk
    nq = q_ref.shape[1] // tq
    heads = range(n_h)
    kpos = lax.broadcasted_iota(jnp.int32, (tk, tq), 0)
    qpos = lax.broadcasted_iota(jnp.int32, (tk, tq), 1)

    def scores(qt, j, slot, valid):
        cms = []
        for h in heads:
            q = q_ref[h, pl.ds(pl.multiple_of(qt * tq, tq), tq), :]
            k = k_ref[h, pl.ds(pl.multiple_of(j * tk, tk), tk), :]
            s = lax.dot_general(k, q, (((1,), (1,)), ((), ())), preferred_element_type=F32)
            if valid is not None:
                s = jnp.where(valid, s, -jnp.inf)
            s_scr[2 * h + slot][...] = s
            cms.append(jnp.max(s, axis=0, keepdims=True))
        return tuple(cms)

    def accumulate(j, slot, cmax, ml):
        out = []
        for h in heads:
            m, l = ml[h]
            m_new = jnp.maximum(m, cmax[h])
            alpha = jnp.exp2(m - m_new)
            p = jnp.exp2(s_scr[2 * h + slot][...] - m_new)
            l_new = alpha * l + jnp.sum(p, axis=0, keepdims=True)
            pv = jnp.dot(vt_ref[h, j], p.astype(BF16), preferred_element_type=F32)
            acc_scr[h] = alpha * acc_scr[h] + pv
            out.append((m_new, l_new))
        return tuple(out)

    def pair(qi, p, carry, second_valid):
        ml, cm0 = carry
        cm1 = scores(qi, 2 * p + 1, 1, None)
        ml = accumulate(2 * p, 0, cm0, ml)
        cm0 = scores(qi, 2 * p + 2, 0, second_valid)
        ml = accumulate(2 * p + 1, 1, cm1, ml)
        return ml, cm0

    def query_tile(qi, cm0):
        acc_scr[...] = jnp.zeros_like(acc_scr)
        ml0 = tuple((jnp.full((1, tq), -jnp.inf, F32), jnp.zeros((1, tq), F32)) for _ in heads)
        carry = lax.fori_loop(0, qi - 1, lambda p, c: pair(qi, p, c, None), (ml0, cm0))
        carry = lax.cond(qi > 0, lambda c: pair(qi, qi - 1, c, kpos <= qpos), lambda c: c, carry)
        ml, cm0 = carry
        cm1 = scores(qi, 2 * qi + 1, 1, kpos + tk <= qpos)
        ml = accumulate(2 * qi, 0, cm0, ml)
        cm0 = scores(jnp.minimum(qi + 1, nq - 1), 0, 0, None)
        ml = accumulate(2 * qi + 1, 1, cm1, ml)
        for h in heads:
            o = acc_scr[h, 0:HEAD_DIM, :] / ml[h][1]
            o_ref[h, 2 * qi] = o[:, :tk]
            o_ref[h, 2 * qi + 1] = o[:, tk:]
        return cm0

    lax.fori_loop(0, nq, query_tile, scores(0, 0, 0, kpos <= qpos))


def _fox_attention(q_aug, k_aug, vt_aug):
    b, h, s, _ = q_aug.shape
    tk = TOKEN_TILE
    nk = s // tk
    hp = ATTN_HEADS_PER_STEP
    return pl.pallas_call(
        _attn_body,
        grid=(b, h // hp),
        in_specs=[
            pl.BlockSpec((None, hp, s, LANES), lambda i, j: (i, j, 0, 0)),
            pl.BlockSpec((None, hp, s, LANES), lambda i, j: (i, j, 0, 0)),
            pl.BlockSpec((None, hp, nk, LANES, tk), lambda i, j: (i, j, 0, 0, 0)),
        ],
        out_specs=pl.BlockSpec((None, hp, nk, HEAD_DIM, tk), lambda i, j: (i, j, 0, 0, 0)),
        out_shape=jax.ShapeDtypeStruct((b, h, nk, HEAD_DIM, tk), F32),
        scratch_shapes=[pltpu.VMEM((hp, LANES, ATTN_Q_TILE), F32)]
        + [pltpu.VMEM((tk, ATTN_Q_TILE), F32)] * (2 * hp),
        compiler_params=_params(2),
        name="fox_attention",
    )(q_aug, k_aug, vt_aug)


def _out_body(x_ref, mod_ref, pre_ref, post_ref, ot_ref, wg_ref, wo_ref, o_ref):
    tm, d = x_ref.shape
    x = x_ref[...]
    h = _prenorm(x, pre_ref[...], mod_ref[0:1, :], mod_ref[1:2, :]).astype(BF16)
    gate = _sigmoid(jnp.dot(h, wg_ref[...], preferred_element_type=F32))
    o = ot_ref[...].reshape(d, tm).T
    y = jnp.dot((o * gate).astype(BF16), wo_ref[...], preferred_element_type=F32)
    o_ref[...] = x + mod_ref[2:3, :] * _rms(y, post_ref[...])


def _fox_output(x, mod, pre_g, post_g, o_t, w_g, w_o):
    b, s, d = x.shape
    tm = TOKEN_TILE
    tok = pl.BlockSpec((None, tm, d), lambda i, j: (i, j, 0))
    return pl.pallas_call(
        _out_body,
        grid=(b, s // tm),
        in_specs=[
            tok,
            pl.BlockSpec((None, 6, d), lambda i, j: (i, 0, 0)),
            _const_spec((1, d)), _const_spec((1, d)),
            pl.BlockSpec((None, N_HEADS, None, HEAD_DIM, tm), lambda i, j: (i, 0, j, 0, 0)),
            _const_spec((d, d)), _const_spec((d, d)),
        ],
        out_specs=tok,
        out_shape=jax.ShapeDtypeStruct((b, s, d), F32),
        compiler_params=_params(2),
        name="fox_output",
    )(x, mod, pre_g, post_g, o_t, w_g, w_o)


def kernel(x, c, ada_w, ada_b, pre_mix_g, post_mix_g, pre_ffn_g, post_ffn_g, ffn_w_gu, ffn_w_down,
           a_w_in, a_b_in, a_ln_g, a_ln_b, a_w_s, a_b_s, a_w_out, kv_ada_w, kv_ada_b, kv_norm_g,
           kv_w, kv_b_f, k_norm_g, b_w_qg, b_q_norm_g, b_w_o):
    b, s, d = x.shape
    depth = ada_w.shape[0]
    n_a = a_w_in.shape[0]
    assert d == N_HEADS * HEAD_DIM and s % ATTN_Q_TILE == 0

    c8 = jnp.pad(c, ((0, 8 - b), (0, 0)))
    mods = _modulation(c8, ada_w, ada_b)[:, :b].reshape(depth, b, 6, d)
    kvmod = _modulation(c8, kv_ada_w[None], kv_ada_b[None])[0, :b].reshape(b, 2, d)

    row = lambda v: v.reshape(1, -1)
    q_scale = LOG2E * HEAD_DIM ** -0.5

    for layer in range(depth):
        mod = mods[layer]
        if layer < n_a:
            i = layer
            bs_full = jnp.repeat(a_b_s[i].T, CHUNK, axis=1)
            x = _gmlp_layer(x, mod, row(pre_mix_g[layer]), row(post_mix_g[layer]),
                            a_w_in[i].astype(BF16), row(a_b_in[i]), row(a_ln_g[i]), row(a_ln_b[i]),
                            a_w_s[i], bs_full, a_w_out[i].astype(BF16))
        else:
            j = layer - n_a
            qg2 = row(jnp.tile(b_q_norm_g[j] * q_scale, 2))
            q_aug = _q_project(x, mod, row(pre_mix_g[layer]), b_w_qg[j][:, :d].astype(BF16), qg2, dsc)
            o_t = _fox_attention(q_aug, k_aug, vt_aug)
            x = _fox_output(x, mod, row(pre_mix_g[layer]), row(post_mix_g[layer]), o_t,
                            b_w_qg[j][:, d:].astype(BF16), b_w_o[j].astype(BF16))
        x = _ffn_layer(x, mod, row(pre_ffn_g[layer]), row(post_ffn_g[layer]),
                       ffn_w_gu[layer].astype(BF16), ffn_w_down[layer].astype(BF16))
        if layer == n_a - 1:
            w_f = kv_w[:, 2 * d:]
            pad = jnp.zeros((d, LANES - 3 * N_HEADS), F32)
            w_f3 = jnp.concatenate([w_f, w_f, w_f, pad], axis=1).astype(BF16)
            b_f3 = row(jnp.concatenate([kv_b_f, kv_b_f, kv_b_f, jnp.zeros((LANES - 3 * N_HEADS,), F32)]))
            k_aug, vt_aug, dsc = _kv_project(
                x, kvmod, row(kv_norm_g), kv_w[:, :d].astype(BF16), kv_w[:, d:2 * d].astype(BF16),
                w_f3, b_f3, row(jnp.tile(k_norm_g, 2)))
    return x
```

```python
import functools

import jax
import jax.numpy as jnp
from jax import lax
from jax.experimental import pallas as pl
from jax.experimental.pallas import tpu as pltpu

F32 = jnp.float32
BF16 = jnp.bfloat16

EPS = 1e-6
N_HEADS = 16
HEAD_DIM = 64
CHUNK = 128
GROUPS = 16
LOG2E = 1.4426950408889634
LANES = 128
MXU_N = 256
TOKEN_TILE = 512
ATTN_Q_TILE = 2 * TOKEN_TILE
ATTN_HEADS_PER_STEP = 2
VMEM_LIMIT = 56 * 1024 * 1024


def _sigmoid(x):
    return 1.0 / (1.0 + jnp.exp(-x))


def _rms(x, g):
    return x * lax.rsqrt(jnp.mean(x * x, axis=-1, keepdims=True) + EPS) * g


def _prenorm(x, g, shift, scale):
    return _rms(x, g) * (1.0 + scale) + shift


def _split3(x):
    hi = x.astype(BF16).astype(F32)
    r = x - hi
    mid = r.astype(BF16).astype(F32)
    lo = (r - mid).astype(BF16).astype(F32)
    return hi, mid, lo


def _const_spec(shape):
    n = len(shape)
    return pl.BlockSpec(shape, lambda *_: (0,) * n, pipeline_mode=pl.Buffered(1))


def _params(n_grid):
    return pltpu.CompilerParams(
        dimension_semantics=("arbitrary",) * n_grid, vmem_limit_bytes=VMEM_LIMIT)


def _mod_body(c_ref, w_ref, b_ref, o_ref):
    c = c_ref[...]
    ca = (c * _sigmoid(c)).astype(BF16)
    o_ref[...] = jnp.dot(ca, w_ref[...].astype(BF16), preferred_element_type=F32) + b_ref[...]


def _modulation(c8, w, b):
    n_l, d, n = w.shape
    nb = 1024
    return pl.pallas_call(
        _mod_body,
        grid=(n_l, n // nb),
        in_specs=[
            pl.BlockSpec((8, d), lambda l, j: (0, 0)),
            pl.BlockSpec((None, d, nb), lambda l, j: (l, 0, j)),
            pl.BlockSpec((None, 1, nb), lambda l, j: (l, 0, j)),
        ],
        out_specs=pl.BlockSpec((None, 8, nb), lambda l, j: (l, 0, j)),
        out_shape=jax.ShapeDtypeStruct((n_l, 8, n), F32),
        compiler_params=_params(2),
        name="modulation",
    )(c8, w, b.reshape(n_l, 1, n))


def _gmlp_body(x_ref, mod_ref, pre_ref, post_ref, win_ref, bin_ref, lng_ref, lnb_ref, ws_ref,
               bs_ref, wout_ref, o_ref, h_scr, u_scr, v_scr, vn_scr, y_scr):
    tm, d = x_ref.shape
    gw = u_scr.shape[1]
    x = x_ref[...]
    h_scr[...] = _prenorm(x, pre_ref[...], mod_ref[0:1, :], mod_ref[1:2, :]).astype(BF16)

    k_gelu = 0.7978845608028654
    nc = 512
    for c0 in range(0, 2 * gw, nc):
        z = jnp.dot(h_scr[...], win_ref[:, c0:c0 + nc], preferred_element_type=F32)
        z = z + bin_ref[:, c0:c0 + nc]
        z = z * (0.5 * (1.0 + jnp.tanh(k_gelu * (z + 0.044715 * (z * z * z)))))
        if c0 < gw:
            u_scr[:, c0:c0 + nc] = z
        else:
            v_scr[:, c0 - gw:c0 - gw + nc] = z

    v = v_scr[...]
    mu = jnp.mean(v, axis=-1, keepdims=True)
    vc = v - mu
    rstd = lax.rsqrt(jnp.mean(vc * vc, axis=-1, keepdims=True) + EPS)
    vn_scr[...] = (vc * rstd * lng_ref[...] + lnb_ref[...]).astype(BF16)

    n_chunks = tm // CHUNK
    row = lax.broadcasted_iota(jnp.int32, (CHUNK, CHUNK), 0)
    col = lax.broadcasted_iota(jnp.int32, (CHUNK, CHUNK), 1)
    causal = col <= row
    for g in range(GROUPS):
        gs = slice(g * CHUNK, (g + 1) * CHUNK)
        ws = jnp.where(causal, ws_ref[g], 0.0).astype(BF16)
        rhs = jnp.concatenate(
            [vn_scr[c * CHUNK:(c + 1) * CHUNK, gs] for c in range(n_chunks)], axis=1)
        sp = jnp.dot(ws, rhs, preferred_element_type=F32)
        for c in range(n_chunks):
            cs = slice(c * CHUNK, (c + 1) * CHUNK)
            y_scr[cs, gs] = (u_scr[cs, gs] * (sp[:, cs] + bs_ref[:, gs])).astype(BF16)

    y = jnp.dot(y_scr[...], wout_ref[...], preferred_element_type=F32)
    o_ref[...] = x + mod_ref[2:3, :] * _rms(y, post_ref[...])


def _gmlp_layer(x, mod, pre_g, post_g, w_in, b_in, ln_g, ln_b, w_s, bs_full, w_out):
    b, s, d = x.shape
    gw = w_out.shape[0]
    tm = TOKEN_TILE
    tok = pl.BlockSpec((None, tm, d), lambda i, j: (i, j, 0))
    return pl.pallas_call(
        _gmlp_body,
        grid=(b, s // tm),
        in_specs=[
            tok,
            pl.BlockSpec((None, 6, d), lambda i, j: (i, 0, 0)),
            _const_spec((1, d)), _const_spec((1, d)),
            _const_spec((d, 2 * gw)), _const_spec((1, 2 * gw)),
            _const_spec((1, gw)), _const_spec((1, gw)),
            _const_spec((GROUPS, CHUNK, CHUNK)), _const_spec((CHUNK, gw)),
            _const_spec((gw, d)),
        ],
        out_specs=tok,
        out_shape=jax.ShapeDtypeStruct((b, s, d), F32),
        scratch_shapes=[
            pltpu.VMEM((tm, d), BF16), pltpu.VMEM((tm, gw), F32), pltpu.VMEM((tm, gw), F32),
            pltpu.VMEM((tm, gw), BF16), pltpu.VMEM((tm, gw), BF16),
        ],
        compiler_params=_params(2),
        name="gmlp_mixer",
    )(x, mod, pre_g, post_g, w_in, b_in, ln_g, ln_b, w_s, bs_full, w_out)


def _ffn_body(x_ref, mod_ref, pre_ref, post_ref, wgu_ref, wd_ref, o_ref, h_scr, a_scr):
    f = a_scr.shape[1]
    x = x_ref[...]
    h_scr[...] = _prenorm(x, pre_ref[...], mod_ref[3:4, :], mod_ref[4:5, :]).astype(BF16)
    for c0 in range(0, f, MXU_N):
        g = jnp.dot(h_scr[...], wgu_ref[:, c0:c0 + MXU_N], preferred_element_type=F32)
        u = jnp.dot(h_scr[...], wgu_ref[:, f + c0:f + c0 + MXU_N], preferred_element_type=F32)
        a_scr[:, c0:c0 + MXU_N] = (g * _sigmoid(g) * u).astype(BF16)
    y = jnp.dot(a_scr[...], wd_ref[...], preferred_element_type=F32)
    o_ref[...] = x + mod_ref[5:6, :] * _rms(y, post_ref[...])


def _ffn_layer(x, mod, pre_g, post_g, w_gu, w_down):
    b, s, d = x.shape
    f = w_down.shape[0]
    tm = TOKEN_TILE
    tok = pl.BlockSpec((None, tm, d), lambda i, j: (i, j, 0))
    return pl.pallas_call(
        _ffn_body,
        grid=(b, s // tm),
        in_specs=[
            tok,
            pl.BlockSpec((None, 6, d), lambda i, j: (i, 0, 0)),
            _const_spec((1, d)), _const_spec((1, d)),
            _const_spec((d, 2 * f)), _const_spec((f, d)),
        ],
        out_specs=tok,
        out_shape=jax.ShapeDtypeStruct((b, s, d), F32),
        scratch_shapes=[pltpu.VMEM((tm, d), BF16), pltpu.VMEM((tm, f), BF16)],
        compiler_params=_params(2),
        name="swiglu_ffn",
    )(x, mod, pre_g, post_g, w_gu, w_down)


def _head_pair_norm(t, gain2):
    lane = lax.broadcasted_iota(jnp.int32, t.shape, 1)
    sq = t * t
    first = lane < HEAD_DIM
    s0 = jnp.sum(jnp.where(first, sq, 0.0), axis=-1, keepdims=True)
    s1 = jnp.sum(jnp.where(first, 0.0, sq), axis=-1, keepdims=True)
    rs = jnp.where(first, lax.rsqrt(s0 / HEAD_DIM + EPS), lax.rsqrt(s1 / HEAD_DIM + EPS))
    return t * rs * gain2


def _decay_selectors(query_side):
    row = lax.broadcasted_iota(jnp.int32, (LANES, N_HEADS * LANES), 0)
    col = lax.broadcasted_iota(jnp.int32, (LANES, N_HEADS * LANES), 1)
    head, off = col // LANES, col % LANES - HEAD_DIM
    first, second = (off >= 0) & (off < 3), (off >= 3) & (off < 6)
    piece_cols, piece_idx = (first, off) if query_side else (second, off - 3)
    one_cols = second if query_side else first
    is_piece = piece_cols & (row == piece_idx * N_HEADS + head)
    is_one = one_cols & (row == 3 * N_HEADS)
    sel = jnp.where(is_piece, 1.0 if query_side else -1.0, jnp.where(is_one, 1.0, 0.0))
    return sel.astype(BF16)


def _with_tail(t, tail):
    lane = lax.broadcasted_iota(jnp.int32, t.shape, 1)
    return jnp.where(lane < HEAD_DIM, t, tail).astype(BF16)


def _kv_body(x_ref, mod_ref, g_ref, wk_ref, wv_ref, wf_ref, bf_ref, kg_ref, sel_ref,
             k_out, vt_out, d_out, h_scr, carry_scr):
    tm, d = x_ref.shape

    @pl.when(pl.program_id(1) == 0)
    def _():
        carry_scr[...] = jnp.zeros_like(carry_scr)

    h_scr[...] = _prenorm(x_ref[...], g_ref[...], mod_ref[0:1, :], mod_ref[1:2, :]).astype(BF16)

    lane = lax.broadcasted_iota(jnp.int32, (tm, LANES), 1)
    fl = jnp.dot(h_scr[...], wf_ref[...], preferred_element_type=F32) + bf_ref[...]
    z = -fl
    ls = -(jnp.maximum(z, 0.0) + jnp.log1p(jnp.exp(-jnp.abs(z))))
    hi, mid, lo = _split3(ls)
    pieces = jnp.where(lane < N_HEADS, hi, jnp.where(lane < 2 * N_HEADS, mid, jnp.where(
        lane < 3 * N_HEADS, lo, 0.0))).astype(BF16)
    r = lax.broadcasted_iota(jnp.int32, (tm, tm), 0)
    c = lax.broadcasted_iota(jnp.int32, (tm, tm), 1)
    tri = jnp.where(c <= r, 1.0, 0.0).astype(BF16)
    cs = jnp.dot(tri, pieces, preferred_element_type=F32)
    tot = cs + pltpu.roll(cs, LANES - N_HEADS, axis=1) + pltpu.roll(cs, LANES - 2 * N_HEADS, axis=1)
    dc = jnp.where(lane < N_HEADS, tot + carry_scr[...], 0.0)
    carry_scr[...] = dc[tm - 1:tm, :]
    hi, mid, lo = _split3(dc * LOG2E)
    d3 = (hi + pltpu.roll(mid, N_HEADS, axis=1) + pltpu.roll(lo, 2 * N_HEADS, axis=1)
          + jnp.where(lane == 3 * N_HEADS, 1.0, 0.0)).astype(BF16)
    d_out[...] = d3

    kg2 = kg_ref[...]
    ones_rows = jnp.where(lax.broadcasted_iota(jnp.int32, (HEAD_DIM, tm), 0) == 0, 1.0, 0.0)
    heads_per_chunk = MXU_N // HEAD_DIM
    for c4 in range(d // MXU_N):
        kc = jnp.dot(h_scr[...], wk_ref[:, c4 * MXU_N:(c4 + 1) * MXU_N], preferred_element_type=F32)
        vc = jnp.dot(h_scr[...], wv_ref[:, c4 * MXU_N:(c4 + 1) * MXU_N], preferred_element_type=F32)
        tails = jnp.dot(d3, sel_ref[:, c4 * heads_per_chunk * LANES:(c4 + 1) * heads_per_chunk * LANES],
                        preferred_element_type=F32)
        for half in range(2):
            kn = _head_pair_norm(kc[:, half * LANES:(half + 1) * LANES], kg2)
            vt = vc[:, half * LANES:(half + 1) * LANES].T
            for sub, src in ((0, kn), (1, pltpu.roll(kn, HEAD_DIM, axis=1))):
                loc = 2 * half + sub
                hd = c4 * heads_per_chunk + loc
                k_out[hd] = _with_tail(src, tails[:, loc * LANES:(loc + 1) * LANES])
                vt_out[hd] = jnp.concatenate(
                    [vt[sub * HEAD_DIM:(sub + 1) * HEAD_DIM, :], ones_rows], axis=0).astype(BF16)


def _kv_project(x, kvmod, norm_g, w_k, w_v, w_f3, b_f3, kg2):
    b, s, d = x.shape
    tm = TOKEN_TILE
    return pl.pallas_call(
        _kv_body,
        grid=(b, s // tm),
        in_specs=[
            pl.BlockSpec((None, tm, d), lambda i, j: (i, j, 0)),
            pl.BlockSpec((None, 2, d), lambda i, j: (i, 0, 0)),
            _const_spec((1, d)),
            _const_spec((d, d)), _const_spec((d, d)), _const_spec((d, LANES)),
            _const_spec((1, LANES)), _const_spec((1, LANES)), _const_spec((LANES, N_HEADS * LANES)),
        ],
        out_specs=[
            pl.BlockSpec((None, N_HEADS, tm, LANES), lambda i, j: (i, 0, j, 0)),
            pl.BlockSpec((None, N_HEADS, None, LANES, tm), lambda i, j: (i, 0, j, 0, 0)),
            pl.BlockSpec((None, tm, LANES), lambda i, j: (i, j, 0)),
        ],
        out_shape=[
            jax.ShapeDtypeStruct((b, N_HEADS, s, LANES), BF16),
            jax.ShapeDtypeStruct((b, N_HEADS, s // tm, LANES, tm), BF16),
            jax.ShapeDtypeStruct((b, s, LANES), BF16),
        ],
        scratch_shapes=[pltpu.VMEM((tm, d), BF16), pltpu.VMEM((1, LANES), F32)],
        compiler_params=_params(2),
        name="kv_project",
    )(x, kvmod, norm_g, w_k, w_v, w_f3, b_f3, kg2, _decay_selectors(query_side=False))


def _q_body(x_ref, mod_ref, pre_ref, wq_ref, qg_ref, d_ref, sel_ref, q_out, h_scr):
    tm, d = x_ref.shape
    h_scr[...] = _prenorm(x_ref[...], pre_ref[...], mod_ref[0:1, :], mod_ref[1:2, :]).astype(BF16)
    d3 = d_ref[...]
    qg2 = qg_ref[...]
    heads_per_chunk = MXU_N // HEAD_DIM
    for c4 in range(d // MXU_N):
        qc = jnp.dot(h_scr[...], wq_ref[:, c4 * MXU_N:(c4 + 1) * MXU_N], preferred_element_type=F32)
        tails = jnp.dot(d3, sel_ref[:, c4 * heads_per_chunk * LANES:(c4 + 1) * heads_per_chunk * LANES],
                        preferred_element_type=F32)
        for half in range(2):
            qn = _head_pair_norm(qc[:, half * LANES:(half + 1) * LANES], qg2)
            for sub, src in ((0, qn), (1, pltpu.roll(qn, HEAD_DIM, axis=1))):
                loc = 2 * half + sub
                q_out[c4 * heads_per_chunk + loc] = _with_tail(src, tails[:, loc * LANES:(loc + 1) * LANES])


def _q_project(x, mod, pre_g, w_q, qg2, d3):
    b, s, d = x.shape
    tm = TOKEN_TILE
    return pl.pallas_call(
        _q_body,
        grid=(b, s // tm),
        in_specs=[
            pl.BlockSpec((None, tm, d), lambda i, j: (i, j, 0)),
            pl.BlockSpec((None, 6, d), lambda i, j: (i, 0, 0)),
            _const_spec((1, d)), _const_spec((d, d)), _const_spec((1, LANES)),
            pl.BlockSpec((None, tm, LANES), lambda i, j: (i, j, 0)),
            _const_spec((LANES, N_HEADS * LANES)),
        ],
        out_specs=pl.BlockSpec((None, N_HEADS, tm, LANES), lambda i, j: (i, 0, j, 0)),
        out_shape=jax.ShapeDtypeStruct((b, N_HEADS, s, LANES), BF16),
        scratch_shapes=[pltpu.VMEM((tm, d), BF16)],
        compiler_params=_params(2),
        name="q_project",
    )(x, mod, pre_g, w_q, qg2, d3, _decay_selectors(query_side=True))


def _attn_body(q_ref, k_ref, vt_ref, o_ref, acc_scr, *s_scr):
    n_h = q_ref.shape[0]
    tk = vt_ref.shape[3]
    tq = 2 * tk
    nq = q_ref.shape[1] // tq
    heads = range(n_h)
    kpos = lax.broadcasted_iota(jnp.int32, (tk, tq), 0)
    qpos = lax.broadcasted_iota(jnp.int32, (tk, tq), 1)

    def scores(qt, j, slot, valid):
        cms = []
        for h in heads:
            q = q_ref[h, pl.ds(pl.multiple_of(qt * tq, tq), tq), :]
            k = k_ref[h, pl.ds(pl.multiple_of(j * tk, tk), tk), :]
            s = lax.dot_general(k, q, (((1,), (1,)), ((), ())), preferred_element_type=F32)
            if valid is not None:
                s = jnp.where(valid, s, -jnp.inf)
            s_scr[2 * h + slot][...] = s
            cms.append(jnp.max(s, axis=0, keepdims=True))
        return tuple(cms)

    def accumulate(j, slot, cmax, ms):
        out = []
        for h in heads:
            m_new = jnp.maximum(ms[h], cmax[h])
            alpha = jnp.exp2(ms[h] - m_new)
            p = jnp.exp2(s_scr[2 * h + slot][...] - m_new)
            pv = jnp.dot(vt_ref[h, j], p.astype(BF16), preferred_element_type=F32)
            acc_scr[h] = alpha * acc_scr[h] + pv
            out.append(m_new)
        return tuple(out)

    def pair(qi, p, carry, second_valid):
        ms, cm0 = carry
        cm1 = scores(qi, 2 * p + 1, 1, None)
        ms = accumulate(2 * p, 0, cm0, ms)
        cm0 = scores(qi, 2 * p + 2, 0, second_valid)
        ms = accumulate(2 * p + 1, 1, cm1, ms)
        return ms, cm0

    def query_tile(qi, cm0):
        acc_scr[...] = jnp.zeros_like(acc_scr)
        ms0 = tuple(jnp.full((1, tq), -jnp.inf, F32) for _ in heads)
        carry = lax.fori_loop(0, qi - 1, lambda p, c: pair(qi, p, c, None), (ms0, cm0))
        carry = lax.cond(qi > 0, lambda c: pair(qi, qi - 1, c, kpos <= qpos), lambda c: c, carry)
        ms, cm0 = carry
        cm1 = scores(qi, 2 * qi + 1, 1, kpos + tk <= qpos)
        ms = accumulate(2 * qi, 0, cm0, ms)
        cm0 = scores(jnp.minimum(qi + 1, nq - 1), 0, 0, None)
        accumulate(2 * qi + 1, 1, cm1, ms)
        for h in heads:
            o = acc_scr[h, 0:HEAD_DIM, :] / acc_scr[h, HEAD_DIM:HEAD_DIM + 1, :]
            o_ref[h, 2 * qi] = o[:, :tk]
            o_ref[h, 2 * qi + 1] = o[:, tk:]
        return cm0

    lax.fori_loop(0, nq, query_tile, scores(0, 0, 0, kpos <= qpos))


def _fox_attention(q_aug, k_aug, vt_aug):
    b, h, s, _ = q_aug.shape
    tk = TOKEN_TILE
    nk = s // tk
    hp = ATTN_HEADS_PER_STEP
    return pl.pallas_call(
        _attn_body,
        grid=(b, h // hp),
        in_specs=[
            pl.BlockSpec((None, hp, s, LANES), lambda i, j: (i, j, 0, 0)),
            pl.BlockSpec((None, hp, s, LANES), lambda i, j: (i, j, 0, 0)),
            pl.BlockSpec((None, hp, nk, LANES, tk), lambda i, j: (i, j, 0, 0, 0)),
        ],
        out_specs=pl.BlockSpec((None, hp, nk, HEAD_DIM, tk), lambda i, j: (i, j, 0, 0, 0)),
        out_shape=jax.ShapeDtypeStruct((b, h, nk, HEAD_DIM, tk), F32),
        scratch_shapes=[pltpu.VMEM((hp, LANES, ATTN_Q_TILE), F32)]
        + [pltpu.VMEM((tk, ATTN_Q_TILE), F32)] * (2 * hp),
        compiler_params=_params(2),
        name="fox_attention",
    )(q_aug, k_aug, vt_aug)


def _out_body(x_ref, mod_ref, pre_ref, post_ref, ot_ref, wg_ref, wo_ref, o_ref):
    tm, d = x_ref.shape
    x = x_ref[...]
    h = _prenorm(x, pre_ref[...], mod_ref[0:1, :], mod_ref[1:2, :]).astype(BF16)
    gate = _sigmoid(jnp.dot(h, wg_ref[...], preferred_element_type=F32))
    o = ot_ref[...].reshape(d, tm).T
    y = jnp.dot((o * gate).astype(BF16), wo_ref[...], preferred_element_type=F32)
    o_ref[...] = x + mod_ref[2:3, :] * _rms(y, post_ref[...])


def _fox_output(x, mod, pre_g, post_g, o_t, w_g, w_o):
    b, s, d = x.shape
    tm = TOKEN_TILE
    tok = pl.BlockSpec((None, tm, d), lambda i, j: (i, j, 0))
    return pl.pallas_call(
        _out_body,
        grid=(b, s // tm),
        in_specs=[
            tok,
            pl.BlockSpec((None, 6, d), lambda i, j: (i, 0, 0)),
            _const_spec((1, d)), _const_spec((1, d)),
            pl.BlockSpec((None, N_HEADS, None, HEAD_DIM, tm), lambda i, j: (i, 0, j, 0, 0)),
            _const_spec((d, d)), _const_spec((d, d)),
        ],
        out_specs=tok,
        out_shape=jax.ShapeDtypeStruct((b, s, d), F32),
        compiler_params=_params(2),
        name="fox_output",
    )(x, mod, pre_g, post_g, o_t, w_g, w_o)


def kernel(x, c, ada_w, ada_b, pre_mix_g, post_mix_g, pre_ffn_g, post_ffn_g, ffn_w_gu, ffn_w_down,
           a_w_in, a_b_in, a_ln_g, a_ln_b, a_w_s, a_b_s, a_w_out, kv_ada_w, kv_ada_b, kv_norm_g,
           kv_w, kv_b_f, k_norm_g, b_w_qg, b_q_norm_g, b_w_o):
    b, s, d = x.shape
    depth = ada_w.shape[0]
    n_a = a_w_in.shape[0]
    assert d == N_HEADS * HEAD_DIM and s % ATTN_Q_TILE == 0

    c8 = jnp.pad(c, ((0, 8 - b), (0, 0)))
    mods = _modulation(c8, ada_w, ada_b)[:, :b].reshape(depth, b, 6, d)
    kvmod = _modulation(c8, kv_ada_w[None], kv_ada_b[None])[0, :b].reshape(b, 2, d)

    row = lambda v: v.reshape(1, -1)
    q_scale = LOG2E * HEAD_DIM ** -0.5

    for layer in range(depth):
        mod = mods[layer]
        if layer < n_a:
            i = layer
            bs_full = jnp.repeat(a_b_s[i].T, CHUNK, axis=1)
            x = _gmlp_layer(x, mod, row(pre_mix_g[layer]), row(post_mix_g[layer]),
                            a_w_in[i].astype(BF16), row(a_b_in[i]), row(a_ln_g[i]), row(a_ln_b[i]),
                            a_w_s[i], bs_full, a_w_out[i].astype(BF16))
        else:
            j = layer - n_a
            qg2 = row(jnp.tile(b_q_norm_g[j] * q_scale, 2))
            q_aug = _q_project(x, mod, row(pre_mix_g[layer]), b_w_qg[j][:, :d].astype(BF16), qg2, d3)
            o_t = _fox_attention(q_aug, k_aug, vt_aug)
            x = _fox_output(x, mod, row(pre_mix_g[layer]), row(post_mix_g[layer]), o_t,
                            b_w_qg[j][:, d:].astype(BF16), b_w_o[j].astype(BF16))
        x = _ffn_layer(x, mod, row(pre_ffn_g[layer]), row(post_ffn_g[layer]),
                       ffn_w_gu[layer].astype(BF16), ffn_w_down[layer].astype(BF16))
        if layer == n_a - 1:
            w_f = kv_w[:, 2 * d:]
            pad = jnp.zeros((d, LANES - 3 * N_HEADS), F32)
            w_f3 = jnp.concatenate([w_f, w_f, w_f, pad], axis=1).astype(BF16)
            b_f3 = row(jnp.concatenate([kv_b_f, kv_b_f, kv_b_f, jnp.zeros((LANES - 3 * N_HEADS,), F32)]))
            k_aug, vt_aug, d3 = _kv_project(
                x, kvmod, row(kv_norm_g), kv_w[:, :d].astype(BF16), kv_w[:, d:2 * d].astype(BF16),
                w_f3, b_f3, row(jnp.tile(k_norm_g, 2)))
    return x
```

```python
import functools

import jax
import jax.numpy as jnp
from jax import lax
from jax.experimental import pallas as pl
from jax.experimental.pallas import tpu as pltpu

F32 = jnp.float32
BF16 = jnp.bfloat16

EPS = 1e-6
N_HEADS = 16
HEAD_DIM = 64
CHUNK = 128
GROUPS = 16
LOG2E = 1.4426950408889634
LANES = 128
MXU_N = 256
TOKEN_TILE = 512
ATTN_Q_TILE = 2 * TOKEN_TILE
ATTN_HEADS_PER_STEP = 2
VMEM_LIMIT = 56 * 1024 * 1024


def _sigmoid(x):
    return 1.0 / (1.0 + jnp.exp(-x))


def _rms(x, g):
    return x * lax.rsqrt(jnp.mean(x * x, axis=-1, keepdims=True) + EPS) * g


def _prenorm(x, g, shift, scale):
    return _rms(x, g) * (1.0 + scale) + shift


def _split3(x):
    hi = x.astype(BF16).astype(F32)
    r = x - hi
    mid = r.astype(BF16).astype(F32)
    lo = (r - mid).astype(BF16).astype(F32)
    return hi, mid, lo


def _const_spec(shape):
    n = len(shape)
    return pl.BlockSpec(shape, lambda *_: (0,) * n, pipeline_mode=pl.Buffered(1))


def _params(n_grid):
    return pltpu.CompilerParams(
        dimension_semantics=("arbitrary",) * n_grid, vmem_limit_bytes=VMEM_LIMIT)


def _mod_body(c_ref, w_ref, b_ref, o_ref):
    c = c_ref[...]
    ca = (c * _sigmoid(c)).astype(BF16)
    o_ref[...] = jnp.dot(ca, w_ref[...].astype(BF16), preferred_element_type=F32) + b_ref[...]


def _modulation(c8, w, b):
    n_l, d, n = w.shape
    nb = 1024
    return pl.pallas_call(
        _mod_body,
        grid=(n_l, n // nb),
        in_specs=[
            pl.BlockSpec((8, d), lambda l, j: (0, 0)),
            pl.BlockSpec((None, d, nb), lambda l, j: (l, 0, j)),
            pl.BlockSpec((None, 1, nb), lambda l, j: (l, 0, j)),
        ],
        out_specs=pl.BlockSpec((None, 8, nb), lambda l, j: (l, 0, j)),
        out_shape=jax.ShapeDtypeStruct((n_l, 8, n), F32),
        compiler_params=_params(2),
        name="modulation",
    )(c8, w, b.reshape(n_l, 1, n))


def _gmlp_body(x_ref, mod_ref, pre_ref, post_ref, win_ref, bin_ref, lng_ref, lnb_ref, ws_ref,
               bs_ref, wout_ref, o_ref, h_scr, u_scr, v_scr, vn_scr, y_scr):
    tm, d = x_ref.shape
    gw = u_scr.shape[1]
    x = x_ref[...]
    h_scr[...] = _prenorm(x, pre_ref[...], mod_ref[0:1, :], mod_ref[1:2, :]).astype(BF16)

    k_gelu = 0.7978845608028654
    nc = 512
    for c0 in range(0, 2 * gw, nc):
        z = jnp.dot(h_scr[...], win_ref[:, c0:c0 + nc], preferred_element_type=F32)
        z = z + bin_ref[:, c0:c0 + nc]
        z = z * (0.5 * (1.0 + jnp.tanh(k_gelu * (z + 0.044715 * (z * z * z)))))
        if c0 < gw:
            u_scr[:, c0:c0 + nc] = z
        else:
            v_scr[:, c0 - gw:c0 - gw + nc] = z

    v = v_scr[...]
    mu = jnp.mean(v, axis=-1, keepdims=True)
    vc = v - mu
    rstd = lax.rsqrt(jnp.mean(vc * vc, axis=-1, keepdims=True) + EPS)
    vn_scr[...] = (vc * rstd * lng_ref[...] + lnb_ref[...]).astype(BF16)

    n_chunks = tm // CHUNK
    row = lax.broadcasted_iota(jnp.int32, (CHUNK, CHUNK), 0)
    col = lax.broadcasted_iota(jnp.int32, (CHUNK, CHUNK), 1)
    causal = col <= row
    for g in range(GROUPS):
        gs = slice(g * CHUNK, (g + 1) * CHUNK)
        ws = jnp.where(causal, ws_ref[g], 0.0).astype(BF16)
        rhs = jnp.concatenate(
            [vn_scr[c * CHUNK:(c + 1) * CHUNK, gs] for c in range(n_chunks)], axis=1)
        sp = jnp.dot(ws, rhs, preferred_element_type=F32)
        for c in range(n_chunks):
            cs = slice(c * CHUNK, (c + 1) * CHUNK)
            y_scr[cs, gs] = (u_scr[cs, gs] * (sp[:, cs] + bs_ref[:, gs])).astype(BF16)

    y = jnp.dot(y_scr[...], wout_ref[...], preferred_element_type=F32)
    o_ref[...] = x + mod_ref[2:3, :] * _rms(y, post_ref[...])


def _gmlp_layer(x, mod, pre_g, post_g, w_in, b_in, ln_g, ln_b, w_s, bs_full, w_out):
    b, s, d = x.shape
    gw = w_out.shape[0]
    tm = TOKEN_TILE
    tok = pl.BlockSpec((None, tm, d), lambda i, j: (i, j, 0))
    return pl.pallas_call(
        _gmlp_body,
        grid=(b, s // tm),
        in_specs=[
            tok,
            pl.BlockSpec((None, 6, d), lambda i, j: (i, 0, 0)),
            _const_spec((1, d)), _const_spec((1, d)),
            _const_spec((d, 2 * gw)), _const_spec((1, 2 * gw)),
            _const_spec((1, gw)), _const_spec((1, gw)),
            _const_spec((GROUPS, CHUNK, CHUNK)), _const_spec((CHUNK, gw)),
            _const_spec((gw, d)),
        ],
        out_specs=tok,
        out_shape=jax.ShapeDtypeStruct((b, s, d), F32),
        scratch_shapes=[
            pltpu.VMEM((tm, d), BF16), pltpu.VMEM((tm, gw), F32), pltpu.VMEM((tm, gw), F32),
            pltpu.VMEM((tm, gw), BF16), pltpu.VMEM((tm, gw), BF16),
        ],
        compiler_params=_params(2),
        name="gmlp_mixer",
    )(x, mod, pre_g, post_g, w_in, b_in, ln_g, ln_b, w_s, bs_full, w_out)


def _ffn_body(x_ref, mod_ref, pre_ref, post_ref, wgu_ref, wd_ref, o_ref, h_scr, a_scr):
    f = a_scr.shape[1]
    x = x_ref[...]
    h_scr[...] = _prenorm(x, pre_ref[...], mod_ref[3:4, :], mod_ref[4:5, :]).astype(BF16)
    for c0 in range(0, f, MXU_N):
        g = jnp.dot(h_scr[...], wgu_ref[:, c0:c0 + MXU_N], preferred_element_type=F32)
        u = jnp.dot(h_scr[...], wgu_ref[:, f + c0:f + c0 + MXU_N], preferred_element_type=F32)
        a_scr[:, c0:c0 + MXU_N] = (g * _sigmoid(g) * u).astype(BF16)
    y = jnp.dot(a_scr[...], wd_ref[...], preferred_element_type=F32)
    o_ref[...] = x + mod_ref[5:6, :] * _rms(y, post_ref[...])


def _ffn_layer(x, mod, pre_g, post_g, w_gu, w_down):
    b, s, d = x.shape
    f = w_down.shape[0]
    tm = TOKEN_TILE
    tok = pl.BlockSpec((None, tm, d), lambda i, j: (i, j, 0))
    return pl.pallas_call(
        _ffn_body,
        grid=(b, s // tm),
        in_specs=[
            tok,
            pl.BlockSpec((None, 6, d), lambda i, j: (i, 0, 0)),
            _const_spec((1, d)), _const_spec((1, d)),
            _const_spec((d, 2 * f)), _const_spec((f, d)),
        ],
        out_specs=tok,
        out_shape=jax.ShapeDtypeStruct((b, s, d), F32),
        scratch_shapes=[pltpu.VMEM((tm, d), BF16), pltpu.VMEM((tm, f), BF16)],
        compiler_params=_params(2),
        name="swiglu_ffn",
    )(x, mod, pre_g, post_g, w_gu, w_down)


def _head_pair_norm(t, gain2):
    lane = lax.broadcasted_iota(jnp.int32, t.shape, 1)
    sq = t * t
    first = lane < HEAD_DIM
    s0 = jnp.sum(jnp.where(first, sq, 0.0), axis=-1, keepdims=True)
    s1 = jnp.sum(jnp.where(first, 0.0, sq), axis=-1, keepdims=True)
    rs = jnp.where(first, lax.rsqrt(s0 / HEAD_DIM + EPS), lax.rsqrt(s1 / HEAD_DIM + EPS))
    return t * rs * gain2


def _decay_selectors(query_side):
    row = lax.broadcasted_iota(jnp.int32, (LANES, N_HEADS * LANES), 0)
    col = lax.broadcasted_iota(jnp.int32, (LANES, N_HEADS * LANES), 1)
    head, off = col // LANES, col % LANES - HEAD_DIM
    first, second = (off >= 0) & (off < 3), (off >= 3) & (off < 6)
    piece_cols, piece_idx = (first, off) if query_side else (second, off - 3)
    one_cols = second if query_side else first
    is_piece = piece_cols & (row == piece_idx * N_HEADS + head)
    is_one = one_cols & (row == 3 * N_HEADS)
    sel = jnp.where(is_piece, 1.0 if query_side else -1.0, jnp.where(is_one, 1.0, 0.0))
    return sel.astype(BF16)


def _with_tail(t, tail):
    lane = lax.broadcasted_iota(jnp.int32, t.shape, 1)
    return jnp.where(lane < HEAD_DIM, t, tail)


def _kv_body(x_ref, mod_ref, g_ref, wk_ref, wv_ref, wf_ref, bf_ref, kg_ref, sel_ref,
             k_out, vt_out, d_out, h_scr, carry_scr):
    tm, d = x_ref.shape

    @pl.when(pl.program_id(1) == 0)
    def _():
        carry_scr[...] = jnp.zeros_like(carry_scr)

    h_scr[...] = _prenorm(x_ref[...], g_ref[...], mod_ref[0:1, :], mod_ref[1:2, :]).astype(BF16)

    lane = lax.broadcasted_iota(jnp.int32, (tm, LANES), 1)
    fl = jnp.dot(h_scr[...], wf_ref[...], preferred_element_type=F32) + bf_ref[...]
    z = -fl
    ls = -(jnp.maximum(z, 0.0) + jnp.log1p(jnp.exp(-jnp.abs(z))))
    hi, mid, lo = _split3(ls)
    pieces = jnp.where(lane < N_HEADS, hi, jnp.where(lane < 2 * N_HEADS, mid, jnp.where(
        lane < 3 * N_HEADS, lo, 0.0))).astype(BF16)
    r = lax.broadcasted_iota(jnp.int32, (tm, tm), 0)
    c = lax.broadcasted_iota(jnp.int32, (tm, tm), 1)
    tri = jnp.where(c <= r, 1.0, 0.0).astype(BF16)
    cs = jnp.dot(tri, pieces, preferred_element_type=F32)
    tot = cs + pltpu.roll(cs, LANES - N_HEADS, axis=1) + pltpu.roll(cs, LANES - 2 * N_HEADS, axis=1)
    dc = jnp.where(lane < N_HEADS, tot + carry_scr[...], 0.0)
    carry_scr[...] = dc[tm - 1:tm, :]
    hi, mid, lo = _split3(dc * LOG2E)
    d3 = (hi + pltpu.roll(mid, N_HEADS, axis=1) + pltpu.roll(lo, 2 * N_HEADS, axis=1)
          + jnp.where(lane == 3 * N_HEADS, 1.0, 0.0)).astype(BF16)
    d_out[...] = d3

    kg2 = kg_ref[...]
    heads_per_chunk = MXU_N // HEAD_DIM
    for c4 in range(d // MXU_N):
        kc = jnp.dot(h_scr[...], wk_ref[:, c4 * MXU_N:(c4 + 1) * MXU_N], preferred_element_type=F32)
        vc = jnp.dot(h_scr[...], wv_ref[:, c4 * MXU_N:(c4 + 1) * MXU_N], preferred_element_type=F32)
        tails = jnp.dot(d3, sel_ref[:, c4 * heads_per_chunk * LANES:(c4 + 1) * heads_per_chunk * LANES],
                        preferred_element_type=F32)
        for half in range(2):
            kn = _head_pair_norm(kc[:, half * LANES:(half + 1) * LANES], kg2)
            vt = vc[:, half * LANES:(half + 1) * LANES].T
            for sub, src in ((0, kn), (1, pltpu.roll(kn, HEAD_DIM, axis=1))):
                loc = 2 * half + sub
                hd = c4 * heads_per_chunk + loc
                k_out[hd] = _with_tail(src, tails[:, loc * LANES:(loc + 1) * LANES]).astype(BF16)
                vt_out[hd] = vt[sub * HEAD_DIM:(sub + 1) * HEAD_DIM, :].astype(BF16)


def _kv_project(x, kvmod, norm_g, w_k, w_v, w_f3, b_f3, kg2):
    b, s, d = x.shape
    tm = TOKEN_TILE
    return pl.pallas_call(
        _kv_body,
        grid=(b, s // tm),
        in_specs=[
            pl.BlockSpec((None, tm, d), lambda i, j: (i, j, 0)),
            pl.BlockSpec((None, 2, d), lambda i, j: (i, 0, 0)),
            _const_spec((1, d)),
            _const_spec((d, d)), _const_spec((d, d)), _const_spec((d, LANES)),
            _const_spec((1, LANES)), _const_spec((1, LANES)), _const_spec((LANES, N_HEADS * LANES)),
        ],
        out_specs=[
            pl.BlockSpec((None, N_HEADS, tm, LANES), lambda i, j: (i, 0, j, 0)),
            pl.BlockSpec((None, N_HEADS, None, HEAD_DIM, tm), lambda i, j: (i, 0, j, 0, 0)),
            pl.BlockSpec((None, tm, LANES), lambda i, j: (i, j, 0)),
        ],
        out_shape=[
            jax.ShapeDtypeStruct((b, N_HEADS, s, LANES), BF16),
            jax.ShapeDtypeStruct((b, N_HEADS, s // tm, HEAD_DIM, tm), BF16),
            jax.ShapeDtypeStruct((b, s, LANES), BF16),
        ],
        scratch_shapes=[pltpu.VMEM((tm, d), BF16), pltpu.VMEM((1, LANES), F32)],
        compiler_params=_params(2),
        name="kv_project",
    )(x, kvmod, norm_g, w_k, w_v, w_f3, b_f3, kg2, _decay_selectors(query_side=False))


def _q_body(x_ref, mod_ref, pre_ref, wq_ref, qg_ref, d_ref, sel_ref, q_out, h_scr):
    tm, d = x_ref.shape
    h_scr[...] = _prenorm(x_ref[...], pre_ref[...], mod_ref[0:1, :], mod_ref[1:2, :]).astype(BF16)
    d3 = d_ref[...]
    qg2 = qg_ref[...]
    heads_per_chunk = MXU_N // HEAD_DIM
    for c4 in range(d // MXU_N):
        qc = jnp.dot(h_scr[...], wq_ref[:, c4 * MXU_N:(c4 + 1) * MXU_N], preferred_element_type=F32)
        tails = jnp.dot(d3, sel_ref[:, c4 * heads_per_chunk * LANES:(c4 + 1) * heads_per_chunk * LANES],
                        preferred_element_type=F32)
        for half in range(2):
            qn = _head_pair_norm(qc[:, half * LANES:(half + 1) * LANES], qg2)
            for sub, src in ((0, qn), (1, pltpu.roll(qn, HEAD_DIM, axis=1))):
                loc = 2 * half + sub
                qa = _with_tail(src, tails[:, loc * LANES:(loc + 1) * LANES])
                q_out[c4 * heads_per_chunk + loc] = qa.astype(BF16)


def _q_project(x, mod, pre_g, w_q, qg2, d3):
    b, s, d = x.shape
    tm = TOKEN_TILE
    return pl.pallas_call(
        _q_body,
        grid=(b, s // tm),
        in_specs=[
            pl.BlockSpec((None, tm, d), lambda i, j: (i, j, 0)),
            pl.BlockSpec((None, 6, d), lambda i, j: (i, 0, 0)),
            _const_spec((1, d)), _const_spec((d, d)), _const_spec((1, LANES)),
            pl.BlockSpec((None, tm, LANES), lambda i, j: (i, j, 0)),
            _const_spec((LANES, N_HEADS * LANES)),
        ],
        out_specs=pl.BlockSpec((None, N_HEADS, tm, LANES), lambda i, j: (i, 0, j, 0)),
        out_shape=jax.ShapeDtypeStruct((b, N_HEADS, s, LANES), BF16),
        scratch_shapes=[pltpu.VMEM((tm, d), BF16)],
        compiler_params=_params(2),
        name="q_project",
    )(x, mod, pre_g, w_q, qg2, d3, _decay_selectors(query_side=True))


def _attn_body(q_ref, k_ref, vt_ref, o_ref, acc_scr, *s_scr):
    n_h = q_ref.shape[0]
    tk = vt_ref.shape[3]
    tq = 2 * tk
    nq = q_ref.shape[1] // tq
    heads = range(n_h)
    kpos = lax.broadcasted_iota(jnp.int32, (tk, tq), 0)
    qpos = lax.broadcasted_iota(jnp.int32, (tk, tq), 1)

    def scores(qt, j, slot, valid):
        cms = []
        for h in heads:
            q = q_ref[h, pl.ds(pl.multiple_of(qt * tq, tq), tq), :]
            k = k_ref[h, pl.ds(pl.multiple_of(j * tk, tk), tk), :]
            s = lax.dot_general(k, q, (((1,), (1,)), ((), ())), preferred_element_type=F32)
            if valid is not None:
                s = jnp.where(valid, s, -jnp.inf)
            s_scr[2 * h + slot][...] = s
            cms.append(jnp.max(s, axis=0, keepdims=True))
        return tuple(cms)

    def accumulate(j, slot, cmax, ml):
        out = []
        for h in heads:
            m, l = ml[h]
            m_new = jnp.maximum(m, cmax[h])
            alpha = jnp.exp2(m - m_new)
            p = jnp.exp2(s_scr[2 * h + slot][...] - m_new)
            l_new = alpha * l + jnp.sum(p, axis=0, keepdims=True)
            pv = jnp.dot(vt_ref[h, j], p.astype(BF16), preferred_element_type=F32)
            acc_scr[h] = alpha * acc_scr[h] + pv
            out.append((m_new, l_new))
        return tuple(out)

    def pair(qi, p, carry, second_valid):
        ml, cm0 = carry
        cm1 = scores(qi, 2 * p + 1, 1, None)
        ml = accumulate(2 * p, 0, cm0, ml)
        cm0 = scores(qi, 2 * p + 2, 0, second_valid)
        ml = accumulate(2 * p + 1, 1, cm1, ml)
        return ml, cm0

    def query_tile(qi, cm0):
        acc_scr[...] = jnp.zeros_like(acc_scr)
        ml0 = tuple((jnp.full((1, tq), -jnp.inf, F32), jnp.zeros((1, tq), F32)) for _ in heads)
        carry = lax.fori_loop(0, qi - 1, lambda p, c: pair(qi, p, c, None), (ml0, cm0))
        carry = lax.cond(qi > 0, lambda c: pair(qi, qi - 1, c, kpos <= qpos), lambda c: c, carry)
        ml, cm0 = carry
        cm1 = scores(qi, 2 * qi + 1, 1, kpos + tk <= qpos)
        ml = accumulate(2 * qi, 0, cm0, ml)
        cm0 = scores(jnp.minimum(qi + 1, nq - 1), 0, 0, None)
        ml = accumulate(2 * qi + 1, 1, cm1, ml)
        for h in heads:
            o = acc_scr[h, 0:HEAD_DIM, :] / ml[h][1]
            o_ref[h, 2 * qi] = o[:, :tk]
            o_ref[h, 2 * qi + 1] = o[:, tk:]
        return cm0

    lax.fori_loop(0, nq, query_tile, scores(0, 0, 0, kpos <= qpos))


def _fox_attention(q_aug, k_aug, vt_aug):
    b, h, s, _ = k_aug.shape
    tk = TOKEN_TILE
    nk = s // tk
    hp = ATTN_HEADS_PER_STEP
    return pl.pallas_call(
        _attn_body,
        grid=(b, h // hp),
        in_specs=[
            pl.BlockSpec((None, hp, s, LANES), lambda i, j: (i, j, 0, 0)),
            pl.BlockSpec((None, hp, s, LANES), lambda i, j: (i, j, 0, 0)),
            pl.BlockSpec((None, hp, nk, HEAD_DIM, tk), lambda i, j: (i, j, 0, 0, 0)),
        ],
        out_specs=pl.BlockSpec((None, hp, nk, HEAD_DIM, tk), lambda i, j: (i, j, 0, 0, 0)),
        out_shape=jax.ShapeDtypeStruct((b, h, nk, HEAD_DIM, tk), F32),
        scratch_shapes=[pltpu.VMEM((hp, HEAD_DIM, ATTN_Q_TILE), F32)]
        + [pltpu.VMEM((tk, ATTN_Q_TILE), F32)] * (2 * hp),
        compiler_params=_params(2),
        name="fox_attention",
    )(q_aug, k_aug, vt_aug)


def _out_body(x_ref, mod_ref, pre_ref, post_ref, ot_ref, wg_ref, wo_ref, o_ref):
    tm, d = x_ref.shape
    x = x_ref[...]
    h = _prenorm(x, pre_ref[...], mod_ref[0:1, :], mod_ref[1:2, :]).astype(BF16)
    gate = _sigmoid(jnp.dot(h, wg_ref[...], preferred_element_type=F32))
    o = ot_ref[...].reshape(d, tm).T
    y = jnp.dot((o * gate).astype(BF16), wo_ref[...], preferred_element_type=F32)
    o_ref[...] = x + mod_ref[2:3, :] * _rms(y, post_ref[...])


def _fox_output(x, mod, pre_g, post_g, o_t, w_g, w_o):
    b, s, d = x.shape
    tm = TOKEN_TILE
    tok = pl.BlockSpec((None, tm, d), lambda i, j: (i, j, 0))
    return pl.pallas_call(
        _out_body,
        grid=(b, s // tm),
        in_specs=[
            tok,
            pl.BlockSpec((None, 6, d), lambda i, j: (i, 0, 0)),
            _const_spec((1, d)), _const_spec((1, d)),
            pl.BlockSpec((None, N_HEADS, None, HEAD_DIM, tm), lambda i, j: (i, 0, j, 0, 0)),
            _const_spec((d, d)), _const_spec((d, d)),
        ],
        out_specs=tok,
        out_shape=jax.ShapeDtypeStruct((b, s, d), F32),
        compiler_params=_params(2),
        name="fox_output",
    )(x, mod, pre_g, post_g, o_t, w_g, w_o)


def kernel(x, c, ada_w, ada_b, pre_mix_g, post_mix_g, pre_ffn_g, post_ffn_g, ffn_w_gu, ffn_w_down,
           a_w_in, a_b_in, a_ln_g, a_ln_b, a_w_s, a_b_s, a_w_out, kv_ada_w, kv_ada_b, kv_norm_g,
           kv_w, kv_b_f, k_norm_g, b_w_qg, b_q_norm_g, b_w_o):
    b, s, d = x.shape
    depth = ada_w.shape[0]
    n_a = a_w_in.shape[0]
    assert d == N_HEADS * HEAD_DIM and s % ATTN_Q_TILE == 0

    c8 = jnp.pad(c, ((0, 8 - b), (0, 0)))
    mods = _modulation(c8, ada_w, ada_b)[:, :b].reshape(depth, b, 6, d)
    kvmod = _modulation(c8, kv_ada_w[None], kv_ada_b[None])[0, :b].reshape(b, 2, d)

    row = lambda v: v.reshape(1, -1)
    q_scale = LOG2E * HEAD_DIM ** -0.5

    for layer in range(depth):
        mod = mods[layer]
        if layer < n_a:
            i = layer
            bs_full = jnp.repeat(a_b_s[i].T, CHUNK, axis=1)
            x = _gmlp_layer(x, mod, row(pre_mix_g[layer]), row(post_mix_g[layer]),
                            a_w_in[i].astype(BF16), row(a_b_in[i]), row(a_ln_g[i]), row(a_ln_b[i]),
                            a_w_s[i], bs_full, a_w_out[i].astype(BF16))
        else:
            j = layer - n_a
            qg2 = row(jnp.tile(b_q_norm_g[j] * q_scale, 2))
            q_aug = _q_project(x, mod, row(pre_mix_g[layer]), b_w_qg[j][:, :d].astype(BF16), qg2, d3)
            o_t = _fox_attention(q_aug, k_aug, vt_aug)
            x = _fox_output(x, mod, row(pre_mix_g[layer]), row(post_mix_g[layer]), o_t,
                            b_w_qg[j][:, d:].astype(BF16), b_w_o[j].astype(BF16))
        x = _ffn_layer(x, mod, row(pre_ffn_g[layer]), row(post_ffn_g[layer]),
                       ffn_w_gu[layer].astype(BF16), ffn_w_down[layer].astype(BF16))
        if layer == n_a - 1:
            w_f = kv_w[:, 2 * d:]
            pad = jnp.zeros((d, LANES - 3 * N_HEADS), F32)
            w_f3 = jnp.concatenate([w_f, w_f, w_f, pad], axis=1).astype(BF16)
            b_f3 = row(jnp.concatenate([kv_b_f, kv_b_f, kv_b_f, jnp.zeros((LANES - 3 * N_HEADS,), F32)]))
            k_aug, vt_aug, d3 = _kv_project(
                x, kvmod, row(kv_norm_g), kv_w[:, :d].astype(BF16), kv_w[:, d:2 * d].astype(BF16),
                w_f3, b_f3, row(jnp.tile(k_norm_g, 2)))
    return x
```

```python
import functools

import jax
import jax.numpy as jnp
from jax import lax
from jax.experimental import pallas as pl
from jax.experimental.pallas import tpu as pltpu

F32 = jnp.float32
BF16 = jnp.bfloat16

EPS = 1e-6
N_HEADS = 16
HEAD_DIM = 64
CHUNK = 128
GROUPS = 16
LOG2E = 1.4426950408889634
LANES = 128
MXU_N = 256
TOKEN_TILE = 512
ATTN_Q_TILE = 2 * TOKEN_TILE
ATTN_HEADS_PER_STEP = 2
VMEM_LIMIT = 56 * 1024 * 1024


def _sigmoid(x):
    return 1.0 / (1.0 + jnp.exp(-x))


def _rms(x, g):
    return x * lax.rsqrt(jnp.mean(x * x, axis=-1, keepdims=True) + EPS) * g


def _prenorm(x, g, shift, scale):
    return _rms(x, g) * (1.0 + scale) + shift


def _split3(x):
    hi = x.astype(BF16).astype(F32)
    r = x - hi
    mid = r.astype(BF16).astype(F32)
    lo = (r - mid).astype(BF16).astype(F32)
    return hi, mid, lo


def _const_spec(shape):
    n = len(shape)
    return pl.BlockSpec(shape, lambda *_: (0,) * n, pipeline_mode=pl.Buffered(1))


def _params(n_grid):
    return pltpu.CompilerParams(
        dimension_semantics=("arbitrary",) * n_grid, vmem_limit_bytes=VMEM_LIMIT)


def _mod_body(c_ref, w_ref, b_ref, o_ref):
    c = c_ref[...]
    ca = (c * _sigmoid(c)).astype(BF16)
    o_ref[...] = jnp.dot(ca, w_ref[...].astype(BF16), preferred_element_type=F32) + b_ref[...]


def _modulation(c8, w, b):
    n_l, d, n = w.shape
    nb = 1024
    return pl.pallas_call(
        _mod_body,
        grid=(n_l, n // nb),
        in_specs=[
            pl.BlockSpec((8, d), lambda l, j: (0, 0)),
            pl.BlockSpec((None, d, nb), lambda l, j: (l, 0, j)),
            pl.BlockSpec((None, 1, nb), lambda l, j: (l, 0, j)),
        ],
        out_specs=pl.BlockSpec((None, 8, nb), lambda l, j: (l, 0, j)),
        out_shape=jax.ShapeDtypeStruct((n_l, 8, n), F32),
        compiler_params=_params(2),
        name="modulation",
    )(c8, w, b.reshape(n_l, 1, n))


def _gmlp_body(x_ref, mod_ref, pre_ref, post_ref, win_ref, bin_ref, lng_ref, lnb_ref, ws_ref,
               bs_ref, wout_ref, o_ref, h_scr, u_scr, v_scr, vn_scr, y_scr):
    tm, d = x_ref.shape
    gw = u_scr.shape[1]
    x = x_ref[...]
    h_scr[...] = _prenorm(x, pre_ref[...], mod_ref[0:1, :], mod_ref[1:2, :]).astype(BF16)

    k_gelu = 0.7978845608028654
    nc = 512
    for c0 in range(0, 2 * gw, nc):
        z = jnp.dot(h_scr[...], win_ref[:, c0:c0 + nc], preferred_element_type=F32)
        z = z + bin_ref[:, c0:c0 + nc]
        z = z * (0.5 * (1.0 + jnp.tanh(k_gelu * (z + 0.044715 * (z * z * z)))))
        if c0 < gw:
            u_scr[:, c0:c0 + nc] = z
        else:
            v_scr[:, c0 - gw:c0 - gw + nc] = z

    v = v_scr[...]
    mu = jnp.mean(v, axis=-1, keepdims=True)
    vc = v - mu
    rstd = lax.rsqrt(jnp.mean(vc * vc, axis=-1, keepdims=True) + EPS)
    vn_scr[...] = (vc * rstd * lng_ref[...] + lnb_ref[...]).astype(BF16)

    n_chunks = tm // CHUNK
    row = lax.broadcasted_iota(jnp.int32, (CHUNK, CHUNK), 0)
    col = lax.broadcasted_iota(jnp.int32, (CHUNK, CHUNK), 1)
    causal = col <= row
    for g in range(GROUPS):
        gs = slice(g * CHUNK, (g + 1) * CHUNK)
        ws = jnp.where(causal, ws_ref[g], 0.0).astype(BF16)
        rhs = jnp.concatenate(
            [vn_scr[c * CHUNK:(c + 1) * CHUNK, gs] for c in range(n_chunks)], axis=1)
        sp = jnp.dot(ws, rhs, preferred_element_type=F32)
        for c in range(n_chunks):
            cs = slice(c * CHUNK, (c + 1) * CHUNK)
            y_scr[cs, gs] = (u_scr[cs, gs] * (sp[:, cs] + bs_ref[:, gs])).astype(BF16)

    y = jnp.dot(y_scr[...], wout_ref[...], preferred_element_type=F32)
    o_ref[...] = x + mod_ref[2:3, :] * _rms(y, post_ref[...])


def _gmlp_layer(x, mod, pre_g, post_g, w_in, b_in, ln_g, ln_b, w_s, bs_full, w_out):
    b, s, d = x.shape
    gw = w_out.shape[0]
    tm = TOKEN_TILE
    tok = pl.BlockSpec((None, tm, d), lambda i, j: (i, j, 0))
    return pl.pallas_call(
        _gmlp_body,
        grid=(b, s // tm),
        in_specs=[
            tok,
            pl.BlockSpec((None, 6, d), lambda i, j: (i, 0, 0)),
            _const_spec((1, d)), _const_spec((1, d)),
            _const_spec((d, 2 * gw)), _const_spec((1, 2 * gw)),
            _const_spec((1, gw)), _const_spec((1, gw)),
            _const_spec((GROUPS, CHUNK, CHUNK)), _const_spec((CHUNK, gw)),
            _const_spec((gw, d)),
        ],
        out_specs=tok,
        out_shape=jax.ShapeDtypeStruct((b, s, d), F32),
        scratch_shapes=[
            pltpu.VMEM((tm, d), BF16), pltpu.VMEM((tm, gw), F32), pltpu.VMEM((tm, gw), F32),
            pltpu.VMEM((tm, gw), BF16), pltpu.VMEM((tm, gw), BF16),
        ],
        compiler_params=_params(2),
        name="gmlp_mixer",
    )(x, mod, pre_g, post_g, w_in, b_in, ln_g, ln_b, w_s, bs_full, w_out)


def _ffn_body(x_ref, mod_ref, pre_ref, post_ref, wgu_ref, wd_ref, o_ref, h_scr, a_scr):
    f = a_scr.shape[1]
    x = x_ref[...]
    h_scr[...] = _prenorm(x, pre_ref[...], mod_ref[3:4, :], mod_ref[4:5, :]).astype(BF16)
    for c0 in range(0, f, MXU_N):
        g = jnp.dot(h_scr[...], wgu_ref[:, c0:c0 + MXU_N], preferred_element_type=F32)
        u = jnp.dot(h_scr[...], wgu_ref[:, f + c0:f + c0 + MXU_N], preferred_element_type=F32)
        a_scr[:, c0:c0 + MXU_N] = (g * _sigmoid(g) * u).astype(BF16)
    y = jnp.dot(a_scr[...], wd_ref[...], preferred_element_type=F32)
    o_ref[...] = x + mod_ref[5:6, :] * _rms(y, post_ref[...])


def _ffn_layer(x, mod, pre_g, post_g, w_gu, w_down):
    b, s, d = x.shape
    f = w_down.shape[0]
    tm = TOKEN_TILE
    tok = pl.BlockSpec((None, tm, d), lambda i, j: (i, j, 0))
    return pl.pallas_call(
        _ffn_body,
        grid=(b, s // tm),
        in_specs=[
            tok,
            pl.BlockSpec((None, 6, d), lambda i, j: (i, 0, 0)),
            _const_spec((1, d)), _const_spec((1, d)),
            _const_spec((d, 2 * f)), _const_spec((f, d)),
        ],
        out_specs=tok,
        out_shape=jax.ShapeDtypeStruct((b, s, d), F32),
        scratch_shapes=[pltpu.VMEM((tm, d), BF16), pltpu.VMEM((tm, f), BF16)],
        compiler_params=_params(2),
        name="swiglu_ffn",
    )(x, mod, pre_g, post_g, w_gu, w_down)


def _head_pair_norm(t, gain2):
    lane = lax.broadcasted_iota(jnp.int32, t.shape, 1)
    sq = t * t
    first = lane < HEAD_DIM
    s0 = jnp.sum(jnp.where(first, sq, 0.0), axis=-1, keepdims=True)
    s1 = jnp.sum(jnp.where(first, 0.0, sq), axis=-1, keepdims=True)
    rs = jnp.where(first, lax.rsqrt(s0 / HEAD_DIM + EPS), lax.rsqrt(s1 / HEAD_DIM + EPS))
    return t * rs * gain2


def _decay_selectors(query_side):
    row = lax.broadcasted_iota(jnp.int32, (LANES, N_HEADS * LANES), 0)
    col = lax.broadcasted_iota(jnp.int32, (LANES, N_HEADS * LANES), 1)
    head, off = col // LANES, col % LANES - HEAD_DIM
    first, second = (off >= 0) & (off < 3), (off >= 3) & (off < 6)
    piece_cols, piece_idx = (first, off) if query_side else (second, off - 3)
    one_cols = second if query_side else first
    is_piece = piece_cols & (row == piece_idx * N_HEADS + head)
    is_one = one_cols & (row == 3 * N_HEADS)
    sel = jnp.where(is_piece, 1.0 if query_side else -1.0, jnp.where(is_one, 1.0, 0.0))
    return sel.astype(BF16)


def _with_tail(t, tail):
    lane = lax.broadcasted_iota(jnp.int32, t.shape, 1)
    return jnp.where(lane < HEAD_DIM, t, tail)


def _kv_body(x_ref, mod_ref, g_ref, wk_ref, wv_ref, wf_ref, bf_ref, kg_ref, sel_ref,
             k_out, vt_out, d_out, h_scr, carry_scr):
    tm, d = x_ref.shape

    @pl.when(pl.program_id(1) == 0)
    def _():
        carry_scr[...] = jnp.zeros_like(carry_scr)

    h_scr[...] = _prenorm(x_ref[...], g_ref[...], mod_ref[0:1, :], mod_ref[1:2, :]).astype(BF16)

    lane = lax.broadcasted_iota(jnp.int32, (tm, LANES), 1)
    fl = jnp.dot(h_scr[...], wf_ref[...], preferred_element_type=F32) + bf_ref[...]
    z = -fl
    ls = -(jnp.maximum(z, 0.0) + jnp.log1p(jnp.exp(-jnp.abs(z))))
    hi, mid, lo = _split3(ls)
    pieces = jnp.where(lane < N_HEADS, hi, jnp.where(lane < 2 * N_HEADS, mid, jnp.where(
        lane < 3 * N_HEADS, lo, 0.0))).astype(BF16)
    r = lax.broadcasted_iota(jnp.int32, (tm, tm), 0)
    c = lax.broadcasted_iota(jnp.int32, (tm, tm), 1)
    tri = jnp.where(c <= r, 1.0, 0.0).astype(BF16)
    cs = jnp.dot(tri, pieces, preferred_element_type=F32)
    tot = cs + pltpu.roll(cs, LANES - N_HEADS, axis=1) + pltpu.roll(cs, LANES - 2 * N_HEADS, axis=1)
    dc = jnp.where(lane < N_HEADS, tot + carry_scr[...], 0.0)
    carry_scr[...] = dc[tm - 1:tm, :]
    hi, mid, lo = _split3(dc * LOG2E)
    d3 = (hi + pltpu.roll(mid, N_HEADS, axis=1) + pltpu.roll(lo, 2 * N_HEADS, axis=1)
          + jnp.where(lane == 3 * N_HEADS, 1.0, 0.0)).astype(BF16)
    d_out[...] = d3

    kg2 = kg_ref[...]
    heads_per_chunk = MXU_N // HEAD_DIM
    for c4 in range(d // MXU_N):
        kc = jnp.dot(h_scr[...], wk_ref[:, c4 * MXU_N:(c4 + 1) * MXU_N], preferred_element_type=F32)
        vc = jnp.dot(h_scr[...], wv_ref[:, c4 * MXU_N:(c4 + 1) * MXU_N], preferred_element_type=F32)
        tails = jnp.dot(d3, sel_ref[:, c4 * heads_per_chunk * LANES:(c4 + 1) * heads_per_chunk * LANES],
                        preferred_element_type=F32)
        for half in range(2):
            kn = _head_pair_norm(kc[:, half * LANES:(half + 1) * LANES], kg2)
            vt = vc[:, half * LANES:(half + 1) * LANES].T
            for sub, src in ((0, kn), (1, pltpu.roll(kn, HEAD_DIM, axis=1))):
                loc = 2 * half + sub
                hd = c4 * heads_per_chunk + loc
                k_out[hd] = _with_tail(src, tails[:, loc * LANES:(loc + 1) * LANES]).astype(BF16)
                vt_out[hd] = vt[sub * HEAD_DIM:(sub + 1) * HEAD_DIM, :].astype(BF16)


def _kv_project(x, kvmod, norm_g, w_k, w_v, w_f3, b_f3, kg2):
    b, s, d = x.shape
    tm = TOKEN_TILE
    return pl.pallas_call(
        _kv_body,
        grid=(b, s // tm),
        in_specs=[
            pl.BlockSpec((None, tm, d), lambda i, j: (i, j, 0)),
            pl.BlockSpec((None, 2, d), lambda i, j: (i, 0, 0)),
            _const_spec((1, d)),
            _const_spec((d, d)), _const_spec((d, d)), _const_spec((d, LANES)),
            _const_spec((1, LANES)), _const_spec((1, LANES)), _const_spec((LANES, N_HEADS * LANES)),
        ],
        out_specs=[
            pl.BlockSpec((None, N_HEADS, tm, LANES), lambda i, j: (i, 0, j, 0)),
            pl.BlockSpec((None, N_HEADS, None, HEAD_DIM, tm), lambda i, j: (i, 0, j, 0, 0)),
            pl.BlockSpec((None, tm, LANES), lambda i, j: (i, j, 0)),
        ],
        out_shape=[
            jax.ShapeDtypeStruct((b, N_HEADS, s, LANES), BF16),
            jax.ShapeDtypeStruct((b, N_HEADS, s // tm, HEAD_DIM, tm), BF16),
            jax.ShapeDtypeStruct((b, s, LANES), BF16),
        ],
        scratch_shapes=[pltpu.VMEM((tm, d), BF16), pltpu.VMEM((1, LANES), F32)],
        compiler_params=_params(2),
        name="kv_project",
    )(x, kvmod, norm_g, w_k, w_v, w_f3, b_f3, kg2, _decay_selectors(query_side=False))


def _q_body(x_ref, mod_ref, pre_ref, wq_ref, qg_ref, d_ref, sel_ref, q_out, h_scr):
    tm, d = x_ref.shape
    h_scr[...] = _prenorm(x_ref[...], pre_ref[...], mod_ref[0:1, :], mod_ref[1:2, :]).astype(BF16)
    d3 = d_ref[...]
    qg2 = qg_ref[...]
    heads_per_chunk = MXU_N // HEAD_DIM
    for c4 in range(d // MXU_N):
        qc = jnp.dot(h_scr[...], wq_ref[:, c4 * MXU_N:(c4 + 1) * MXU_N], preferred_element_type=F32)
        tails = jnp.dot(d3, sel_ref[:, c4 * heads_per_chunk * LANES:(c4 + 1) * heads_per_chunk * LANES],
                        preferred_element_type=F32)
        for half in range(2):
            qn = _head_pair_norm(qc[:, half * LANES:(half + 1) * LANES], qg2)
            for sub, src in ((0, qn), (1, pltpu.roll(qn, HEAD_DIM, axis=1))):
                loc = 2 * half + sub
                qa = _with_tail(src, tails[:, loc * LANES:(loc + 1) * LANES])
                q_out[c4 * heads_per_chunk + loc] = qa.astype(BF16)


def _q_project(x, mod, pre_g, w_q, qg2, d3):
    b, s, d = x.shape
    tm = TOKEN_TILE
    return pl.pallas_call(
        _q_body,
        grid=(b, s // tm),
        in_specs=[
            pl.BlockSpec((None, tm, d), lambda i, j: (i, j, 0)),
            pl.BlockSpec((None, 6, d), lambda i, j: (i, 0, 0)),
            _const_spec((1, d)), _const_spec((d, d)), _const_spec((1, LANES)),
            pl.BlockSpec((None, tm, LANES), lambda i, j: (i, j, 0)),
            _const_spec((LANES, N_HEADS * LANES)),
        ],
        out_specs=pl.BlockSpec((None, N_HEADS, tm, LANES), lambda i, j: (i, 0, j, 0)),
        out_shape=jax.ShapeDtypeStruct((b, N_HEADS, s, LANES), BF16),
        scratch_shapes=[pltpu.VMEM((tm, d), BF16)],
        compiler_params=_params(2),
        name="q_project",
    )(x, mod, pre_g, w_q, qg2, d3, _decay_selectors(query_side=True))


def _attn_body(q_ref, k_ref, vt_ref, o_ref, acc_scr, *s_scr):
    n_h = q_ref.shape[0]
    tk = vt_ref.shape[3]
    tq = 2 * tk
    nq = q_ref.shape[1] // tq
    heads = range(n_h)
    halves = (slice(0, tk), slice(tk, tq))
    kpos = lax.broadcasted_iota(jnp.int32, (tk, tk), 0)
    qpos = lax.broadcasted_iota(jnp.int32, (tk, tk), 1)
    tri = kpos <= qpos
    LO, HI = 0, 1

    def scores(qt, j, slot, g, valid):
        cms = []
        for h in heads:
            q = q_ref[h, pl.ds(pl.multiple_of(qt * tq + g * tk, tk), tk), :]
            k = k_ref[h, pl.ds(pl.multiple_of(j * tk, tk), tk), :]
            s = lax.dot_general(k, q, (((1,), (1,)), ((), ())), preferred_element_type=F32)
            if valid is not None:
                s = jnp.where(valid, s, -jnp.inf)
            s_scr[2 * h + slot][:, halves[g]] = s
            cms.append(jnp.max(s, axis=0, keepdims=True))
        return tuple(cms)

    def accumulate(j, slot, g, cmax, ml):
        out = []
        for h in heads:
            m, l = ml[h]
            m_new = jnp.maximum(m, cmax[h])
            alpha = jnp.exp2(m - m_new)
            p = jnp.exp2(s_scr[2 * h + slot][:, halves[g]] - m_new)
            l_new = alpha * l + jnp.sum(p, axis=0, keepdims=True)
            pv = jnp.dot(vt_ref[h, j], p.astype(BF16), preferred_element_type=F32)
            acc_scr[h, :, halves[g]] = alpha * acc_scr[h, :, halves[g]] + pv
            out.append((m_new, l_new))
        return tuple(out)

    def pair(qi, p, carry, diag):
        (ml_lo, ml_hi), (c0_lo, c0_hi) = carry
        c1_lo = scores(qi, 2 * p + 1, 1, LO, None)
        ml_lo = accumulate(2 * p, 0, LO, c0_lo, ml_lo)
        c1_hi = scores(qi, 2 * p + 1, 1, HI, None)
        ml_hi = accumulate(2 * p, 0, HI, c0_hi, ml_hi)
        c0_lo = scores(qi, 2 * p + 2, 0, LO, tri if diag else None)
        ml_lo = accumulate(2 * p + 1, 1, LO, c1_lo, ml_lo)
        c0_hi = scores(qi, 2 * p + 2, 0, HI, None)
        ml_hi = accumulate(2 * p + 1, 1, HI, c1_hi, ml_hi)
        return (ml_lo, ml_hi), (c0_lo, c0_hi)

    def query_tile(qi, cm0):
        acc_scr[...] = jnp.zeros_like(acc_scr)
        fresh = tuple((jnp.full((1, tk), -jnp.inf, F32), jnp.zeros((1, tk), F32)) for _ in heads)
        carry = lax.fori_loop(0, qi - 1, lambda p, c: pair(qi, p, c, False), ((fresh, fresh), cm0))
        carry = lax.cond(qi > 0, lambda c: pair(qi, qi - 1, c, True), lambda c: c, carry)
        (ml_lo, ml_hi), (c0_lo, c0_hi) = carry
        ml_lo = accumulate(2 * qi, 0, LO, c0_lo, ml_lo)
        c1_hi = scores(qi, 2 * qi + 1, 1, HI, tri)
        ml_hi = accumulate(2 * qi, 0, HI, c0_hi, ml_hi)
        nxt = jnp.minimum(qi + 1, nq - 1)
        n_lo = scores(nxt, 0, 0, LO, None)
        ml_hi = accumulate(2 * qi + 1, 1, HI, c1_hi, ml_hi)
        n_hi = scores(nxt, 0, 0, HI, None)
        for h in heads:
            o_ref[h, 2 * qi] = acc_scr[h, :, halves[LO]] / ml_lo[h][1]
            o_ref[h, 2 * qi + 1] = acc_scr[h, :, halves[HI]] / ml_hi[h][1]
        return n_lo, n_hi

    lax.fori_loop(0, nq, query_tile, (scores(0, 0, 0, LO, tri), scores(0, 0, 0, HI, None)))


def _fox_attention(q_aug, k_aug, vt_aug):
    b, h, s, _ = k_aug.shape
    tk = TOKEN_TILE
    nk = s // tk
    hp = ATTN_HEADS_PER_STEP
    return pl.pallas_call(
        _attn_body,
        grid=(b, h // hp),
        in_specs=[
            pl.BlockSpec((None, hp, s, LANES), lambda i, j: (i, j, 0, 0)),
            pl.BlockSpec((None, hp, s, LANES), lambda i, j: (i, j, 0, 0)),
            pl.BlockSpec((None, hp, nk, HEAD_DIM, tk), lambda i, j: (i, j, 0, 0, 0)),
        ],
        out_specs=pl.BlockSpec((None, hp, nk, HEAD_DIM, tk), lambda i, j: (i, j, 0, 0, 0)),
        out_shape=jax.ShapeDtypeStruct((b, h, nk, HEAD_DIM, tk), F32),
        scratch_shapes=[pltpu.VMEM((hp, HEAD_DIM, ATTN_Q_TILE), F32)]
        + [pltpu.VMEM((tk, ATTN_Q_TILE), F32)] * (2 * hp),
        compiler_params=_params(2),
        name="fox_attention",
    )(q_aug, k_aug, vt_aug)


def _out_body(x_ref, mod_ref, pre_ref, post_ref, ot_ref, wg_ref, wo_ref, o_ref):
    tm, d = x_ref.shape
    x = x_ref[...]
    h = _prenorm(x, pre_ref[...], mod_ref[0:1, :], mod_ref[1:2, :]).astype(BF16)
    gate = _sigmoid(jnp.dot(h, wg_ref[...], preferred_element_type=F32))
    o = ot_ref[...].reshape(d, tm).T
    y = jnp.dot((o * gate).astype(BF16), wo_ref[...], preferred_element_type=F32)
    o_ref[...] = x + mod_ref[2:3, :] * _rms(y, post_ref[...])


def _fox_output(x, mod, pre_g, post_g, o_t, w_g, w_o):
    b, s, d = x.shape
    tm = TOKEN_TILE
    tok = pl.BlockSpec((None, tm, d), lambda i, j: (i, j, 0))
    return pl.pallas_call(
        _out_body,
        grid=(b, s // tm),
        in_specs=[
            tok,
            pl.BlockSpec((None, 6, d), lambda i, j: (i, 0, 0)),
            _const_spec((1, d)), _const_spec((1, d)),
            pl.BlockSpec((None, N_HEADS, None, HEAD_DIM, tm), lambda i, j: (i, 0, j, 0, 0)),
            _const_spec((d, d)), _const_spec((d, d)),
        ],
        out_specs=tok,
        out_shape=jax.ShapeDtypeStruct((b, s, d), F32),
        compiler_params=_params(2),
        name="fox_output",
    )(x, mod, pre_g, post_g, o_t, w_g, w_o)


def kernel(x, c, ada_w, ada_b, pre_mix_g, post_mix_g, pre_ffn_g, post_ffn_g, ffn_w_gu, ffn_w_down,
           a_w_in, a_b_in, a_ln_g, a_ln_b, a_w_s, a_b_s, a_w_out, kv_ada_w, kv_ada_b, kv_norm_g,
           kv_w, kv_b_f, k_norm_g, b_w_qg, b_q_norm_g, b_w_o):
    b, s, d = x.shape
    depth = ada_w.shape[0]
    n_a = a_w_in.shape[0]
    assert d == N_HEADS * HEAD_DIM and s % ATTN_Q_TILE == 0

    c8 = jnp.pad(c, ((0, 8 - b), (0, 0)))
    mods = _modulation(c8, ada_w, ada_b)[:, :b].reshape(depth, b, 6, d)
    kvmod = _modulation(c8, kv_ada_w[None], kv_ada_b[None])[0, :b].reshape(b, 2, d)

    row = lambda v: v.reshape(1, -1)
    q_scale = LOG2E * HEAD_DIM ** -0.5

    for layer in range(depth):
        mod = mods[layer]
        if layer < n_a:
            i = layer
            bs_full = jnp.repeat(a_b_s[i].T, CHUNK, axis=1)
            x = _gmlp_layer(x, mod, row(pre_mix_g[layer]), row(post_mix_g[layer]),
                            a_w_in[i].astype(BF16), row(a_b_in[i]), row(a_ln_g[i]), row(a_ln_b[i]),
                            a_w_s[i], bs_full, a_w_out[i].astype(BF16))
        else:
            j = layer - n_a
            qg2 = row(jnp.tile(b_q_norm_g[j] * q_scale, 2))
            q_aug = _q_project(x, mod, row(pre_mix_g[layer]), b_w_qg[j][:, :d].astype(BF16), qg2, d3)
            o_t = _fox_attention(q_aug, k_aug, vt_aug)
            x = _fox_output(x, mod, row(pre_mix_g[layer]), row(post_mix_g[layer]), o_t,
                            b_w_qg[j][:, d:].astype(BF16), b_w_o[j].astype(BF16))
        x = _ffn_layer(x, mod, row(pre_ffn_g[layer]), row(post_ffn_g[layer]),
                       ffn_w_gu[layer].astype(BF16), ffn_w_down[layer].astype(BF16))
        if layer == n_a - 1:
            w_f = kv_w[:, 2 * d:]
            pad = jnp.zeros((d, LANES - 3 * N_HEADS), F32)
            w_f3 = jnp.concatenate([w_f, w_f, w_f, pad], axis=1).astype(BF16)
            b_f3 = row(jnp.concatenate([kv_b_f, kv_b_f, kv_b_f, jnp.zeros((LANES - 3 * N_HEADS,), F32)]))
            k_aug, vt_aug, d3 = _kv_project(
                x, kvmod, row(kv_norm_g), kv_w[:, :d].astype(BF16), kv_w[:, d:2 * d].astype(BF16),
                w_f3, b_f3, row(jnp.tile(k_norm_g, 2)))
    return x
```

```python
import functools

import jax
import jax.numpy as jnp
from jax import lax
from jax.experimental import pallas as pl
from jax.experimental.pallas import tpu as pltpu

F32 = jnp.float32
BF16 = jnp.bfloat16

EPS = 1e-6
N_HEADS = 16
HEAD_DIM = 64
CHUNK = 128
GROUPS = 16
LOG2E = 1.4426950408889634
LANES = 128
MXU_N = 256
TOKEN_TILE = 512
ATTN_Q_TILE = 2 * TOKEN_TILE
ATTN_HEADS_PER_STEP = 2
BF16_SUBLANES = 16
V_ROWS = HEAD_DIM + BF16_SUBLANES
VMEM_LIMIT = 56 * 1024 * 1024


def _sigmoid(x):
    return 1.0 / (1.0 + jnp.exp(-x))


def _rms(x, g):
    return x * lax.rsqrt(jnp.mean(x * x, axis=-1, keepdims=True) + EPS) * g


def _prenorm(x, g, shift, scale):
    return _rms(x, g) * (1.0 + scale) + shift


def _split3(x):
    hi = x.astype(BF16).astype(F32)
    r = x - hi
    mid = r.astype(BF16).astype(F32)
    lo = (r - mid).astype(BF16).astype(F32)
    return hi, mid, lo


def _const_spec(shape):
    n = len(shape)
    return pl.BlockSpec(shape, lambda *_: (0,) * n, pipeline_mode=pl.Buffered(1))


def _params(n_grid):
    return pltpu.CompilerParams(
        dimension_semantics=("arbitrary",) * n_grid, vmem_limit_bytes=VMEM_LIMIT)


def _mod_body(c_ref, w_ref, b_ref, o_ref):
    c = c_ref[...]
    ca = (c * _sigmoid(c)).astype(BF16)
    o_ref[...] = jnp.dot(ca, w_ref[...].astype(BF16), preferred_element_type=F32) + b_ref[...]


def _modulation(c8, w, b):
    n_l, d, n = w.shape
    nb = 1024
    return pl.pallas_call(
        _mod_body,
        grid=(n_l, n // nb),
        in_specs=[
            pl.BlockSpec((8, d), lambda l, j: (0, 0)),
            pl.BlockSpec((None, d, nb), lambda l, j: (l, 0, j)),
            pl.BlockSpec((None, 1, nb), lambda l, j: (l, 0, j)),
        ],
        out_specs=pl.BlockSpec((None, 8, nb), lambda l, j: (l, 0, j)),
        out_shape=jax.ShapeDtypeStruct((n_l, 8, n), F32),
        compiler_params=_params(2),
        name="modulation",
    )(c8, w, b.reshape(n_l, 1, n))


def _gmlp_body(x_ref, mod_ref, pre_ref, post_ref, win_ref, bin_ref, lng_ref, lnb_ref, ws_ref,
               bs_ref, wout_ref, o_ref, h_scr, u_scr, v_scr, vn_scr, y_scr):
    tm, d = x_ref.shape
    gw = u_scr.shape[1]
    x = x_ref[...]
    h_scr[...] = _prenorm(x, pre_ref[...], mod_ref[0:1, :], mod_ref[1:2, :]).astype(BF16)

    k_gelu = 0.7978845608028654
    nc = 512
    for c0 in range(0, 2 * gw, nc):
        z = jnp.dot(h_scr[...], win_ref[:, c0:c0 + nc], preferred_element_type=F32)
        z = z + bin_ref[:, c0:c0 + nc]
        z = z * (0.5 * (1.0 + jnp.tanh(k_gelu * (z + 0.044715 * (z * z * z)))))
        if c0 < gw:
            u_scr[:, c0:c0 + nc] = z
        else:
            v_scr[:, c0 - gw:c0 - gw + nc] = z

    v = v_scr[...]
    mu = jnp.mean(v, axis=-1, keepdims=True)
    vc = v - mu
    rstd = lax.rsqrt(jnp.mean(vc * vc, axis=-1, keepdims=True) + EPS)
    vn_scr[...] = (vc * rstd * lng_ref[...] + lnb_ref[...]).astype(BF16)

    n_chunks = tm // CHUNK
    row = lax.broadcasted_iota(jnp.int32, (CHUNK, CHUNK), 0)
    col = lax.broadcasted_iota(jnp.int32, (CHUNK, CHUNK), 1)
    causal = col <= row
    for g in range(GROUPS):
        gs = slice(g * CHUNK, (g + 1) * CHUNK)
        ws = jnp.where(causal, ws_ref[g], 0.0).astype(BF16)
        rhs = jnp.concatenate(
            [vn_scr[c * CHUNK:(c + 1) * CHUNK, gs] for c in range(n_chunks)], axis=1)
        sp = jnp.dot(ws, rhs, preferred_element_type=F32)
        for c in range(n_chunks):
            cs = slice(c * CHUNK, (c + 1) * CHUNK)
            y_scr[cs, gs] = (u_scr[cs, gs] * (sp[:, cs] + bs_ref[:, gs])).astype(BF16)

    y = jnp.dot(y_scr[...], wout_ref[...], preferred_element_type=F32)
    o_ref[...] = x + mod_ref[2:3, :] * _rms(y, post_ref[...])


def _gmlp_layer(x, mod, pre_g, post_g, w_in, b_in, ln_g, ln_b, w_s, bs_full, w_out):
    b, s, d = x.shape
    gw = w_out.shape[0]
    tm = TOKEN_TILE
    tok = pl.BlockSpec((None, tm, d), lambda i, j: (i, j, 0))
    return pl.pallas_call(
        _gmlp_body,
        grid=(b, s // tm),
        in_specs=[
            tok,
            pl.BlockSpec((None, 6, d), lambda i, j: (i, 0, 0)),
            _const_spec((1, d)), _const_spec((1, d)),
            _const_spec((d, 2 * gw)), _const_spec((1, 2 * gw)),
            _const_spec((1, gw)), _const_spec((1, gw)),
            _const_spec((GROUPS, CHUNK, CHUNK)), _const_spec((CHUNK, gw)),
            _const_spec((gw, d)),
        ],
        out_specs=tok,
        out_shape=jax.ShapeDtypeStruct((b, s, d), F32),
        scratch_shapes=[
            pltpu.VMEM((tm, d), BF16), pltpu.VMEM((tm, gw), F32), pltpu.VMEM((tm, gw), F32),
            pltpu.VMEM((tm, gw), BF16), pltpu.VMEM((tm, gw), BF16),
        ],
        compiler_params=_params(2),
        name="gmlp_mixer",
    )(x, mod, pre_g, post_g, w_in, b_in, ln_g, ln_b, w_s, bs_full, w_out)


def _ffn_body(x_ref, mod_ref, pre_ref, post_ref, wgu_ref, wd_ref, o_ref, h_scr, a_scr):
    f = a_scr.shape[1]
    x = x_ref[...]
    h_scr[...] = _prenorm(x, pre_ref[...], mod_ref[3:4, :], mod_ref[4:5, :]).astype(BF16)
    for c0 in range(0, f, MXU_N):
        g = jnp.dot(h_scr[...], wgu_ref[:, c0:c0 + MXU_N], preferred_element_type=F32)
        u = jnp.dot(h_scr[...], wgu_ref[:, f + c0:f + c0 + MXU_N], preferred_element_type=F32)
        a_scr[:, c0:c0 + MXU_N] = (g * _sigmoid(g) * u).astype(BF16)
    y = jnp.dot(a_scr[...], wd_ref[...], preferred_element_type=F32)
    o_ref[...] = x + mod_ref[5:6, :] * _rms(y, post_ref[...])


def _ffn_layer(x, mod, pre_g, post_g, w_gu, w_down):
    b, s, d = x.shape
    f = w_down.shape[0]
    tm = TOKEN_TILE
    tok = pl.BlockSpec((None, tm, d), lambda i, j: (i, j, 0))
    return pl.pallas_call(
        _ffn_body,
        grid=(b, s // tm),
        in_specs=[
            tok,
            pl.BlockSpec((None, 6, d), lambda i, j: (i, 0, 0)),
            _const_spec((1, d)), _const_spec((1, d)),
            _const_spec((d, 2 * f)), _const_spec((f, d)),
        ],
        out_specs=tok,
        out_shape=jax.ShapeDtypeStruct((b, s, d), F32),
        scratch_shapes=[pltpu.VMEM((tm, d), BF16), pltpu.VMEM((tm, f), BF16)],
        compiler_params=_params(2),
        name="swiglu_ffn",
    )(x, mod, pre_g, post_g, w_gu, w_down)


def _head_pair_norm(t, gain2):
    lane = lax.broadcasted_iota(jnp.int32, t.shape, 1)
    sq = t * t
    first = lane < HEAD_DIM
    s0 = jnp.sum(jnp.where(first, sq, 0.0), axis=-1, keepdims=True)
    s1 = jnp.sum(jnp.where(first, 0.0, sq), axis=-1, keepdims=True)
    rs = jnp.where(first, lax.rsqrt(s0 / HEAD_DIM + EPS), lax.rsqrt(s1 / HEAD_DIM + EPS))
    return t * rs * gain2


def _decay_selectors(query_side):
    row = lax.broadcasted_iota(jnp.int32, (LANES, N_HEADS * LANES), 0)
    col = lax.broadcasted_iota(jnp.int32, (LANES, N_HEADS * LANES), 1)
    head, off = col // LANES, col % LANES - HEAD_DIM
    first, second = (off >= 0) & (off < 3), (off >= 3) & (off < 6)
    piece_cols, piece_idx = (first, off) if query_side else (second, off - 3)
    one_cols = second if query_side else first
    is_piece = piece_cols & (row == piece_idx * N_HEADS + head)
    is_one = one_cols & (row == 3 * N_HEADS)
    sel = jnp.where(is_piece, 1.0 if query_side else -1.0, jnp.where(is_one, 1.0, 0.0))
    return sel.astype(BF16)


def _with_tail(t, tail):
    lane = lax.broadcasted_iota(jnp.int32, t.shape, 1)
    return jnp.where(lane < HEAD_DIM, t, tail)


def _kv_body(x_ref, mod_ref, g_ref, wk_ref, wv_ref, wf_ref, bf_ref, kg_ref, sel_ref,
             k_out, vt_out, d_out, h_scr, carry_scr):
    tm, d = x_ref.shape

    @pl.when(pl.program_id(1) == 0)
    def _():
        carry_scr[...] = jnp.zeros_like(carry_scr)

    h_scr[...] = _prenorm(x_ref[...], g_ref[...], mod_ref[0:1, :], mod_ref[1:2, :]).astype(BF16)

    lane = lax.broadcasted_iota(jnp.int32, (tm, LANES), 1)
    fl = jnp.dot(h_scr[...], wf_ref[...], preferred_element_type=F32) + bf_ref[...]
    z = -fl
    ls = -(jnp.maximum(z, 0.0) + jnp.log1p(jnp.exp(-jnp.abs(z))))
    hi, mid, lo = _split3(ls)
    pieces = jnp.where(lane < N_HEADS, hi, jnp.where(lane < 2 * N_HEADS, mid, jnp.where(
        lane < 3 * N_HEADS, lo, 0.0))).astype(BF16)
    r = lax.broadcasted_iota(jnp.int32, (tm, tm), 0)
    c = lax.broadcasted_iota(jnp.int32, (tm, tm), 1)
    tri = jnp.where(c <= r, 1.0, 0.0).astype(BF16)
    cs = jnp.dot(tri, pieces, preferred_element_type=F32)
    tot = cs + pltpu.roll(cs, LANES - N_HEADS, axis=1) + pltpu.roll(cs, LANES - 2 * N_HEADS, axis=1)
    dc = jnp.where(lane < N_HEADS, tot + carry_scr[...], 0.0)
    carry_scr[...] = dc[tm - 1:tm, :]
    hi, mid, lo = _split3(dc * LOG2E)
    d3 = (hi + pltpu.roll(mid, N_HEADS, axis=1) + pltpu.roll(lo, 2 * N_HEADS, axis=1)
          + jnp.where(lane == 3 * N_HEADS, 1.0, 0.0)).astype(BF16)
    d_out[...] = d3

    kg2 = kg_ref[...]
    ones_rows = jnp.where(
        lax.broadcasted_iota(jnp.int32, (V_ROWS - HEAD_DIM, tm), 0) == 0, 1.0, 0.0)
    heads_per_chunk = MXU_N // HEAD_DIM
    for c4 in range(d // MXU_N):
        kc =jnp.dot(h_scr[...], wk_ref[:, c4 * MXU_N:(c4 + 1) * MXU_N], preferred_element_type=F32)
        vc = jnp.dot(h_scr[...], wv_ref[:, c4 * MXU_N:(c4 + 1) * MXU_N], preferred_element_type=F32)
        tails = jnp.dot(d3, sel_ref[:, c4 * heads_per_chunk * LANES:(c4 + 1) * heads_per_chunk * LANES],
                        preferred_element_type=F32)
        for half in range(2):
            kn = _head_pair_norm(kc[:, half * LANES:(half + 1) * LANES], kg2)
            vt = vc[:, half * LANES:(half + 1) * LANES].T
            for sub, src in ((0, kn), (1, pltpu.roll(kn, HEAD_DIM, axis=1))):
                loc = 2 * half + sub
                hd = c4 * heads_per_chunk + loc
                k_out[hd] = _with_tail(src, tails[:, loc * LANES:(loc + 1) * LANES]).astype(BF16)
                vt_out[hd] = jnp.concatenate(
                    [vt[sub * HEAD_DIM:(sub + 1) * HEAD_DIM, :], ones_rows], axis=0).astype(BF16)


def _kv_project(x, kvmod, norm_g, w_k, w_v, w_f3, b_f3, kg2):
    b, s, d = x.shape
    tm = TOKEN_TILE
    return pl.pallas_call(
        _kv_body,
        grid=(b, s // tm),
        in_specs=[
            pl.BlockSpec((None, tm, d), lambda i, j: (i, j, 0)),
            pl.BlockSpec((None, 2, d), lambda i, j: (i, 0, 0)),
            _const_spec((1, d)),
            _const_spec((d, d)), _const_spec((d, d)), _const_spec((d, LANES)),
            _const_spec((1, LANES)), _const_spec((1, LANES)), _const_spec((LANES, N_HEADS * LANES)),
        ],
        out_specs=[
            pl.BlockSpec((None, N_HEADS, tm, LANES), lambda i, j: (i, 0, j, 0)),
            pl.BlockSpec((None, N_HEADS, None, V_ROWS, tm), lambda i, j: (i, 0, j, 0, 0)),
            pl.BlockSpec((None, tm, LANES), lambda i, j: (i, j, 0)),
        ],
        out_shape=[
            jax.ShapeDtypeStruct((b, N_HEADS, s, LANES), BF16),
            jax.ShapeDtypeStruct((b, N_HEADS, s // tm, V_ROWS, tm), BF16),
            jax.ShapeDtypeStruct((b, s, LANES), BF16),
        ],
        scratch_shapes=[pltpu.VMEM((tm, d), BF16), pltpu.VMEM((1, LANES), F32)],
        compiler_params=_params(2),
        name="kv_project",
    )(x, kvmod, norm_g, w_k, w_v, w_f3, b_f3, kg2, _decay_selectors(query_side=False))


def _q_body(x_ref, mod_ref, pre_ref, wq_ref, qg_ref, d_ref, sel_ref, q_out, h_scr):
    tm, d = x_ref.shape
    h_scr[...] = _prenorm(x_ref[...], pre_ref[...], mod_ref[0:1, :], mod_ref[1:2, :]).astype(BF16)
    d3 = d_ref[...]
    qg2 = qg_ref[...]
    heads_per_chunk = MXU_N // HEAD_DIM
    for c4 in range(d // MXU_N):
        qc = jnp.dot(h_scr[...], wq_ref[:, c4 * MXU_N:(c4 + 1) * MXU_N], preferred_element_type=F32)
        tails = jnp.dot(d3, sel_ref[:, c4 * heads_per_chunk * LANES:(c4 + 1) * heads_per_chunk * LANES],
                        preferred_element_type=F32)
        for half in range(2):
            qn = _head_pair_norm(qc[:, half * LANES:(half + 1) * LANES], qg2)
            for sub, src in ((0, qn), (1, pltpu.roll(qn, HEAD_DIM, axis=1))):
                loc = 2 * half + sub
                qa = _with_tail(src, tails[:, loc * LANES:(loc + 1) * LANES])
                q_out[c4 * heads_per_chunk + loc] = qa.astype(BF16)


def _q_project(x, mod, pre_g, w_q, qg2, d3):
    b, s, d = x.shape
    tm = TOKEN_TILE
    return pl.pallas_call(
        _q_body,
        grid=(b, s // tm),
        in_specs=[
            pl.BlockSpec((None, tm, d), lambda i, j: (i, j, 0)),
            pl.BlockSpec((None, 6, d), lambda i, j: (i, 0, 0)),
            _const_spec((1, d)), _const_spec((d, d)), _const_spec((1, LANES)),
            pl.BlockSpec((None, tm, LANES), lambda i, j: (i, j, 0)),
            _const_spec((LANES, N_HEADS * LANES)),
        ],
        out_specs=pl.BlockSpec((None, N_HEADS, tm, LANES), lambda i, j: (i, 0, j, 0)),
        out_shape=jax.ShapeDtypeStruct((b, N_HEADS, s, LANES), BF16),
        scratch_shapes=[pltpu.VMEM((tm, d), BF16)],
        compiler_params=_params(2),
        name="q_project",
    )(x, mod, pre_g, w_q, qg2, d3, _decay_selectors(query_side=True))


def _attn_body(q_ref, k_ref, vt_ref, o_ref, acc_scr, *s_scr):
    n_h = q_ref.shape[0]
    tk = vt_ref.shape[3]
    tq = 2 * tk
    nq = q_ref.shape[1] // tq
    heads = range(n_h)
    halves = (slice(0, tk), slice(tk, tq))
    kpos = lax.broadcasted_iota(jnp.int32, (tk, tk), 0)
    qpos = lax.broadcasted_iota(jnp.int32, (tk, tk), 1)
    tri = kpos <= qpos
    LO, HI = 0, 1

    def scores(qt, j, slot, g, valid):
        cms = []
        for h in heads:
            q = q_ref[h, pl.ds(pl.multiple_of(qt * tq + g * tk, tk), tk), :]
            k = k_ref[h, pl.ds(pl.multiple_of(j * tk, tk), tk), :]
            s = lax.dot_general(k, q, (((1,), (1,)), ((), ())), preferred_element_type=F32)
            if valid is not None:
                s = jnp.where(valid, s, -jnp.inf)
            s_scr[2 * h + slot][:, halves[g]] = s
            cms.append(jnp.max(s, axis=0, keepdims=True))
        return tuple(cms)

    def accumulate(j, slot, g, cmax, ml):
        out = []
        for h in heads:
            m_new = jnp.maximum(ml[h], cmax[h])
            alpha = jnp.exp2(ml[h] - m_new)
            p = jnp.exp2(s_scr[2 * h + slot][:, halves[g]] - m_new)
            pv = jnp.dot(vt_ref[h, j], p.astype(BF16), preferred_element_type=F32)
            acc_scr[h, :, halves[g]] = alpha * acc_scr[h, :, halves[g]] + pv
            out.append(m_new)
        return tuple(out)

    def pair(qi, p, carry, diag):
        (ml_lo, ml_hi), (c0_lo, c0_hi) = carry
        c1_lo = scores(qi, 2 * p + 1, 1, LO, None)
        ml_lo = accumulate(2 * p, 0, LO, c0_lo, ml_lo)
        c1_hi = scores(qi, 2 * p + 1, 1, HI, None)
        ml_hi = accumulate(2 * p, 0, HI, c0_hi, ml_hi)
        c0_lo = scores(qi, 2 * p + 2, 0, LO, tri if diag else None)
        ml_lo = accumulate(2 * p + 1, 1, LO, c1_lo, ml_lo)
        c0_hi = scores(qi, 2 * p + 2, 0, HI, None)
        ml_hi = accumulate(2 * p + 1, 1, HI, c1_hi, ml_hi)
        return (ml_lo, ml_hi), (c0_lo, c0_hi)

    def query_tile(qi, cm0):
        acc_scr[...] = jnp.zeros_like(acc_scr)
        fresh = tuple(jnp.full((1, tk), -jnp.inf, F32) for _ in heads)
        carry = lax.fori_loop(0, qi - 1, lambda p, c: pair(qi, p, c, False), ((fresh, fresh), cm0))
        carry = lax.cond(qi > 0, lambda c: pair(qi, qi - 1, c, True), lambda c: c, carry)
        (ml_lo, ml_hi), (c0_lo, c0_hi) = carry
        ml_lo = accumulate(2 * qi, 0, LO, c0_lo, ml_lo)
        c1_hi = scores(qi, 2 * qi + 1, 1, HI, tri)
        ml_hi = accumulate(2 * qi, 0, HI, c0_hi, ml_hi)
        nxt = jnp.minimum(qi + 1, nq - 1)
        n_lo = scores(nxt, 0, 0, LO, None)
        accumulate(2 * qi + 1, 1, HI, c1_hi, ml_hi)
        n_hi = scores(nxt, 0, 0, HI, None)
        for h in heads:
            for g in (LO, HI):
                o_ref[h, 2 * qi + g] = (acc_scr[h, 0:HEAD_DIM, halves[g]]
                                        / acc_scr[h, HEAD_DIM:HEAD_DIM + 1, halves[g]])
        return n_lo, n_hi

    lax.fori_loop(0, nq, query_tile, (scores(0, 0, 0, LO, tri), scores(0, 0, 0, HI, None)))


def _fox_attention(q_aug, k_aug, vt_aug):
    b, h, s, _ = k_aug.shape
    tk = TOKEN_TILE
    nk = s // tk
    hp = ATTN_HEADS_PER_STEP
    return pl.pallas_call(
        _attn_body,
        grid=(b, h // hp),
        in_specs=[
            pl.BlockSpec((None, hp, s, LANES), lambda i, j: (i, j, 0, 0)),
            pl.BlockSpec((None, hp, s, LANES), lambda i, j: (i, j, 0, 0)),
            pl.BlockSpec((None, hp, nk, V_ROWS, tk), lambda i, j: (i, j, 0, 0, 0)),
        ],
        out_specs=pl.BlockSpec((None, hp, nk, HEAD_DIM, tk), lambda i, j: (i, j, 0, 0, 0)),
        out_shape=jax.ShapeDtypeStruct((b, h, nk, HEAD_DIM, tk), F32),
        scratch_shapes=[pltpu.VMEM((hp, V_ROWS, ATTN_Q_TILE), F32)]
        + [pltpu.VMEM((tk, ATTN_Q_TILE), F32)] * (2 * hp),
        compiler_params=_params(2),
        name="fox_attention",
    )(q_aug, k_aug, vt_aug)


def _out_body(x_ref, mod_ref, pre_ref, post_ref, ot_ref, wg_ref, wo_ref, o_ref):
    tm, d = x_ref.shape
    x = x_ref[...]
    h = _prenorm(x, pre_ref[...], mod_ref[0:1, :], mod_ref[1:2, :]).astype(BF16)
    gate = _sigmoid(jnp.dot(h, wg_ref[...], preferred_element_type=F32))
    o = ot_ref[...].reshape(d, tm).T
    y = jnp.dot((o * gate).astype(BF16), wo_ref[...], preferred_element_type=F32)
    o_ref[...] = x + mod_ref[2:3, :] * _rms(y, post_ref[...])


def _fox_output(x, mod, pre_g, post_g, o_t, w_g, w_o):
    b, s, d = x.shape
    tm = TOKEN_TILE
    tok = pl.BlockSpec((None, tm, d), lambda i, j: (i, j, 0))
    return pl.pallas_call(
        _out_body,
        grid=(b, s // tm),
        in_specs=[
            tok,
            pl.BlockSpec((None, 6, d), lambda i, j: (i, 0, 0)),
            _const_spec((1, d)), _const_spec((1, d)),
            pl.BlockSpec((None, N_HEADS, None, HEAD_DIM, tm), lambda i, j: (i, 0, j, 0, 0)),
            _const_spec((d, d)), _const_spec((d, d)),
        ],
        out_specs=tok,
        out_shape=jax.ShapeDtypeStruct((b, s, d), F32),
        compiler_params=_params(2),
        name="fox_output",
    )(x, mod, pre_g, post_g, o_t, w_g, w_o)


def kernel(x, c, ada_w, ada_b, pre_mix_g, post_mix_g, pre_ffn_g, post_ffn_g, ffn_w_gu, ffn_w_down,
           a_w_in, a_b_in, a_ln_g, a_ln_b, a_w_s, a_b_s, a_w_out, kv_ada_w, kv_ada_b, kv_norm_g,
           kv_w, kv_b_f, k_norm_g, b_w_qg, b_q_norm_g, b_w_o):
    b, s, d = x.shape
    depth = ada_w.shape[0]
    n_a = a_w_in.shape[0]
    assert d == N_HEADS * HEAD_DIM and s % ATTN_Q_TILE == 0

    c8 = jnp.pad(c, ((0, 8 - b), (0, 0)))
    mods = _modulation(c8, ada_w, ada_b)[:, :b].reshape(depth, b, 6, d)
    kvmod = _modulation(c8, kv_ada_w[None], kv_ada_b[None])[0, :b].reshape(b, 2, d)

    row = lambda v: v.reshape(1, -1)
    q_scale = LOG2E * HEAD_DIM ** -0.5

    for layer in range(depth):
        mod = mods[layer]
        if layer < n_a:
            i = layer
            bs_full = jnp.repeat(a_b_s[i].T, CHUNK, axis=1)
            x = _gmlp_layer(x, mod, row(pre_mix_g[layer]), row(post_mix_g[layer]),
                            a_w_in[i].astype(BF16), row(a_b_in[i]), row(a_ln_g[i]), row(a_ln_b[i]),
                            a_w_s[i], bs_full, a_w_out[i].astype(BF16))
        else:
            j = layer - n_a
            qg2 = row(jnp.tile(b_q_norm_g[j] * q_scale, 2))
            q_aug = _q_project(x, mod, row(pre_mix_g[layer]), b_w_qg[j][:, :d].astype(BF16), qg2, d3)
            o_t = _fox_attention(q_aug, k_aug, vt_aug)
            x = _fox_output(x, mod, row(pre_mix_g[layer]), row(post_mix_g[layer]), o_t,
                            b_w_qg[j][:, d:].astype(BF16), b_w_o[j].astype(BF16))
        x = _ffn_layer(x, mod, row(pre_ffn_g[layer]), row(post_ffn_g[layer]),
                       ffn_w_gu[layer].astype(BF16), ffn_w_down[layer].astype(BF16))
        if layer == n_a - 1:
            w_f = kv_w[:, 2 * d:]
            pad = jnp.zeros((d, LANES - 3 * N_HEADS), F32)
            w_f3 = jnp.concatenate([w_f, w_f, w_f, pad], axis=1).astype(BF16)
            b_f3 = row(jnp.concatenate([kv_b_f, kv_b_f, kv_b_f, jnp.zeros((LANES - 3 * N_HEADS,), F32)]))
            k_aug, vt_aug, d3 = _kv_project(
                x, kvmod, row(kv_norm_g), kv_w[:, :d].astype(BF16), kv_w[:, d:2 * d].astype(BF16),
                w_f3, b_f3, row(jnp.tile(k_norm_g, 2)))
    return x
```

```python
import functools

import jax
import jax.numpy as jnp
from jax import lax
from jax.experimental import pallas as pl
from jax.experimental.pallas import tpu as pltpu

F32 = jnp.float32
BF16 = jnp.bfloat16

EPS = 1e-6
N_HEADS = 16
HEAD_DIM = 64
CHUNK = 128
GROUPS = 16
LOG2E = 1.4426950408889634
LANES = 128
MXU_N = 256
TOKEN_TILE = 512
ATTN_Q_TILE = 2 * TOKEN_TILE
ATTN_HEADS_PER_STEP = 2
BF16_SUBLANES = 16
V_ROWS = HEAD_DIM + BF16_SUBLANES
VMEM_LIMIT = 56 * 1024 * 1024


def _sigmoid(x):
    return 1.0 / (1.0 + jnp.exp(-x))


def _unit_rms(x):
    return x * lax.rsqrt(jnp.mean(x * x, axis=-1, keepdims=True) + EPS)


def _prenorm(x, g, shift, scale):
    return _unit_rms(x) * (g * (1.0 + scale)) + shift


def _gated_residual(x, y, g, gate):
    return x + _unit_rms(y) * (g * gate)


def _split3(x):
    hi = x.astype(BF16).astype(F32)
    r = x - hi
    mid = r.astype(BF16).astype(F32)
    lo = (r - mid).astype(BF16).astype(F32)
    return hi, mid, lo


def _const_spec(shape):
    n = len(shape)
    return pl.BlockSpec(shape, lambda *_: (0,) * n, pipeline_mode=pl.Buffered(1))


def _params(n_grid):
    return pltpu.CompilerParams(
        dimension_semantics=("arbitrary",) * n_grid, vmem_limit_bytes=VMEM_LIMIT)


def _mod_body(c_ref, w_ref, b_ref, o_ref):
    c = c_ref[...]
    ca = (c * _sigmoid(c)).astype(BF16)
    o_ref[...] = jnp.dot(ca, w_ref[...].astype(BF16), preferred_element_type=F32) + b_ref[...]


def _modulation(c8, w, b):
    n_l, d, n = w.shape
    nb = 1024
    return pl.pallas_call(
        _mod_body,
        grid=(n_l, n // nb),
        in_specs=[
            pl.BlockSpec((8, d), lambda l, j: (0, 0)),
            pl.BlockSpec((None, d, nb), lambda l, j: (l, 0, j)),
            pl.BlockSpec((None, 1, nb), lambda l, j: (l, 0, j)),
        ],
        out_specs=pl.BlockSpec((None, 8, nb), lambda l, j: (l, 0, j)),
        out_shape=jax.ShapeDtypeStruct((n_l, 8, n), F32),
        compiler_params=_params(2),
        name="modulation",
    )(c8, w, b.reshape(n_l, 1, n))


def _gmlp_body(x_ref, mod_ref, pre_ref, post_ref, win_ref, bin_ref, lng_ref, lnb_ref, ws_ref,
               bs_ref, wout_ref, o_ref, h_scr, u_scr, v_scr, vn_scr, y_scr):
    tm, d = x_ref.shape
    gw = u_scr.shape[1]
    x = x_ref[...]
    h_scr[...] = _prenorm(x, pre_ref[...], mod_ref[0:1, :], mod_ref[1:2, :]).astype(BF16)

    k_gelu = 0.7978845608028654
    nc = 512
    for c0 in range(0, 2 * gw, nc):
        z = jnp.dot(h_scr[...], win_ref[:, c0:c0 + nc], preferred_element_type=F32)
        z = z + bin_ref[:, c0:c0 + nc]
        z = z * (0.5 * (1.0 + jnp.tanh(k_gelu * (z + 0.044715 * (z * z * z)))))
        if c0 < gw:
            u_scr[:, c0:c0 + nc] = z
        else:
            v_scr[:, c0 - gw:c0 - gw + nc] = z

    v = v_scr[...]
    mu = jnp.mean(v, axis=-1, keepdims=True)
    vc = v - mu
    rstd = lax.rsqrt(jnp.mean(vc * vc, axis=-1, keepdims=True) + EPS)
    vn_scr[...] = (vc * rstd * lng_ref[...] + lnb_ref[...]).astype(BF16)

    n_chunks = tm // CHUNK
    row = lax.broadcasted_iota(jnp.int32, (CHUNK, CHUNK), 0)
    col = lax.broadcasted_iota(jnp.int32, (CHUNK, CHUNK), 1)
    causal = col <= row
    for g in range(GROUPS):
        gs = slice(g * CHUNK, (g + 1) * CHUNK)
        ws = jnp.where(causal, ws_ref[g], 0.0).astype(BF16)
        rhs = jnp.concatenate(
            [vn_scr[c * CHUNK:(c + 1) * CHUNK, gs] for c in range(n_chunks)], axis=1)
        sp = jnp.dot(ws, rhs, preferred_element_type=F32)
        for c in range(n_chunks):
            cs = slice(c * CHUNK, (c + 1) * CHUNK)
            y_scr[cs, gs] = (u_scr[cs, gs] * (sp[:, cs] + bs_ref[:, gs])).astype(BF16)

    y = jnp.dot(y_scr[...], wout_ref[...], preferred_element_type=F32)
    o_ref[...] = _gated_residual(x, y, post_ref[...], mod_ref[2:3, :])


def _gmlp_layer(x, mod, pre_g, post_g, w_in, b_in, ln_g, ln_b, w_s, bs_full, w_out):
    b, s, d = x.shape
    gw = w_out.shape[0]
    tm = TOKEN_TILE
    tok = pl.BlockSpec((None, tm, d), lambda i, j: (i, j, 0))
    return pl.pallas_call(
        _gmlp_body,
        grid=(b, s // tm),
        in_specs=[
            tok,
            pl.BlockSpec((None, 6, d), lambda i, j: (i, 0, 0)),
            _const_spec((1, d)), _const_spec((1, d)),
            _const_spec((d, 2 * gw)), _const_spec((1, 2 * gw)),
            _const_spec((1, gw)), _const_spec((1, gw)),
            _const_spec((GROUPS, CHUNK, CHUNK)), _const_spec((CHUNK, gw)),
            _const_spec((gw, d)),
        ],
        out_specs=tok,
        out_shape=jax.ShapeDtypeStruct((b, s, d), F32),
        scratch_shapes=[
            pltpu.VMEM((tm, d), BF16), pltpu.VMEM((tm, gw), F32), pltpu.VMEM((tm, gw), F32),
            pltpu.VMEM((tm, gw), BF16), pltpu.VMEM((tm, gw), BF16),
        ],
        compiler_params=_params(2),
        name="gmlp_mixer",
    )(x, mod, pre_g, post_g, w_in, b_in, ln_g, ln_b, w_s, bs_full, w_out)


def _ffn_body(x_ref, mod_ref, pre_ref, post_ref, wgu_ref, wd_ref, o_ref, h_scr, a_scr):
    f = a_scr.shape[1]
    x = x_ref[...]
    h_scr[...] = _prenorm(x, pre_ref[...], mod_ref[3:4, :], mod_ref[4:5, :]).astype(BF16)
    for c0 in range(0, f, MXU_N):
        g = jnp.dot(h_scr[...], wgu_ref[:, c0:c0 + MXU_N], preferred_element_type=F32)
        u = jnp.dot(h_scr[...], wgu_ref[:, f + c0:f + c0 + MXU_N], preferred_element_type=F32)
        a_scr[:, c0:c0 + MXU_N] = (g * _sigmoid(g) * u).astype(BF16)
    y = jnp.dot(a_scr[...], wd_ref[...], preferred_element_type=F32)
    o_ref[...] = _gated_residual(x, y, post_ref[...], mod_ref[5:6, :])


def _ffn_layer(x, mod, pre_g, post_g, w_gu, w_down):
    b, s, d = x.shape
    f = w_down.shape[0]
    tm = TOKEN_TILE
    tok = pl.BlockSpec((None, tm, d), lambda i, j: (i, j, 0))
    return pl.pallas_call(
        _ffn_body,
        grid=(b, s // tm),
        in_specs=[
            tok,
            pl.BlockSpec((None, 6, d), lambda i, j: (i, 0, 0)),
            _const_spec((1, d)), _const_spec((1, d)),
            _const_spec((d, 2 * f)), _const_spec((f, d)),
        ],
        out_specs=tok,
        out_shape=jax.ShapeDtypeStruct((b, s, d), F32),
        scratch_shapes=[pltpu.VMEM((tm, d), BF16), pltpu.VMEM((tm, f), BF16)],
        compiler_params=_params(2),
        name="swiglu_ffn",
    )(x, mod, pre_g, post_g, w_gu, w_down)


def _head_pair_norm(t, gain2):
    lane = lax.broadcasted_iota(jnp.int32, t.shape, 1)
    sq = t * t
    first = lane < HEAD_DIM
    s0 = jnp.sum(jnp.where(first, sq, 0.0), axis=-1, keepdims=True)
    s1 = jnp.sum(jnp.where(first, 0.0, sq), axis=-1, keepdims=True)
    rs = jnp.where(first, lax.rsqrt(s0 / HEAD_DIM + EPS), lax.rsqrt(s1 / HEAD_DIM + EPS))
    return t * rs * gain2


def _decay_selectors(query_side):
    row = lax.broadcasted_iota(jnp.int32, (LANES, N_HEADS * LANES), 0)
    col = lax.broadcasted_iota(jnp.int32, (LANES, N_HEADS * LANES), 1)
    head, off = col // LANES, col % LANES - HEAD_DIM
    first, second = (off >= 0) & (off < 3), (off >= 3) & (off < 6)
    piece_cols, piece_idx = (first, off) if query_side else (second, off - 3)
    one_cols = second if query_side else first
    is_piece = piece_cols & (row == piece_idx * N_HEADS + head)
    is_one = one_cols & (row == 3 * N_HEADS)
    sel = jnp.where(is_piece, 1.0 if query_side else -1.0, jnp.where(is_one, 1.0, 0.0))
    return sel.astype(BF16)


def _with_tail(t, tail):
    lane = lax.broadcasted_iota(jnp.int32, t.shape, 1)
    return jnp.where(lane < HEAD_DIM, t, tail)


def _kv_body(x_ref, mod_ref, g_ref, wk_ref, wv_ref, wf_ref, bf_ref, kg_ref, sel_ref,
             k_out, vt_out, d_out, h_scr, carry_scr):
    tm, d = x_ref.shape

    @pl.when(pl.program_id(1) == 0)
    def _():
        carry_scr[...] = jnp.zeros_like(carry_scr)

    h_scr[...] = _prenorm(x_ref[...], g_ref[...], mod_ref[0:1, :], mod_ref[1:2, :]).astype(BF16)

    lane = lax.broadcasted_iota(jnp.int32, (tm, LANES), 1)
    fl = jnp.dot(h_scr[...], wf_ref[...], preferred_element_type=F32) + bf_ref[...]
    z = -fl
    ls = -(jnp.maximum(z, 0.0) + jnp.log1p(jnp.exp(-jnp.abs(z))))
    hi, mid, lo = _split3(ls)
    pieces = jnp.where(lane < N_HEADS, hi, jnp.where(lane < 2 * N_HEADS, mid, jnp.where(
        lane < 3 * N_HEADS, lo, 0.0))).astype(BF16)
    r = lax.broadcasted_iota(jnp.int32, (tm, tm), 0)
    c = lax.broadcasted_iota(jnp.int32, (tm, tm), 1)
    tri = jnp.where(c <= r, 1.0, 0.0).astype(BF16)
    cs = jnp.dot(tri, pieces, preferred_element_type=F32)
    tot = cs + pltpu.roll(cs, LANES - N_HEADS, axis=1) + pltpu.roll(cs, LANES - 2 * N_HEADS, axis=1)
    dc = jnp.where(lane < N_HEADS, tot + carry_scr[...], 0.0)
    carry_scr[...] = dc[tm - 1:tm, :]
    hi, mid, lo = _split3(dc * LOG2E)
    d3 = (hi + pltpu.roll(mid, N_HEADS, axis=1) + pltpu.roll(lo, 2 * N_HEADS, axis=1)
          + jnp.where(lane == 3 * N_HEADS, 1.0, 0.0)).astype(BF16)
    d_out[...] = d3

    kg2 = kg_ref[...]
    ones_rows = jnp.where(
        lax.broadcasted_iota(jnp.int32, (V_ROWS - HEAD_DIM, tm), 0) == 0, 1.0, 0.0)
    heads_per_chunk = MXU_N // HEAD_DIM
    for c4 in range(d // MXU_N):
        kc =jnp.dot(h_scr[...], wk_ref[:, c4 * MXU_N:(c4 + 1) * MXU_N], preferred_element_type=F32)
        vc = jnp.dot(h_scr[...], wv_ref[:, c4 * MXU_N:(c4 + 1) * MXU_N], preferred_element_type=F32)
        tails = jnp.dot(d3, sel_ref[:, c4 * heads_per_chunk * LANES:(c4 + 1) * heads_per_chunk * LANES],
                        preferred_element_type=F32)
        for half in range(2):
            kn = _head_pair_norm(kc[:, half * LANES:(half + 1) * LANES], kg2)
            vt = vc[:, half * LANES:(half + 1) * LANES].T
            for sub, src in ((0, kn), (1, pltpu.roll(kn, HEAD_DIM, axis=1))):
                loc = 2 * half + sub
                hd = c4 * heads_per_chunk + loc
                k_out[hd] = _with_tail(src, tails[:, loc * LANES:(loc + 1) * LANES]).astype(BF16)
                vt_out[hd] = jnp.concatenate(
                    [vt[sub * HEAD_DIM:(sub + 1) * HEAD_DIM, :], ones_rows], axis=0).astype(BF16)


def _kv_project(x, kvmod, norm_g, w_k, w_v, w_f3, b_f3, kg2):
    b, s, d = x.shape
    tm = TOKEN_TILE
    return pl.pallas_call(
        _kv_body,
        grid=(b, s // tm),
        in_specs=[
            pl.BlockSpec((None, tm, d), lambda i, j: (i, j, 0)),
            pl.BlockSpec((None, 2, d), lambda i, j: (i, 0, 0)),
            _const_spec((1, d)),
            _const_spec((d, d)), _const_spec((d, d)), _const_spec((d, LANES)),
            _const_spec((1, LANES)), _const_spec((1, LANES)), _const_spec((LANES, N_HEADS * LANES)),
        ],
        out_specs=[
            pl.BlockSpec((None, N_HEADS, tm, LANES), lambda i, j: (i, 0, j, 0)),
            pl.BlockSpec((None, N_HEADS, None, V_ROWS, tm), lambda i, j: (i, 0, j, 0, 0)),
            pl.BlockSpec((None, tm, LANES), lambda i, j: (i, j, 0)),
        ],
        out_shape=[
            jax.ShapeDtypeStruct((b, N_HEADS, s, LANES), BF16),
            jax.ShapeDtypeStruct((b, N_HEADS, s // tm, V_ROWS, tm), BF16),
            jax.ShapeDtypeStruct((b, s, LANES), BF16),
        ],
        scratch_shapes=[pltpu.VMEM((tm, d), BF16), pltpu.VMEM((1, LANES), F32)],
        compiler_params=_params(2),
        name="kv_project",
    )(x, kvmod, norm_g, w_k, w_v, w_f3, b_f3, kg2, _decay_selectors(query_side=False))


def _q_body(x_ref, mod_ref, pre_ref, wq_ref, qg_ref, d_ref, sel_ref, q_out, h_scr):
    tm, d = x_ref.shape
    h_scr[...] = _prenorm(x_ref[...], pre_ref[...], mod_ref[0:1, :], mod_ref[1:2, :]).astype(BF16)
    d3 = d_ref[...]
    qg2 = qg_ref[...]
    heads_per_chunk = MXU_N // HEAD_DIM
    for c4 in range(d // MXU_N):
        qc = jnp.dot(h_scr[...], wq_ref[:, c4 * MXU_N:(c4 + 1) * MXU_N], preferred_element_type=F32)
        tails = jnp.dot(d3, sel_ref[:, c4 * heads_per_chunk * LANES:(c4 + 1) * heads_per_chunk * LANES],
                        preferred_element_type=F32)
        for half in range(2):
            qn = _head_pair_norm(qc[:, half * LANES:(half + 1) * LANES], qg2)
            for sub, src in ((0, qn), (1, pltpu.roll(qn, HEAD_DIM, axis=1))):
                loc = 2 * half + sub
                qa = _with_tail(src, tails[:, loc * LANES:(loc + 1) * LANES])
                q_out[c4 * heads_per_chunk + loc] = qa.astype(BF16)


def _q_project(x, mod, pre_g, w_q, qg2, d3):
    b, s, d = x.shape
    tm = TOKEN_TILE
    return pl.pallas_call(
        _q_body,
        grid=(b, s // tm),
        in_specs=[
            pl.BlockSpec((None, tm, d), lambda i, j: (i, j, 0)),
            pl.BlockSpec((None, 6, d), lambda i, j: (i, 0, 0)),
            _const_spec((1, d)), _const_spec((d, d)), _const_spec((1, LANES)),
            pl.BlockSpec((None, tm, LANES), lambda i, j: (i, j, 0)),
            _const_spec((LANES, N_HEADS * LANES)),
        ],
        out_specs=pl.BlockSpec((None, N_HEADS, tm, LANES), lambda i, j: (i, 0, j, 0)),
        out_shape=jax.ShapeDtypeStruct((b, N_HEADS, s, LANES), BF16),
        scratch_shapes=[pltpu.VMEM((tm, d), BF16)],
        compiler_params=_params(2),
        name="q_project",
    )(x, mod, pre_g, w_q, qg2, d3, _decay_selectors(query_side=True))


def _attn_body(q_ref, k_ref, vt_ref, o_ref, acc_scr, *s_scr):
    n_h = q_ref.shape[0]
    tk = vt_ref.shape[3]
    tq = 2 * tk
    nq = q_ref.shape[1] // tq
    heads = range(n_h)
    halves = (slice(0, tk), slice(tk, tq))
    kpos = lax.broadcasted_iota(jnp.int32, (tk, tk), 0)
    qpos = lax.broadcasted_iota(jnp.int32, (tk, tk), 1)
    tri = kpos <= qpos
    LO, HI = 0, 1

    def scores(qt, j, slot, g, valid):
        cms = []
        for h in heads:
            q = q_ref[h, pl.ds(pl.multiple_of(qt * tq + g * tk, tk), tk), :]
            k = k_ref[h, pl.ds(pl.multiple_of(j * tk, tk), tk), :]
            s = lax.dot_general(k, q, (((1,), (1,)), ((), ())), preferred_element_type=F32)
            if valid is not None:
                s = jnp.where(valid, s, -jnp.inf)
            s_scr[2 * h + slot][:, halves[g]] = s
            cms.append(jnp.max(s, axis=0, keepdims=True))
        return tuple(cms)

    def accumulate(j, slot, g, cmax, ml):
        out = []
        for h in heads:
            m_new = jnp.maximum(ml[h], cmax[h])
            alpha = jnp.exp2(ml[h] - m_new)
            p = jnp.exp2(s_scr[2 * h + slot][:, halves[g]] - m_new)
            pv = jnp.dot(vt_ref[h, j], p.astype(BF16), preferred_element_type=F32)
            acc_scr[h, :, halves[g]] = alpha * acc_scr[h, :, halves[g]] + pv
            out.append(m_new)
        return tuple(out)

    def pair(qi, p, carry, diag):
        (ml_lo, ml_hi), (c0_lo, c0_hi) = carry
        c1_lo = scores(qi, 2 * p + 1, 1, LO, None)
        ml_lo = accumulate(2 * p, 0, LO, c0_lo, ml_lo)
        c1_hi = scores(qi, 2 * p + 1, 1, HI, None)
        ml_hi = accumulate(2 * p, 0, HI, c0_hi, ml_hi)
        c0_lo = scores(qi, 2 * p + 2, 0, LO, tri if diag else None)
        ml_lo = accumulate(2 * p + 1, 1, LO, c1_lo, ml_lo)
        c0_hi = scores(qi, 2 * p + 2, 0, HI, None)
        ml_hi = accumulate(2 * p + 1, 1, HI, c1_hi, ml_hi)
        return (ml_lo, ml_hi), (c0_lo, c0_hi)

    def query_tile(qi, cm0):
        acc_scr[...] = jnp.zeros_like(acc_scr)
        fresh = tuple(jnp.full((1, tk), -jnp.inf, F32) for _ in heads)
        carry = lax.fori_loop(0, qi - 1, lambda p, c: pair(qi, p, c, False), ((fresh, fresh), cm0))

        def finish(carry, with_pair):
            if with_pair:
                carry = pair(qi, qi - 1, carry, True)
            (ml_lo, ml_hi), (c0_lo, c0_hi) = carry
            ml_lo = accumulate(2 * qi, 0, LO, c0_lo, ml_lo)
            c1_hi = scores(qi, 2 * qi + 1, 1, HI, tri)
            ml_hi = accumulate(2 * qi, 0, HI, c0_hi, ml_hi)
            nxt = jnp.minimum(qi + 1, nq - 1)
            n_lo = scores(nxt, 0, 0, LO, None)
            accumulate(2 * qi + 1, 1, HI, c1_hi, ml_hi)
            n_hi = scores(nxt, 0, 0, HI, None)
            for h in heads:
                for g in (LO, HI):
                    o_ref[h, 2 * qi + g] = (acc_scr[h, 0:HEAD_DIM, halves[g]]
                                            / acc_scr[h, HEAD_DIM:HEAD_DIM + 1, halves[g]])
            return n_lo, n_hi

        return lax.cond(qi > 0, lambda c: finish(c, True), lambda c: finish(c, False), carry)

    lax.fori_loop(0, nq, query_tile, (scores(0, 0, 0, LO, tri), scores(0, 0, 0, HI, None)))


def _fox_attention(q_aug, k_aug, vt_aug):
    b, h, s, _ = k_aug.shape
    tk = TOKEN_TILE
    nk = s // tk
    hp = ATTN_HEADS_PER_STEP
    return pl.pallas_call(
        _attn_body,
        grid=(b, h // hp),
        in_specs=[
            pl.BlockSpec((None, hp, s, LANES), lambda i, j: (i, j, 0, 0)),
            pl.BlockSpec((None, hp, s, LANES), lambda i, j: (i, j, 0, 0)),
            pl.BlockSpec((None, hp, nk, V_ROWS, tk), lambda i, j: (i, j, 0, 0, 0)),
        ],
        out_specs=pl.BlockSpec((None, hp, nk, HEAD_DIM, tk), lambda i, j: (i, j, 0, 0, 0)),
        out_shape=jax.ShapeDtypeStruct((b, h, nk, HEAD_DIM, tk), F32),
        scratch_shapes=[pltpu.VMEM((hp, V_ROWS, ATTN_Q_TILE), F32)]
        + [pltpu.VMEM((tk, ATTN_Q_TILE), F32)] * (2 * hp),
        compiler_params=_params(2),
        name="fox_attention",
    )(q_aug, k_aug, vt_aug)


def _out_body(x_ref, mod_ref, pre_ref, post_ref, ot_ref, wg_ref, wo_ref, o_ref):
    tm, d = x_ref.shape
    x = x_ref[...]
    h = _prenorm(x, pre_ref[...], mod_ref[0:1, :], mod_ref[1:2, :]).astype(BF16)
    gate = _sigmoid(jnp.dot(h, wg_ref[...], preferred_element_type=F32))
    o = ot_ref[...].reshape(d, tm).T
    y = jnp.dot((o * gate).astype(BF16), wo_ref[...], preferred_element_type=F32)
    o_ref[...] = _gated_residual(x, y, post_ref[...], mod_ref[2:3, :])


def _fox_output(x, mod, pre_g, post_g, o_t, w_g, w_o):
    b, s, d = x.shape
    tm = TOKEN_TILE
    tok = pl.BlockSpec((None, tm, d), lambda i, j: (i, j, 0))
    return pl.pallas_call(
        _out_body,
        grid=(b, s // tm),
        in_specs=[
            tok,
            pl.BlockSpec((None, 6, d), lambda i, j: (i, 0, 0)),
            _const_spec((1, d)), _const_spec((1, d)),
            pl.BlockSpec((None, N_HEADS, None, HEAD_DIM, tm), lambda i, j: (i, 0, j, 0, 0)),
            _const_spec((d, d)), _const_spec((d, d)),
        ],
        out_specs=tok,
        out_shape=jax.ShapeDtypeStruct((b, s, d), F32),
        compiler_params=_params(2),
        name="fox_output",
    )(x, mod, pre_g, post_g, o_t, w_g, w_o)


def kernel(x, c, ada_w, ada_b, pre_mix_g, post_mix_g, pre_ffn_g, post_ffn_g, ffn_w_gu, ffn_w_down,
           a_w_in, a_b_in, a_ln_g, a_ln_b, a_w_s, a_b_s, a_w_out, kv_ada_w, kv_ada_b, kv_norm_g,
           kv_w, kv_b_f, k_norm_g, b_w_qg, b_q_norm_g, b_w_o):
    b, s, d = x.shape
    depth = ada_w.shape[0]
    n_a = a_w_in.shape[0]
    assert d == N_HEADS * HEAD_DIM and s % ATTN_Q_TILE == 0

    c8 = jnp.pad(c, ((0, 8 - b), (0, 0)))
    mods = _modulation(c8, ada_w, ada_b)[:, :b].reshape(depth, b, 6, d)
    kvmod = _modulation(c8, kv_ada_w[None], kv_ada_b[None])[0, :b].reshape(b, 2, d)

    row = lambda v: v.reshape(1, -1)
    q_scale = LOG2E * HEAD_DIM ** -0.5

    for layer in range(depth):
        mod = mods[layer]
        if layer < n_a:
            i = layer
            bs_full = jnp.repeat(a_b_s[i].T, CHUNK, axis=1)
            x = _gmlp_layer(x, mod, row(pre_mix_g[layer]), row(post_mix_g[layer]),
                            a_w_in[i].astype(BF16), row(a_b_in[i]), row(a_ln_g[i]), row(a_ln_b[i]),
                            a_w_s[i], bs_full, a_w_out[i].astype(BF16))
        else:
            j = layer - n_a
            qg2 = row(jnp.tile(b_q_norm_g[j] * q_scale, 2))
            q_aug = _q_project(x, mod, row(pre_mix_g[layer]), b_w_qg[j][:, :d].astype(BF16), qg2, d3)
            o_t = _fox_attention(q_aug, k_aug, vt_aug)
            x = _fox_output(x, mod, row(pre_mix_g[layer]), row(post_mix_g[layer]), o_t,
                            b_w_qg[j][:, d:].astype(BF16), b_w_o[j].astype(BF16))
        x = _ffn_layer(x, mod, row(pre_ffn_g[layer]), row(post_ffn_g[layer]),
                       ffn_w_gu[layer].astype(BF16), ffn_w_down[layer].astype(BF16))
        if layer == n_a - 1:
            w_f = kv_w[:, 2 * d:]
            pad = jnp.zeros((d, LANES - 3 * N_HEADS), F32)
            w_f3 = jnp.concatenate([w_f, w_f, w_f, pad], axis=1).astype(BF16)
            b_f3 = row(jnp.concatenate([kv_b_f, kv_b_f, kv_b_f, jnp.zeros((LANES - 3 * N_HEADS,), F32)]))
            k_aug, vt_aug, d3 = _kv_project(
                x, kvmod, row(kv_norm_g), kv_w[:, :d].astype(BF16), kv_w[:, d:2 * d].astype(BF16),
                w_f3, b_f3, row(jnp.tile(k_norm_g, 2)))
    return x
```

```python
import functools

import jax
import jax.numpy as jnp
from jax import lax
from jax.experimental import pallas as pl
from jax.experimental.pallas import tpu as pltpu

F32 = jnp.float32
BF16 = jnp.bfloat16

EPS = 1e-6
N_HEADS = 16
HEAD_DIM = 64
CHUNK = 128
GROUPS = 16
LOG2E = 1.4426950408889634
LANES = 128
MXU_N = 256
TOKEN_TILE = 512
ATTN_Q_TILE = 2 * TOKEN_TILE
ATTN_HEADS_PER_STEP = 2
BF16_SUBLANES = 16
V_ROWS = HEAD_DIM + BF16_SUBLANES
VMEM_LIMIT = 56 * 1024 * 1024


def _sigmoid(x):
    return 1.0 / (1.0 + jnp.exp(-x))


def _unit_rms(x):
    return x * lax.rsqrt(jnp.mean(x * x, axis=-1, keepdims=True) + EPS)


def _prenorm(x, g, shift, scale):
    return _unit_rms(x) * (g * (1.0 + scale)) + shift


def _gated_residual(x, y, g, gate):
    return x + _unit_rms(y) * (g * gate)


def _split3(x):
    hi = x.astype(BF16).astype(F32)
    r = x - hi
    mid = r.astype(BF16).astype(F32)
    lo = (r - mid).astype(BF16).astype(F32)
    return hi, mid, lo


def _const_spec(shape):
    n = len(shape)
    return pl.BlockSpec(shape, lambda *_: (0,) * n, pipeline_mode=pl.Buffered(1))


def _lane_padded(w):
    return jnp.pad(w.astype(BF16), ((0, 0), (0, LANES)))


def _params(n_grid):
    return pltpu.CompilerParams(
        dimension_semantics=("arbitrary",) * n_grid, vmem_limit_bytes=VMEM_LIMIT)


def _mod_body(c_ref, w_ref, b_ref, o_ref):
    c = c_ref[...]
    ca = (c * _sigmoid(c)).astype(BF16)
    o_ref[...] = jnp.dot(ca, w_ref[...].astype(BF16), preferred_element_type=F32) + b_ref[...]


def _modulation(c8, w, b):
    n_l, d, n = w.shape
    nb = 1024
    return pl.pallas_call(
        _mod_body,
        grid=(n_l, n // nb),
        in_specs=[
            pl.BlockSpec((8, d), lambda l, j: (0, 0)),
            pl.BlockSpec((None, d, nb), lambda l, j: (l, 0, j)),
            pl.BlockSpec((None, 1, nb), lambda l, j: (l, 0, j)),
        ],
        out_specs=pl.BlockSpec((None, 8, nb), lambda l, j: (l, 0, j)),
        out_shape=jax.ShapeDtypeStruct((n_l, 8, n), F32),
        compiler_params=_params(2),
        name="modulation",
    )(c8, w, b.reshape(n_l, 1, n))


def _gmlp_body(x_ref, mod_ref, pre_ref, post_ref, win_ref, bin_ref, lng_ref, lnb_ref, ws_ref,
               bs_ref, wout_ref, o_ref, h_scr, u_scr, v_scr, vn_scr, y_scr):
    tm, d = x_ref.shape
    gw = u_scr.shape[1]
    x = x_ref[...]
    h_scr[...] = _prenorm(x, pre_ref[...], mod_ref[0:1, :], mod_ref[1:2, :]).astype(BF16)

    k_gelu = 0.7978845608028654
    nc = 512
    for c0 in range(0, 2 * gw, nc):
        z = jnp.dot(h_scr[...], win_ref[:, c0:c0 + nc], preferred_element_type=F32)
        z = z + bin_ref[:, c0:c0 + nc]
        z = z * (0.5 * (1.0 + jnp.tanh(k_gelu * (z + 0.044715 * (z * z * z)))))
        if c0 < gw:
            u_scr[:, c0:c0 + nc] = z
        else:
            v_scr[:, c0 - gw:c0 - gw + nc] = z

    v = v_scr[...]
    mu = jnp.mean(v, axis=-1, keepdims=True)
    vc = v - mu
    rstd = lax.rsqrt(jnp.mean(vc * vc, axis=-1, keepdims=True) + EPS)
    vn_scr[...] = (vc * rstd * lng_ref[...] + lnb_ref[...]).astype(BF16)

    n_chunks = tm // CHUNK
    row = lax.broadcasted_iota(jnp.int32, (CHUNK, CHUNK), 0)
    col = lax.broadcasted_iota(jnp.int32, (CHUNK, CHUNK), 1)
    causal = col <= row
    for g in range(GROUPS):
        gs = slice(g * CHUNK, (g + 1) * CHUNK)
        ws = jnp.where(causal, ws_ref[g], 0.0).astype(BF16)
        rhs = jnp.concatenate(
            [vn_scr[c * CHUNK:(c + 1) * CHUNK, gs] for c in range(n_chunks)], axis=1)
        sp = jnp.dot(ws, rhs, preferred_element_type=F32)
        for c in range(n_chunks):
            cs = slice(c * CHUNK, (c + 1) * CHUNK)
            y_scr[cs, gs] = (u_scr[cs, gs] * (sp[:, cs] + bs_ref[:, gs])).astype(BF16)

    y = jnp.dot(y_scr[...], wout_ref[:, 0:d], preferred_element_type=F32)
    o_ref[...] = _gated_residual(x, y, post_ref[...], mod_ref[2:3, :])


def _gmlp_layer(x, mod, pre_g, post_g, w_in, b_in, ln_g, ln_b, w_s, bs_full, w_out):
    b, s, d = x.shape
    gw = w_out.shape[0]
    tm = TOKEN_TILE
    tok = pl.BlockSpec((None, tm, d), lambda i, j: (i, j, 0))
    return pl.pallas_call(
        _gmlp_body,
        grid=(b, s // tm),
        in_specs=[
            tok,
            pl.BlockSpec((None, 6, d), lambda i, j: (i, 0, 0)),
            _const_spec((1, d)), _const_spec((1, d)),
            _const_spec(w_in.shape), _const_spec((1, 2 * gw)),
            _const_spec((1, gw)), _const_spec((1, gw)),
            _const_spec((GROUPS, CHUNK, CHUNK)), _const_spec((CHUNK, gw)),
            _const_spec(w_out.shape),
        ],
        out_specs=tok,
        out_shape=jax.ShapeDtypeStruct((b, s, d), F32),
        scratch_shapes=[
            pltpu.VMEM((tm, d), BF16), pltpu.VMEM((tm, gw), F32), pltpu.VMEM((tm, gw), F32),
            pltpu.VMEM((tm, gw), BF16), pltpu.VMEM((tm, gw), BF16),
        ],
        compiler_params=_params(2),
        name="gmlp_mixer",
    )(x, mod, pre_g, post_g, w_in, b_in, ln_g, ln_b, w_s, bs_full, w_out)


def _ffn_body(x_ref, mod_ref, pre_ref, post_ref, wgu_ref, wd_ref, o_ref, h_scr, a_scr):
    f = a_scr.shape[1]
    x = x_ref[...]
    h_scr[...] = _prenorm(x, pre_ref[...], mod_ref[3:4, :], mod_ref[4:5, :]).astype(BF16)
    for c0 in range(0, f, MXU_N):
        g = jnp.dot(h_scr[...], wgu_ref[:, c0:c0 + MXU_N], preferred_element_type=F32)
        u = jnp.dot(h_scr[...], wgu_ref[:, f + c0:f + c0 + MXU_N], preferred_element_type=F32)
        a_scr[:, c0:c0 + MXU_N] = (g * _sigmoid(g) * u).astype(BF16)
    y = jnp.dot(a_scr[...], wd_ref[...], preferred_element_type=F32)
    o_ref[...] = _gated_residual(x, y, post_ref[...], mod_ref[5:6, :])


def _ffn_layer(x, mod, pre_g, post_g, w_gu, w_down):
    b, s, d = x.shape
    f = w_down.shape[0]
    tm = TOKEN_TILE
    tok = pl.BlockSpec((None, tm, d), lambda i, j: (i, j, 0))
    return pl.pallas_call(
        _ffn_body,
        grid=(b, s // tm),
        in_specs=[
            tok,
            pl.BlockSpec((None, 6, d), lambda i, j: (i, 0, 0)),
            _const_spec((1, d)), _const_spec((1, d)),
            _const_spec((d, 2 * f)), _const_spec((f, d)),
        ],
        out_specs=tok,
        out_shape=jax.ShapeDtypeStruct((b, s, d), F32),
        scratch_shapes=[pltpu.VMEM((tm, d), BF16), pltpu.VMEM((tm, f), BF16)],
        compiler_params=_params(2),
        name="swiglu_ffn",
    )(x, mod, pre_g, post_g, w_gu, w_down)


def _head_pair_norm(t, gain2):
    lane = lax.broadcasted_iota(jnp.int32, t.shape, 1)
    sq = t * t
    first = lane < HEAD_DIM
    s0 = jnp.sum(jnp.where(first, sq, 0.0), axis=-1, keepdims=True)
    s1 = jnp.sum(jnp.where(first, 0.0, sq), axis=-1, keepdims=True)
    rs = jnp.where(first, lax.rsqrt(s0 / HEAD_DIM + EPS), lax.rsqrt(s1 / HEAD_DIM + EPS))
    return t * rs * gain2


def _decay_selectors(query_side):
    row = lax.broadcasted_iota(jnp.int32, (LANES, N_HEADS * LANES), 0)
    col = lax.broadcasted_iota(jnp.int32, (LANES, N_HEADS * LANES), 1)
    head, off = col // LANES, col % LANES - HEAD_DIM
    first, second = (off >= 0) & (off < 3), (off >= 3) & (off < 6)
    piece_cols, piece_idx = (first, off) if query_side else (second, off - 3)
    one_cols = second if query_side else first
    is_piece = piece_cols & (row == piece_idx * N_HEADS + head)
    is_one = one_cols & (row == 3 * N_HEADS)
    sel = jnp.where(is_piece, 1.0 if query_side else -1.0, jnp.where(is_one, 1.0, 0.0))
    return sel.astype(BF16)


def _with_tail(t, tail):
    lane = lax.broadcasted_iota(jnp.int32, t.shape, 1)
    return jnp.where(lane < HEAD_DIM, t, tail)


def _kv_body(x_ref, mod_ref, g_ref, wk_ref, wv_ref, wf_ref, bf_ref, kg_ref, sel_ref,
             k_out, vt_out, d_out, h_scr, carry_scr):
    tm, d = x_ref.shape

    @pl.when(pl.program_id(1) == 0)
    def _():
        carry_scr[...] = jnp.zeros_like(carry_scr)

    h_scr[...] = _prenorm(x_ref[...], g_ref[...], mod_ref[0:1, :], mod_ref[1:2, :]).astype(BF16)

    lane = lax.broadcasted_iota(jnp.int32, (tm, LANES), 1)
    fl = jnp.dot(h_scr[...], wf_ref[...], preferred_element_type=F32) + bf_ref[...]
    z = -fl
    ls = -(jnp.maximum(z, 0.0) + jnp.log1p(jnp.exp(-jnp.abs(z))))
    hi, mid, lo = _split3(ls)
    pieces = jnp.where(lane < N_HEADS, hi, jnp.where(lane < 2 * N_HEADS, mid, jnp.where(
        lane < 3 * N_HEADS, lo, 0.0))).astype(BF16)
    r = lax.broadcasted_iota(jnp.int32, (tm, tm), 0)
    c = lax.broadcasted_iota(jnp.int32, (tm, tm), 1)
    tri = jnp.where(c <= r, 1.0, 0.0).astype(BF16)
    cs = jnp.dot(tri, pieces, preferred_element_type=F32)
    tot = cs + pltpu.roll(cs, LANES - N_HEADS, axis=1) + pltpu.roll(cs, LANES - 2 * N_HEADS, axis=1)
    dc = jnp.where(lane < N_HEADS, tot + carry_scr[...], 0.0)
    carry_scr[...] = dc[tm - 1:tm, :]
    hi, mid, lo = _split3(dc * LOG2E)
    d3 = (hi + pltpu.roll(mid, N_HEADS, axis=1) + pltpu.roll(lo, 2 * N_HEADS, axis=1)
          + jnp.where(lane == 3 * N_HEADS, 1.0, 0.0)).astype(BF16)
    d_out[...] = d3

    kg2 = kg_ref[...]
    ones_rows = jnp.where(
        lax.broadcasted_iota(jnp.int32, (V_ROWS - HEAD_DIM, tm), 0) == 0, 1.0, 0.0)
    heads_per_chunk = MXU_N // HEAD_DIM
    for c4 in range(d // MXU_N):
        kc =jnp.dot(h_scr[...], wk_ref[:, c4 * MXU_N:(c4 + 1) * MXU_N], preferred_element_type=F32)
        vc = jnp.dot(h_scr[...], wv_ref[:, c4 * MXU_N:(c4 + 1) * MXU_N], preferred_element_type=F32)
        tails = jnp.dot(d3, sel_ref[:, c4 * heads_per_chunk * LANES:(c4 + 1) * heads_per_chunk * LANES],
                        preferred_element_type=F32)
        for half in range(2):
            kn = _head_pair_norm(kc[:, half * LANES:(half + 1) * LANES], kg2)
            vt = vc[:, half * LANES:(half + 1) * LANES].T
            for sub, src in ((0, kn), (1, pltpu.roll(kn, HEAD_DIM, axis=1))):
                loc = 2 * half + sub
                hd = c4 * heads_per_chunk + loc
                k_out[hd] = _with_tail(src, tails[:, loc * LANES:(loc + 1) * LANES]).astype(BF16)
                vt_out[hd] = jnp.concatenate(
                    [vt[sub * HEAD_DIM:(sub + 1) * HEAD_DIM, :], ones_rows], axis=0).astype(BF16)


def _kv_project(x, kvmod, norm_g, w_k, w_v, w_f3, b_f3, kg2):
    b, s, d = x.shape
    tm = TOKEN_TILE
    return pl.pallas_call(
        _kv_body,
        grid=(b, s // tm),
        in_specs=[
            pl.BlockSpec((None, tm, d), lambda i, j: (i, j, 0)),
            pl.BlockSpec((None, 2, d), lambda i, j: (i, 0, 0)),
            _const_spec((1, d)),
            _const_spec((d, d)), _const_spec((d, d)), _const_spec((d, LANES)),
            _const_spec((1, LANES)), _const_spec((1, LANES)), _const_spec((LANES, N_HEADS * LANES)),
        ],
        out_specs=[
            pl.BlockSpec((None, N_HEADS, tm, LANES), lambda i, j: (i, 0, j, 0)),
            pl.BlockSpec((None, N_HEADS, None, V_ROWS, tm), lambda i, j: (i, 0, j, 0, 0)),
            pl.BlockSpec((None, tm, LANES), lambda i, j: (i, j, 0)),
        ],
        out_shape=[
            jax.ShapeDtypeStruct((b, N_HEADS, s, LANES), BF16),
            jax.ShapeDtypeStruct((b, N_HEADS, s // tm, V_ROWS, tm), BF16),
            jax.ShapeDtypeStruct((b, s, LANES), BF16),
        ],
        scratch_shapes=[pltpu.VMEM((tm, d), BF16), pltpu.VMEM((1, LANES), F32)],
        compiler_params=_params(2),
        name="kv_project",
    )(x, kvmod, norm_g, w_k, w_v, w_f3, b_f3, kg2, _decay_selectors(query_side=False))


def _q_body(x_ref, mod_ref, pre_ref, wq_ref, qg_ref, d_ref, sel_ref, q_out, h_scr):
    tm, d = x_ref.shape
    h_scr[...] = _prenorm(x_ref[...], pre_ref[...], mod_ref[0:1, :], mod_ref[1:2, :]).astype(BF16)
    d3 = d_ref[...]
    qg2 = qg_ref[...]
    heads_per_chunk = MXU_N // HEAD_DIM
    for c4 in range(d // MXU_N):
        qc = jnp.dot(h_scr[...], wq_ref[:, c4 * MXU_N:(c4 + 1) * MXU_N], preferred_element_type=F32)
        tails = jnp.dot(d3, sel_ref[:, c4 * heads_per_chunk * LANES:(c4 + 1) * heads_per_chunk * LANES],
                        preferred_element_type=F32)
        for half in range(2):
            qn = _head_pair_norm(qc[:, half * LANES:(half + 1) * LANES], qg2)
            for sub, src in ((0, qn), (1, pltpu.roll(qn, HEAD_DIM, axis=1))):
                loc = 2 * half + sub
                qa = _with_tail(src, tails[:, loc * LANES:(loc + 1) * LANES])
                q_out[c4 * heads_per_chunk + loc] = qa.astype(BF16)


def _q_project(x, mod, pre_g, w_q, qg2, d3):
    b, s, d = x.shape
    tm = TOKEN_TILE
    return pl.pallas_call(
        _q_body,
        grid=(b, s // tm),
        in_specs=[
            pl.BlockSpec((None, tm, d), lambda i, j: (i, j, 0)),
            pl.BlockSpec((None, 6, d), lambda i, j: (i, 0, 0)),
            _const_spec((1, d)), _const_spec((d, d)), _const_spec((1, LANES)),
            pl.BlockSpec((None, tm, LANES), lambda i, j: (i, j, 0)),
            _const_spec((LANES, N_HEADS * LANES)),
        ],
        out_specs=pl.BlockSpec((None, N_HEADS, tm, LANES), lambda i, j: (i, 0, j, 0)),
        out_shape=jax.ShapeDtypeStruct((b, N_HEADS, s, LANES), BF16),
        scratch_shapes=[pltpu.VMEM((tm, d), BF16)],
        compiler_params=_params(2),
        name="q_project",
    )(x, mod, pre_g, w_q, qg2, d3, _decay_selectors(query_side=True))


def _attn_body(q_ref, k_ref, vt_ref, o_ref, acc_scr, *s_scr):
    n_h = q_ref.shape[0]
    tk = vt_ref.shape[3]
    tq = 2 * tk
    nq = q_ref.shape[1] // tq
    heads = range(n_h)
    halves = (slice(0, tk), slice(tk, tq))
    kpos = lax.broadcasted_iota(jnp.int32, (tk, tk), 0)
    qpos = lax.broadcasted_iota(jnp.int32, (tk, tk), 1)
    tri = kpos <= qpos
    LO, HI = 0, 1

    def scores(qt, j, slot, g, valid):
        cms = []
        for h in heads:
            q = q_ref[h, pl.ds(pl.multiple_of(qt * tq + g * tk, tk), tk), :]
            k = k_ref[h, pl.ds(pl.multiple_of(j * tk, tk), tk), :]
            s = lax.dot_general(k, q, (((1,), (1,)), ((), ())), preferred_element_type=F32)
            if valid is not None:
                s = jnp.where(valid, s, -jnp.inf)
            s_scr[2 * h + slot][:, halves[g]] = s
            cms.append(jnp.max(s, axis=0, keepdims=True))
        return tuple(cms)

    def accumulate(j, slot, g, cmax, ml):
        out = []
        for h in heads:
            m_new = jnp.maximum(ml[h], cmax[h])
            alpha = jnp.exp2(ml[h] - m_new)
            p = jnp.exp2(s_scr[2 * h + slot][:, halves[g]] - m_new)
            pv = jnp.dot(vt_ref[h, j], p.astype(BF16), preferred_element_type=F32)
            acc_scr[h, :, halves[g]] = alpha * acc_scr[h, :, halves[g]] + pv
            out.append(m_new)
        return tuple(out)

    def pair(qi, p, carry, diag):
        (ml_lo, ml_hi), (c0_lo, c0_hi) = carry
        c1_lo = scores(qi, 2 * p + 1, 1, LO, None)
        ml_lo = accumulate(2 * p, 0, LO, c0_lo, ml_lo)
        c1_hi = scores(qi, 2 * p + 1, 1, HI, None)
        ml_hi = accumulate(2 * p, 0, HI, c0_hi, ml_hi)
        c0_lo = scores(qi, 2 * p + 2, 0, LO, tri if diag else None)
        ml_lo = accumulate(2 * p + 1, 1, LO, c1_lo, ml_lo)
        c0_hi = scores(qi, 2 * p + 2, 0, HI, None)
        ml_hi = accumulate(2 * p + 1, 1, HI, c1_hi, ml_hi)
        return (ml_lo, ml_hi), (c0_lo, c0_hi)

    def query_tile(qi, cm0):
        acc_scr[...] = jnp.zeros_like(acc_scr)
        fresh = tuple(jnp.full((1, tk), -jnp.inf, F32) for _ in heads)
        carry = lax.fori_loop(0, qi - 1, lambda p, c: pair(qi, p, c, False), ((fresh, fresh), cm0))

        def finish(carry, with_pair):
            if with_pair:
                carry = pair(qi, qi - 1, carry, True)
            (ml_lo, ml_hi), (c0_lo, c0_hi) = carry
            ml_lo = accumulate(2 * qi, 0, LO, c0_lo, ml_lo)
            c1_hi = scores(qi, 2 * qi + 1, 1, HI, tri)
            ml_hi = accumulate(2 * qi, 0, HI, c0_hi, ml_hi)
            nxt = jnp.minimum(qi + 1, nq - 1)
            n_lo = scores(nxt, 0, 0, LO, None)
            accumulate(2 * qi + 1, 1, HI, c1_hi, ml_hi)
            n_hi = scores(nxt, 0, 0, HI, None)
            for g in (LO, HI):
                o_t = jnp.concatenate(
                    [acc_scr[h, 0:HEAD_DIM, halves[g]] / acc_scr[h, HEAD_DIM:HEAD_DIM + 1, halves[g]]
                     for h in heads], axis=0)
                o_ref[pl.ds(pl.multiple_of((2 * qi + g) * tk, tk), tk), :] = o_t.T
            return n_lo, n_hi

        return lax.cond(qi > 0, lambda c: finish(c, True), lambda c: finish(c, False), carry)

    lax.fori_loop(0, nq, query_tile, (scores(0, 0, 0, LO, tri), scores(0, 0, 0, HI, None)))


def _fox_attention(q_aug, k_aug, vt_aug):
    b, h, s, _ = k_aug.shape
    tk = TOKEN_TILE
    nk = s // tk
    hp = ATTN_HEADS_PER_STEP
    return pl.pallas_call(
        _attn_body,
        grid=(b, h // hp),
        in_specs=[
            pl.BlockSpec((None, hp, s, LANES), lambda i, j: (i, j, 0, 0)),
            pl.BlockSpec((None, hp, s, LANES), lambda i, j: (i, j, 0, 0)),
            pl.BlockSpec((None, hp, nk, V_ROWS, tk), lambda i, j: (i, j, 0, 0, 0)),
        ],
        out_specs=pl.BlockSpec((None, s, hp * HEAD_DIM), lambda i, j: (i, 0, j)),
        out_shape=jax.ShapeDtypeStruct((b, s, h * HEAD_DIM), F32),
        scratch_shapes=[pltpu.VMEM((hp, V_ROWS, ATTN_Q_TILE), F32)]
        + [pltpu.VMEM((tk, ATTN_Q_TILE), F32)] * (2 * hp),
        compiler_params=_params(2),
        name="fox_attention",
    )(q_aug, k_aug, vt_aug)


def _out_body(x_ref, mod_ref, pre_ref, post_ref, a_ref, wg_ref, wo_ref, o_ref):
    x = x_ref[...]
    h = _prenorm(x, pre_ref[...], mod_ref[0:1, :], mod_ref[1:2, :]).astype(BF16)
    gate = _sigmoid(jnp.dot(h, wg_ref[...], preferred_element_type=F32))
    y = jnp.dot((a_ref[...] * gate).astype(BF16), wo_ref[...], preferred_element_type=F32)
    o_ref[...] = _gated_residual(x, y, post_ref[...], mod_ref[2:3, :])


def _fox_output(x, mod, pre_g, post_g, o_t, w_g, w_o):
    b, s, d = x.shape
    tm = TOKEN_TILE
    tok = pl.BlockSpec((None, tm, d), lambda i, j: (i, j, 0))
    return pl.pallas_call(
        _out_body,
        grid=(b, s // tm),
        in_specs=[
            tok,
            pl.BlockSpec((None, 6, d), lambda i, j: (i, 0, 0)),
            _const_spec((1, d)), _const_spec((1, d)),
            tok,
            _const_spec((d, d)), _const_spec((d, d)),
        ],
        out_specs=tok,
        out_shape=jax.ShapeDtypeStruct((b, s, d), F32),
        compiler_params=_params(2),
        name="fox_output",
    )(x, mod, pre_g, post_g, o_t, w_g, w_o)


def kernel(x, c, ada_w, ada_b, pre_mix_g, post_mix_g, pre_ffn_g, post_ffn_g, ffn_w_gu, ffn_w_down,
           a_w_in, a_b_in, a_ln_g, a_ln_b, a_w_s, a_b_s, a_w_out, kv_ada_w, kv_ada_b, kv_norm_g,
           kv_w, kv_b_f, k_norm_g, b_w_qg, b_q_norm_g, b_w_o):
    b, s, d = x.shape
    depth = ada_w.shape[0]
    n_a = a_w_in.shape[0]
    assert d == N_HEADS * HEAD_DIM and s % ATTN_Q_TILE == 0

    c8 = jnp.pad(c, ((0, 8 - b), (0, 0)))
    mods = _modulation(c8, ada_w, ada_b)[:, :b].reshape(depth, b, 6, d)
    kvmod = _modulation(c8, kv_ada_w[None], kv_ada_b[None])[0, :b].reshape(b, 2, d)

    row = lambda v: v.reshape(1, -1)
    q_scale = LOG2E * HEAD_DIM ** -0.5

    for layer in range(depth):
        mod = mods[layer]
        if layer < n_a:
            i = layer
            bs_full = jnp.repeat(a_b_s[i].T, CHUNK, axis=1)
            x = _gmlp_layer(x, mod, row(pre_mix_g[layer]), row(post_mix_g[layer]),
                            _lane_padded(a_w_in[i]), row(a_b_in[i]), row(a_ln_g[i]), row(a_ln_b[i]),
                            a_w_s[i], bs_full, _lane_padded(a_w_out[i]))
        else:
            j = layer - n_a
            qg2 = row(jnp.tile(b_q_norm_g[j] * q_scale, 2))
            q_aug = _q_project(x, mod, row(pre_mix_g[layer]), b_w_qg[j][:, :d].astype(BF16), qg2, d3)
            o_t = _fox_attention(q_aug, k_aug, vt_aug)
            x = _fox_output(x, mod, row(pre_mix_g[layer]), row(post_mix_g[layer]), o_t,
                            b_w_qg[j][:, d:].astype(BF16), b_w_o[j].astype(BF16))
        x = _ffn_layer(x, mod, row(pre_ffn_g[layer]), row(post_ffn_g[layer]),
                       ffn_w_gu[layer].astype(BF16), ffn_w_down[layer].astype(BF16))
        if layer == n_a - 1:
            w_f = kv_w[:, 2 * d:]
            pad = jnp.zeros((d, LANES - 3 * N_HEADS), F32)
            w_f3 = jnp.concatenate([w_f, w_f, w_f, pad], axis=1).astype(BF16)
            b_f3 = row(jnp.concatenate([kv_b_f, kv_b_f, kv_b_f, jnp.zeros((LANES - 3 * N_HEADS,), F32)]))
            k_aug, vt_aug, d3 = _kv_project(
                x, kvmod, row(kv_norm_g), kv_w[:, :d].astype(BF16), kv_w[:, d:2 * d].astype(BF16),
                w_f3, b_f3, row(jnp.tile(k_norm_g, 2)))
    return x
```

```python
import functools

import jax
import jax.numpy as jnp
from jax import lax
from jax.experimental import pallas as pl
from jax.experimental.pallas import tpu as pltpu

F32 = jnp.float32
BF16 = jnp.bfloat16

EPS = 1e-6
N_HEADS = 16
HEAD_DIM = 64
CHUNK = 128
GROUPS = 16
LOG2E = 1.4426950408889634
LANES = 128
MXU_N = 256
TOKEN_TILE = 512
WIDE_TOKEN_TILE = 2 * TOKEN_TILE
ATTN_Q_TILE = 2 * TOKEN_TILE
ATTN_HEADS_PER_STEP = 2
BF16_SUBLANES = 16
V_ROWS = HEAD_DIM + BF16_SUBLANES
VMEM_LIMIT = 56 * 1024 * 1024


def _sigmoid(x):
    return 1.0 / (1.0 + jnp.exp(-x))


def _unit_rms(x):
    return x * lax.rsqrt(jnp.mean(x * x, axis=-1, keepdims=True) + EPS)


def _prenorm(x, g, shift, scale):
    return _unit_rms(x) * (g * (1.0 + scale)) + shift


def _gated_residual(x, y, g, gate):
    return x + _unit_rms(y) * (g * gate)


def _split3(x):
    hi = x.astype(BF16).astype(F32)
    r = x - hi
    mid = r.astype(BF16).astype(F32)
    lo = (r - mid).astype(BF16).astype(F32)
    return hi, mid, lo


def _const_spec(shape):
    n = len(shape)
    return pl.BlockSpec(shape, lambda *_: (0,) * n, pipeline_mode=pl.Buffered(1))


def _lane_padded(w):
    return jnp.pad(w.astype(BF16), ((0, 0), (0, LANES)))


def _params(n_grid):
    return pltpu.CompilerParams(
        dimension_semantics=("arbitrary",) * n_grid, vmem_limit_bytes=VMEM_LIMIT)


def _mod_body(c_ref, w_ref, b_ref, o_ref):
    c = c_ref[...]
    ca = (c * _sigmoid(c)).astype(BF16)
    o_ref[...] = jnp.dot(ca, w_ref[...].astype(BF16), preferred_element_type=F32) + b_ref[...]


def _modulation(c8, w, b):
    n_l, d, n = w.shape
    nb = 1024
    return pl.pallas_call(
        _mod_body,
        grid=(n_l, n // nb),
        in_specs=[
            pl.BlockSpec((8, d), lambda l, j: (0, 0)),
            pl.BlockSpec((None, d, nb), lambda l, j: (l, 0, j)),
            pl.BlockSpec((None, 1, nb), lambda l, j: (l, 0, j)),
        ],
        out_specs=pl.BlockSpec((None, 8, nb), lambda l, j: (l, 0, j)),
        out_shape=jax.ShapeDtypeStruct((n_l, 8, n), F32),
        compiler_params=_params(2),
        name="modulation",
    )(c8, w, b.reshape(n_l, 1, n))


def _gmlp_body(x_ref, mod_ref, pre_ref, post_ref, win_ref, bin_ref, lng_ref, lnb_ref, ws_ref,
               bs_ref, wout_ref, o_ref, h_scr, u_scr, v_scr, vn_scr, y_scr):
    tm, d = x_ref.shape
    gw = u_scr.shape[1]
    x = x_ref[...]
    h_scr[...] = _prenorm(x, pre_ref[...], mod_ref[0:1, :], mod_ref[1:2, :]).astype(BF16)

    k_gelu = 0.7978845608028654
    nc = 512
    for c0 in range(0, 2 * gw, nc):
        z = jnp.dot(h_scr[...], win_ref[:, c0:c0 + nc], preferred_element_type=F32)
        z = z + bin_ref[:, c0:c0 + nc]
        z = z * (0.5 * (1.0 + jnp.tanh(k_gelu * (z + 0.044715 * (z * z * z)))))
        if c0 < gw:
            u_scr[:, c0:c0 + nc] = z
        else:
            v_scr[:, c0 - gw:c0 - gw + nc] = z

    v = v_scr[...]
    mu = jnp.mean(v, axis=-1, keepdims=True)
    vc = v - mu
    rstd = lax.rsqrt(jnp.mean(vc * vc, axis=-1, keepdims=True) + EPS)
    vn_scr[...] = (vc * rstd * lng_ref[...] + lnb_ref[...]).astype(BF16)

    n_chunks = tm // CHUNK
    row = lax.broadcasted_iota(jnp.int32, (CHUNK, CHUNK), 0)
    col = lax.broadcasted_iota(jnp.int32, (CHUNK, CHUNK), 1)
    causal = col <= row
    for g in range(GROUPS):
        gs = slice(g * CHUNK, (g + 1) * CHUNK)
        ws = jnp.where(causal, ws_ref[g], 0.0).astype(BF16)
        rhs = jnp.concatenate(
            [vn_scr[c * CHUNK:(c + 1) * CHUNK, gs] for c in range(n_chunks)], axis=1)
        sp = jnp.dot(ws, rhs, preferred_element_type=F32)
        for c in range(n_chunks):
            cs = slice(c * CHUNK, (c + 1) * CHUNK)
            y_scr[cs, gs] = (u_scr[cs, gs] * (sp[:, cs] + bs_ref[:, gs])).astype(BF16)

    y = jnp.dot(y_scr[...], wout_ref[:, 0:d], preferred_element_type=F32)
    o_ref[...] = _gated_residual(x, y, post_ref[...], mod_ref[2:3, :])


def _gmlp_layer(x, mod, pre_g, post_g, w_in, b_in, ln_g, ln_b, w_s, bs_full, w_out):
    b, s, d = x.shape
    gw = w_out.shape[0]
    tm = TOKEN_TILE
    tok = pl.BlockSpec((None, tm, d), lambda i, j: (i, j, 0))
    return pl.pallas_call(
        _gmlp_body,
        grid=(b, s // tm),
        in_specs=[
            tok,
            pl.BlockSpec((None, 6, d), lambda i, j: (i, 0, 0)),
            _const_spec((1, d)), _const_spec((1, d)),
            _const_spec(w_in.shape), _const_spec((1, 2 * gw)),
            _const_spec((1, gw)), _const_spec((1, gw)),
            _const_spec((GROUPS, CHUNK, CHUNK)), _const_spec((CHUNK, gw)),
            _const_spec(w_out.shape),
        ],
        out_specs=tok,
        out_shape=jax.ShapeDtypeStruct((b, s, d), F32),
        scratch_shapes=[
            pltpu.VMEM((tm, d), BF16), pltpu.VMEM((tm, gw), F32), pltpu.VMEM((tm, gw), F32),
            pltpu.VMEM((tm, gw), BF16), pltpu.VMEM((tm, gw), BF16),
        ],
        compiler_params=_params(2),
        name="gmlp_mixer",
    )(x, mod, pre_g, post_g, w_in, b_in, ln_g, ln_b, w_s, bs_full, w_out)


def _ffn_body(x_ref, mod_ref, pre_ref, post_ref, wgu_ref, wd_ref, o_ref, h_scr, a_scr):
    f = a_scr.shape[1]
    x = x_ref[...]
    h_scr[...] = _prenorm(x, pre_ref[...], mod_ref[3:4, :], mod_ref[4:5, :]).astype(BF16)
    for c0 in range(0, f, MXU_N):
        g = jnp.dot(h_scr[...], wgu_ref[:, c0:c0 + MXU_N], preferred_element_type=F32)
        u = jnp.dot(h_scr[...], wgu_ref[:, f + c0:f + c0 + MXU_N], preferred_element_type=F32)
        a_scr[:, c0:c0 + MXU_N] = (g * _sigmoid(g) * u).astype(BF16)
    y = jnp.dot(a_scr[...], wd_ref[...], preferred_element_type=F32)
    o_ref[...] = _gated_residual(x, y, post_ref[...], mod_ref[5:6, :])


def _ffn_layer(x, mod, pre_g, post_g, w_gu, w_down):
    b, s, d = x.shape
    f = w_down.shape[0]
    tm = WIDE_TOKEN_TILE
    tok = pl.BlockSpec((None, tm, d), lambda i, j: (i, j, 0))
    return pl.pallas_call(
        _ffn_body,
        grid=(b, s // tm),
        in_specs=[
            tok,
            pl.BlockSpec((None, 6, d), lambda i, j: (i, 0, 0)),
            _const_spec((1, d)), _const_spec((1, d)),
            _const_spec((d, 2 * f)), _const_spec((f, d)),
        ],
        out_specs=tok,
        out_shape=jax.ShapeDtypeStruct((b, s, d), F32),
        scratch_shapes=[pltpu.VMEM((tm, d), BF16), pltpu.VMEM((tm, f), BF16)],
        compiler_params=_params(2),
        name="swiglu_ffn",
    )(x, mod, pre_g, post_g, w_gu, w_down)


def _head_pair_norm(t, gain2):
    lane = lax.broadcasted_iota(jnp.int32, t.shape, 1)
    sq = t * t
    first = lane < HEAD_DIM
    s0 = jnp.sum(jnp.where(first, sq, 0.0), axis=-1, keepdims=True)
    s1 = jnp.sum(jnp.where(first, 0.0, sq), axis=-1, keepdims=True)
    rs = jnp.where(first, lax.rsqrt(s0 / HEAD_DIM + EPS), lax.rsqrt(s1 / HEAD_DIM + EPS))
    return t * rs * gain2


def _decay_selectors(query_side):
    row = lax.broadcasted_iota(jnp.int32, (LANES, N_HEADS * LANES), 0)
    col = lax.broadcasted_iota(jnp.int32, (LANES, N_HEADS * LANES), 1)
    head, off = col // LANES, col % LANES - HEAD_DIM
    first, second = (off >= 0) & (off < 3), (off >= 3) & (off < 6)
    piece_cols, piece_idx = (first, off) if query_side else (second, off - 3)
    one_cols = second if query_side else first
    is_piece = piece_cols & (row == piece_idx * N_HEADS + head)
    is_one = one_cols & (row == 3 * N_HEADS)
    sel = jnp.where(is_piece, 1.0 if query_side else -1.0, jnp.where(is_one, 1.0, 0.0))
    return sel.astype(BF16)


def _with_tail(t, tail):
    lane = lax.broadcasted_iota(jnp.int32, t.shape, 1)
    return jnp.where(lane < HEAD_DIM, t, tail)


def _kv_body(x_ref, mod_ref, g_ref, wk_ref, wv_ref, wf_ref, bf_ref, kg_ref, sel_ref,
             k_out, vt_out, d_out, h_scr, carry_scr):
    tm, d = x_ref.shape

    @pl.when(pl.program_id(1) == 0)
    def _():
        carry_scr[...] = jnp.zeros_like(carry_scr)

    h_scr[...] = _prenorm(x_ref[...], g_ref[...], mod_ref[0:1, :], mod_ref[1:2, :]).astype(BF16)

    lane = lax.broadcasted_iota(jnp.int32, (tm, LANES), 1)
    fl = jnp.dot(h_scr[...], wf_ref[...], preferred_element_type=F32) + bf_ref[...]
    z = -fl
    ls = -(jnp.maximum(z, 0.0) + jnp.log1p(jnp.exp(-jnp.abs(z))))
    hi, mid, lo = _split3(ls)
    pieces = jnp.where(lane < N_HEADS, hi, jnp.where(lane < 2 * N_HEADS, mid, jnp.where(
        lane < 3 * N_HEADS, lo, 0.0))).astype(BF16)
    r = lax.broadcasted_iota(jnp.int32, (tm, tm), 0)
    c = lax.broadcasted_iota(jnp.int32, (tm, tm), 1)
    tri = jnp.where(c <= r, 1.0, 0.0).astype(BF16)
    cs = jnp.dot(tri, pieces, preferred_element_type=F32)
    tot = cs + pltpu.roll(cs, LANES - N_HEADS, axis=1) + pltpu.roll(cs, LANES - 2 * N_HEADS, axis=1)
    dc = jnp.where(lane < N_HEADS, tot + carry_scr[...], 0.0)
    carry_scr[...] = dc[tm - 1:tm, :]
    hi, mid, lo = _split3(dc * LOG2E)
    d3 = (hi + pltpu.roll(mid, N_HEADS, axis=1) + pltpu.roll(lo, 2 * N_HEADS, axis=1)
          + jnp.where(lane == 3 * N_HEADS, 1.0, 0.0)).astype(BF16)
    d_out[...] = d3

    kg2 = kg_ref[...]
    ones_rows = jnp.where(
        lax.broadcasted_iota(jnp.int32, (V_ROWS - HEAD_DIM, tm), 0) == 0, 1.0, 0.0)
    heads_per_chunk = MXU_N // HEAD_DIM
    for c4 in range(d // MXU_N):
        kc =jnp.dot(h_scr[...], wk_ref[:, c4 * MXU_N:(c4 + 1) * MXU_N], preferred_element_type=F32)
        vc = jnp.dot(h_scr[...], wv_ref[:, c4 * MXU_N:(c4 + 1) * MXU_N], preferred_element_type=F32)
        tails = jnp.dot(d3, sel_ref[:, c4 * heads_per_chunk * LANES:(c4 + 1) * heads_per_chunk * LANES],
                        preferred_element_type=F32)
        for half in range(2):
            kn = _head_pair_norm(kc[:, half * LANES:(half + 1) * LANES], kg2)
            vt = vc[:, half * LANES:(half + 1) * LANES].T
            for sub, src in ((0, kn), (1, pltpu.roll(kn, HEAD_DIM, axis=1))):
                loc = 2 * half + sub
                hd = c4 * heads_per_chunk + loc
                k_out[hd] = _with_tail(src, tails[:, loc * LANES:(loc + 1) * LANES]).astype(BF16)
                vt_out[hd] = jnp.concatenate(
                    [vt[sub * HEAD_DIM:(sub + 1) * HEAD_DIM, :], ones_rows], axis=0).astype(BF16)


def _kv_project(x, kvmod, norm_g, w_k, w_v, w_f3, b_f3, kg2):
    b, s, d = x.shape
    tm = TOKEN_TILE
    return pl.pallas_call(
        _kv_body,
        grid=(b, s // tm),
        in_specs=[
            pl.BlockSpec((None, tm, d), lambda i, j: (i, j, 0)),
            pl.BlockSpec((None, 2, d), lambda i, j: (i, 0, 0)),
            _const_spec((1, d)),
            _const_spec((d, d)), _const_spec((d, d)), _const_spec((d, LANES)),
            _const_spec((1, LANES)), _const_spec((1, LANES)), _const_spec((LANES, N_HEADS * LANES)),
        ],
        out_specs=[
            pl.BlockSpec((None, N_HEADS, tm, LANES), lambda i, j: (i, 0, j, 0)),
            pl.BlockSpec((None, N_HEADS, None, V_ROWS, tm), lambda i, j: (i, 0, j, 0, 0)),
            pl.BlockSpec((None, tm, LANES), lambda i, j: (i, j, 0)),
        ],
        out_shape=[
            jax.ShapeDtypeStruct((b, N_HEADS, s, LANES), BF16),
            jax.ShapeDtypeStruct((b, N_HEADS, s // tm, V_ROWS, tm), BF16),
            jax.ShapeDtypeStruct((b, s, LANES), BF16),
        ],
        scratch_shapes=[pltpu.VMEM((tm, d), BF16), pltpu.VMEM((1, LANES), F32)],
        compiler_params=_params(2),
        name="kv_project",
    )(x, kvmod, norm_g, w_k, w_v, w_f3, b_f3, kg2, _decay_selectors(query_side=False))


def _q_body(x_ref, mod_ref, pre_ref, wq_ref, qg_ref, d_ref, sel_ref, q_out, h_scr):
    tm, d = x_ref.shape
    h_scr[...] = _prenorm(x_ref[...], pre_ref[...], mod_ref[0:1, :], mod_ref[1:2, :]).astype(BF16)
    d3 = d_ref[...]
    qg2 = qg_ref[...]
    heads_per_chunk = MXU_N // HEAD_DIM
    for c4 in range(d // MXU_N):
        qc = jnp.dot(h_scr[...], wq_ref[:, c4 * MXU_N:(c4 + 1) * MXU_N], preferred_element_type=F32)
        tails = jnp.dot(d3, sel_ref[:, c4 * heads_per_chunk * LANES:(c4 + 1) * heads_per_chunk * LANES],
                        preferred_element_type=F32)
        for half in range(2):
            qn = _head_pair_norm(qc[:, half * LANES:(half + 1) * LANES], qg2)
            for sub, src in ((0, qn), (1, pltpu.roll(qn, HEAD_DIM, axis=1))):
                loc = 2 * half + sub
                qa = _with_tail(src, tails[:, loc * LANES:(loc + 1) * LANES])
                q_out[c4 * heads_per_chunk + loc] = qa.astype(BF16)


def _q_project(x, mod, pre_g, w_q, qg2, d3):
    b, s, d = x.shape
    tm = TOKEN_TILE
    return pl.pallas_call(
        _q_body,
        grid=(b, s // tm),
        in_specs=[
            pl.BlockSpec((None, tm, d), lambda i, j: (i, j, 0)),
            pl.BlockSpec((None, 6, d), lambda i, j: (i, 0, 0)),
            _const_spec((1, d)), _const_spec((d, d)), _const_spec((1, LANES)),
            pl.BlockSpec((None, tm, LANES), lambda i, j: (i, j, 0)),
            _const_spec((LANES, N_HEADS * LANES)),
        ],
        out_specs=pl.BlockSpec((None, N_HEADS, tm, LANES), lambda i, j: (i, 0, j, 0)),
        out_shape=jax.ShapeDtypeStruct((b, N_HEADS, s, LANES), BF16),
        scratch_shapes=[pltpu.VMEM((tm, d), BF16)],
        compiler_params=_params(2),
        name="q_project",
    )(x, mod, pre_g, w_q, qg2, d3, _decay_selectors(query_side=True))


def _attn_body(q_ref, k_ref, vt_ref, o_ref, acc_scr, *s_scr):
    n_h = q_ref.shape[0]
    tk = vt_ref.shape[3]
    tq = 2 * tk
    nq = q_ref.shape[1] // tq
    heads = range(n_h)
    halves = (slice(0, tk), slice(tk, tq))
    kpos = lax.broadcasted_iota(jnp.int32, (tk, tk), 0)
    qpos = lax.broadcasted_iota(jnp.int32, (tk, tk), 1)
    tri = kpos <= qpos
    LO, HI = 0, 1

    def scores(qt, j, slot, g, valid):
        cms = []
        for h in heads:
            q = q_ref[h, pl.ds(pl.multiple_of(qt * tq + g * tk, tk), tk), :]
            k = k_ref[h, pl.ds(pl.multiple_of(j * tk, tk), tk), :]
            s = lax.dot_general(k, q, (((1,), (1,)), ((), ())), preferred_element_type=F32)
            if valid is not None:
                s = jnp.where(valid, s, -jnp.inf)
            s_scr[2 * h + slot][:, halves[g]] = s
            cms.append(jnp.max(s, axis=0, keepdims=True))
        return tuple(cms)

    def accumulate(j, slot, g, cmax, ml):
        out = []
        for h in heads:
            m_new = jnp.maximum(ml[h], cmax[h])
            alpha = jnp.exp2(ml[h] - m_new)
            p = jnp.exp2(s_scr[2 * h + slot][:, halves[g]] - m_new)
            pv = jnp.dot(vt_ref[h, j], p.astype(BF16), preferred_element_type=F32)
            acc_scr[h, :, halves[g]] = alpha * acc_scr[h, :, halves[g]] + pv
            out.append(m_new)
        return tuple(out)

    def pair(qi, p, carry, diag):
        (ml_lo, ml_hi), (c0_lo, c0_hi) = carry
        c1_lo = scores(qi, 2 * p + 1, 1, LO, None)
        ml_lo = accumulate(2 * p, 0, LO, c0_lo, ml_lo)
        c1_hi = scores(qi, 2 * p + 1, 1, HI, None)
        ml_hi = accumulate(2 * p, 0, HI, c0_hi, ml_hi)
        c0_lo = scores(qi, 2 * p + 2, 0, LO, tri if diag else None)
        ml_lo = accumulate(2 * p + 1, 1, LO, c1_lo, ml_lo)
        c0_hi = scores(qi, 2 * p + 2, 0, HI, None)
        ml_hi = accumulate(2 * p + 1, 1, HI, c1_hi, ml_hi)
        return (ml_lo, ml_hi), (c0_lo, c0_hi)

    def query_tile(qi, cm0):
        acc_scr[...] = jnp.zeros_like(acc_scr)
        fresh = tuple(jnp.full((1, tk), -jnp.inf, F32) for _ in heads)
        carry = lax.fori_loop(0, qi - 1, lambda p, c: pair(qi, p, c, False), ((fresh, fresh), cm0))

        def finish(carry, with_pair):
            if with_pair:
                carry = pair(qi, qi - 1, carry, True)
            (ml_lo, ml_hi), (c0_lo, c0_hi) = carry
            ml_lo = accumulate(2 * qi, 0, LO, c0_lo, ml_lo)
            c1_hi = scores(qi, 2 * qi + 1, 1, HI, tri)
            ml_hi = accumulate(2 * qi, 0, HI, c0_hi, ml_hi)
            nxt = jnp.minimum(qi + 1, nq - 1)
            n_lo = scores(nxt, 0, 0, LO, None)
            accumulate(2 * qi + 1, 1, HI, c1_hi, ml_hi)
            n_hi = scores(nxt, 0, 0, HI, None)
            for g in (LO, HI):
                o_t = jnp.concatenate(
                    [acc_scr[h, 0:HEAD_DIM, halves[g]] / acc_scr[h, HEAD_DIM:HEAD_DIM + 1, halves[g]]
                     for h in heads], axis=0)
                o_ref[pl.ds(pl.multiple_of((2 * qi + g) * tk, tk), tk), :] = o_t.T
            return n_lo, n_hi

        return lax.cond(qi > 0, lambda c: finish(c, True), lambda c: finish(c, False), carry)

    lax.fori_loop(0, nq, query_tile, (scores(0, 0, 0, LO, tri), scores(0, 0, 0, HI, None)))


def _fox_attention(q_aug, k_aug, vt_aug):
    b, h, s, _ = k_aug.shape
    tk = TOKEN_TILE
    nk = s // tk
    hp = ATTN_HEADS_PER_STEP
    return pl.pallas_call(
        _attn_body,
        grid=(b, h // hp),
        in_specs=[
            pl.BlockSpec((None, hp, s, LANES), lambda i, j: (i, j, 0, 0)),
            pl.BlockSpec((None, hp, s, LANES), lambda i, j: (i, j, 0, 0)),
            pl.BlockSpec((None, hp, nk, V_ROWS, tk), lambda i, j: (i, j, 0, 0, 0)),
        ],
        out_specs=pl.BlockSpec((None, s, hp * HEAD_DIM), lambda i, j: (i, 0, j)),
        out_shape=jax.ShapeDtypeStruct((b, s, h * HEAD_DIM), F32),
        scratch_shapes=[pltpu.VMEM((hp, V_ROWS, ATTN_Q_TILE), F32)]
        + [pltpu.VMEM((tk, ATTN_Q_TILE), F32)] * (2 * hp),
        compiler_params=_params(2),
        name="fox_attention",
    )(q_aug, k_aug, vt_aug)


def _out_body(x_ref, mod_ref, pre_ref, post_ref, a_ref, wg_ref, wo_ref, o_ref):
    x = x_ref[...]
    h = _prenorm(x, pre_ref[...], mod_ref[0:1, :], mod_ref[1:2, :]).astype(BF16)
    gate = _sigmoid(jnp.dot(h, wg_ref[...], preferred_element_type=F32))
    y = jnp.dot((a_ref[...] * gate).astype(BF16), wo_ref[...], preferred_element_type=F32)
    o_ref[...] = _gated_residual(x, y, post_ref[...], mod_ref[2:3, :])


def _fox_output(x, mod, pre_g, post_g, o_t, w_g, w_o):
    b, s, d = x.shape
    tm = WIDE_TOKEN_TILE
    tok = pl.BlockSpec((None, tm, d), lambda i, j: (i, j, 0))
    return pl.pallas_call(
        _out_body,
        grid=(b, s // tm),
        in_specs=[
            tok,
            pl.BlockSpec((None, 6, d), lambda i, j: (i, 0, 0)),
            _const_spec((1, d)), _const_spec((1, d)),
            tok,
            _const_spec((d, d)), _const_spec((d, d)),
        ],
        out_specs=tok,
        out_shape=jax.ShapeDtypeStruct((b, s, d), F32),
        compiler_params=_params(2),
        name="fox_output",
    )(x, mod, pre_g, post_g, o_t, w_g, w_o)


def kernel(x, c, ada_w, ada_b, pre_mix_g, post_mix_g, pre_ffn_g, post_ffn_g, ffn_w_gu, ffn_w_down,
           a_w_in, a_b_in, a_ln_g, a_ln_b, a_w_s, a_b_s, a_w_out, kv_ada_w, kv_ada_b, kv_norm_g,
           kv_w, kv_b_f, k_norm_g, b_w_qg, b_q_norm_g, b_w_o):
    b, s, d = x.shape
    depth = ada_w.shape[0]
    n_a = a_w_in.shape[0]
    assert d == N_HEADS * HEAD_DIM and s % ATTN_Q_TILE == 0

    c8 = jnp.pad(c, ((0, 8 - b), (0, 0)))
    mods = _modulation(c8, ada_w, ada_b)[:, :b].reshape(depth, b, 6, d)
    kvmod = _modulation(c8, kv_ada_w[None], kv_ada_b[None])[0, :b].reshape(b, 2, d)

    row = lambda v: v.reshape(1, -1)
    q_scale = LOG2E * HEAD_DIM ** -0.5

    for layer in range(depth):
        mod = mods[layer]
        if layer < n_a:
            i = layer
            bs_full = jnp.repeat(a_b_s[i].T, CHUNK, axis=1)
            x = _gmlp_layer(x, mod, row(pre_mix_g[layer]), row(post_mix_g[layer]),
                            _lane_padded(a_w_in[i]), row(a_b_in[i]), row(a_ln_g[i]), row(a_ln_b[i]),
                            a_w_s[i], bs_full, _lane_padded(a_w_out[i]))
        else:
            j = layer - n_a
            qg2 = row(jnp.tile(b_q_norm_g[j] * q_scale, 2))
            q_aug = _q_project(x, mod, row(pre_mix_g[layer]), b_w_qg[j][:, :d].astype(BF16), qg2, d3)
            o_t = _fox_attention(q_aug, k_aug, vt_aug)
            x = _fox_output(x, mod, row(pre_mix_g[layer]), row(post_mix_g[layer]), o_t,
                            b_w_qg[j][:, d:].astype(BF16), b_w_o[j].astype(BF16))
        x = _ffn_layer(x, mod, row(pre_ffn_g[layer]), row(post_ffn_g[layer]),
                       ffn_w_gu[layer].astype(BF16), ffn_w_down[layer].astype(BF16))
        if layer == n_a - 1:
            w_f = kv_w[:, 2 * d:]
            pad = jnp.zeros((d, LANES - 3 * N_HEADS), F32)
            w_f3 = jnp.concatenate([w_f, w_f, w_f, pad], axis=1).astype(BF16)
            b_f3 = row(jnp.concatenate([kv_b_f, kv_b_f, kv_b_f, jnp.zeros((LANES - 3 * N_HEADS,), F32)]))
            k_aug, vt_aug, d3 = _kv_project(
                x, kvmod, row(kv_norm_g), kv_w[:, :d].astype(BF16), kv_w[:, d:2 * d].astype(BF16),
                w_f3, b_f3, row(jnp.tile(k_norm_g, 2)))
    return x
```

```python
import functools

import jax
import jax.numpy as jnp
from jax import lax
from jax.experimental import pallas as pl
from jax.experimental.pallas import tpu as pltpu

F32 = jnp.float32
BF16 = jnp.bfloat16

EPS = 1e-6
N_HEADS = 16
HEAD_DIM = 64
CHUNK = 128
GROUPS = 16
LOG2E = 1.4426950408889634
LANES = 128
MXU_N = 256
TOKEN_TILE = 512
WIDE_TOKEN_TILE = 2 * TOKEN_TILE
ATTN_Q_TILE = 2 * TOKEN_TILE
ATTN_HEADS_PER_STEP = 2
BF16_SUBLANES = 16
V_ROWS = HEAD_DIM + BF16_SUBLANES
VMEM_LIMIT = 56 * 1024 * 1024


def _sigmoid(x):
    return 1.0 / (1.0 + jnp.exp(-x))


def _unit_rms(x):
    return x * lax.rsqrt(jnp.mean(x * x, axis=-1, keepdims=True) + EPS)


def _prenorm(x, g, shift, scale):
    return _unit_rms(x) * (g * (1.0 + scale)) + shift


def _gated_residual(x, y, g, gate):
    return x + _unit_rms(y) * (g * gate)


def _split3(x):
    hi = x.astype(BF16).astype(F32)
    r = x - hi
    mid = r.astype(BF16).astype(F32)
    lo = (r - mid).astype(BF16).astype(F32)
    return hi, mid, lo


def _const_spec(shape):
    n = len(shape)
    return pl.BlockSpec(shape, lambda *_: (0,) * n, pipeline_mode=pl.Buffered(1))


def _lane_padded(w):
    return jnp.pad(w.astype(BF16), ((0, 0), (0, LANES)))


def _params(n_grid):
    return pltpu.CompilerParams(
        dimension_semantics=("arbitrary",) * n_grid, vmem_limit_bytes=VMEM_LIMIT)


def _mod_body(c_ref, w_ref, b_ref, o_ref):
    c = c_ref[...]
    ca = (c * _sigmoid(c)).astype(BF16)
    o_ref[...] = jnp.dot(ca, w_ref[...].astype(BF16), preferred_element_type=F32) + b_ref[...]


def _modulation(c8, w, b):
    n_l, d, n = w.shape
    nb = 1024
    return pl.pallas_call(
        _mod_body,
        grid=(n_l, n // nb),
        in_specs=[
            pl.BlockSpec((8, d), lambda l, j: (0, 0)),
            pl.BlockSpec((None, d, nb), lambda l, j: (l, 0, j)),
            pl.BlockSpec((None, 1, nb), lambda l, j: (l, 0, j)),
        ],
        out_specs=pl.BlockSpec((None, 8, nb), lambda l, j: (l, 0, j)),
        out_shape=jax.ShapeDtypeStruct((n_l, 8, n), F32),
        compiler_params=_params(2),
        name="modulation",
    )(c8, w, b.reshape(n_l, 1, n))


def _gmlp_body(x_ref, mod_ref, pre_ref, post_ref, win_ref, bin_ref, lng_ref, lnb_ref, ws_ref,
               bs_ref, wout_ref, o_ref, h_scr, u_scr, v_scr, vn_scr, y_scr):
    tm, d = x_ref.shape
    gw = u_scr.shape[1]
    x = x_ref[...]
    h_scr[...] = _prenorm(x, pre_ref[...], mod_ref[0:1, :], mod_ref[1:2, :]).astype(BF16)

    k_gelu = 0.7978845608028654
    nc = 512
    for c0 in range(0, 2 * gw, nc):
        z = jnp.dot(h_scr[...], win_ref[:, c0:c0 + nc], preferred_element_type=F32)
        z = z + bin_ref[:, c0:c0 + nc]
        z = z * (0.5 * (1.0 + jnp.tanh(k_gelu * (z + 0.044715 * (z * z * z)))))
        if c0 < gw:
            u_scr[:, c0:c0 + nc] = z
        else:
            v_scr[:, c0 - gw:c0 - gw + nc] = z

    v = v_scr[...]
    mu = jnp.mean(v, axis=-1, keepdims=True)
    vc = v - mu
    rstd = lax.rsqrt(jnp.mean(vc * vc, axis=-1, keepdims=True) + EPS)
    vn_scr[...] = (vc * rstd * lng_ref[...] + lnb_ref[...]).astype(BF16)

    n_chunks = tm // CHUNK
    row = lax.broadcasted_iota(jnp.int32, (CHUNK, CHUNK), 0)
    col = lax.broadcasted_iota(jnp.int32, (CHUNK, CHUNK), 1)
    causal = col <= row
    for g in range(GROUPS):
        gs = slice(g * CHUNK, (g + 1) * CHUNK)
        ws = jnp.where(causal, ws_ref[g], 0.0).astype(BF16)
        rhs = jnp.concatenate(
            [vn_scr[c * CHUNK:(c + 1) * CHUNK, gs] for c in range(n_chunks)], axis=1)
        sp = jnp.dot(ws, rhs, preferred_element_type=F32)
        for c in range(n_chunks):
            cs = slice(c * CHUNK, (c + 1) * CHUNK)
            y_scr[cs, gs] = (u_scr[cs, gs] * (sp[:, cs] + bs_ref[:, gs])).astype(BF16)

    y = jnp.dot(y_scr[...], wout_ref[:, 0:d], preferred_element_type=F32)
    o_ref[...] = _gated_residual(x, y, post_ref[...], mod_ref[2:3, :])


def _gmlp_layer(x, mod, pre_g, post_g, w_in, b_in, ln_g, ln_b, w_s, bs_full, w_out):
    b, s, d = x.shape
    gw = w_out.shape[0]
    tm = TOKEN_TILE
    tok = pl.BlockSpec((None, tm, d), lambda i, j: (i, j, 0))
    return pl.pallas_call(
        _gmlp_body,
        grid=(b, s // tm),
        in_specs=[
            tok,
            pl.BlockSpec((None, 6, d), lambda i, j: (i, 0, 0)),
            _const_spec((1, d)), _const_spec((1, d)),
            _const_spec(w_in.shape), _const_spec((1, 2 * gw)),
            _const_spec((1, gw)), _const_spec((1, gw)),
            _const_spec((GROUPS, CHUNK, CHUNK)), _const_spec((CHUNK, gw)),
            _const_spec(w_out.shape),
        ],
        out_specs=tok,
        out_shape=jax.ShapeDtypeStruct((b, s, d), F32),
        scratch_shapes=[
            pltpu.VMEM((tm, d), BF16), pltpu.VMEM((tm, gw), F32), pltpu.VMEM((tm, gw), F32),
            pltpu.VMEM((tm, gw), BF16), pltpu.VMEM((tm, gw), BF16),
        ],
        compiler_params=_params(2),
        name="gmlp_mixer",
    )(x, mod, pre_g, post_g, w_in, b_in, ln_g, ln_b, w_s, bs_full, w_out)


def _ffn_body(x_ref, mod_ref, pre_ref, post_ref, wgu_ref, wd_ref, o_ref, h_scr, a_scr):
    f = a_scr.shape[1]
    x = x_ref[...]
    h_scr[...] = _prenorm(x, pre_ref[...], mod_ref[3:4, :], mod_ref[4:5, :]).astype(BF16)
    for c0 in range(0, f, MXU_N):
        g = jnp.dot(h_scr[...], wgu_ref[:, c0:c0 + MXU_N], preferred_element_type=F32)
        u = jnp.dot(h_scr[...], wgu_ref[:, f + c0:f + c0 + MXU_N], preferred_element_type=F32)
        a_scr[:, c0:c0 + MXU_N] = (g * _sigmoid(g) * u).astype(BF16)
    y = jnp.dot(a_scr[...], wd_ref[...], preferred_element_type=F32)
    o_ref[...] = _gated_residual(x, y, post_ref[...], mod_ref[5:6, :])


def _ffn_layer(x, mod, pre_g, post_g, w_gu, w_down):
    b, s, d = x.shape
    f = w_down.shape[0]
    tm = WIDE_TOKEN_TILE
    tok = pl.BlockSpec((None, tm, d), lambda i, j: (i, j, 0))
    return pl.pallas_call(
        _ffn_body,
        grid=(b, s // tm),
        in_specs=[
            tok,
            pl.BlockSpec((None, 6, d), lambda i, j: (i, 0, 0)),
            _const_spec((1, d)), _const_spec((1, d)),
            _const_spec((d, 2 * f)), _const_spec((f, d)),
        ],
        out_specs=tok,
        out_shape=jax.ShapeDtypeStruct((b, s, d), F32),
        scratch_shapes=[pltpu.VMEM((tm, d), BF16), pltpu.VMEM((tm, f), BF16)],
        compiler_params=_params(2),
        name="swiglu_ffn",
    )(x, mod, pre_g, post_g, w_gu, w_down)


def _head_pair_norm(t, gain2):
    lane = lax.broadcasted_iota(jnp.int32, t.shape, 1)
    sq = t * t
    first = lane < HEAD_DIM
    s0 = jnp.sum(jnp.where(first, sq, 0.0), axis=-1, keepdims=True)
    s1 = jnp.sum(jnp.where(first, 0.0, sq), axis=-1, keepdims=True)
    rs = jnp.where(first, lax.rsqrt(s0 / HEAD_DIM + EPS), lax.rsqrt(s1 / HEAD_DIM + EPS))
    return t * rs * gain2


def _decay_selectors(query_side):
    row = lax.broadcasted_iota(jnp.int32, (LANES, N_HEADS * LANES), 0)
    col = lax.broadcasted_iota(jnp.int32, (LANES, N_HEADS * LANES), 1)
    head, off = col // LANES, col % LANES - HEAD_DIM
    first, second = (off >= 0) & (off < 3), (off >= 3) & (off < 6)
    piece_cols, piece_idx = (first, off) if query_side else (second, off - 3)
    one_cols = second if query_side else first
    is_piece = piece_cols & (row == piece_idx * N_HEADS + head)
    is_one = one_cols & (row == 3 * N_HEADS)
    sel = jnp.where(is_piece, 1.0 if query_side else -1.0, jnp.where(is_one, 1.0, 0.0))
    return sel.astype(BF16)


def _with_tail(t, tail):
    lane = lax.broadcasted_iota(jnp.int32, t.shape, 1)
    return jnp.where(lane < HEAD_DIM, t, tail)


def _kv_body(x_ref, mod_ref, g_ref, wk_ref, wv_ref, wf_ref, bf_ref, kg_ref, sel_ref,
             k_out, vt_out, d_out, h_scr, carry_scr):
    tm, d = x_ref.shape

    @pl.when(pl.program_id(1) == 0)
    def _():
        carry_scr[...] = jnp.zeros_like(carry_scr)

    h_scr[...] = _prenorm(x_ref[...], g_ref[...], mod_ref[0:1, :], mod_ref[1:2, :]).astype(BF16)

    lane = lax.broadcasted_iota(jnp.int32, (tm, LANES), 1)
    fl = jnp.dot(h_scr[...], wf_ref[...], preferred_element_type=F32) + bf_ref[...]
    z = -fl
    ls = -(jnp.maximum(z, 0.0) + jnp.log1p(jnp.exp(-jnp.abs(z))))
    hi, mid, lo = _split3(ls)
    pieces = jnp.where(lane < N_HEADS, hi, jnp.where(lane < 2 * N_HEADS, mid, jnp.where(
        lane < 3 * N_HEADS, lo, 0.0))).astype(BF16)
    r = lax.broadcasted_iota(jnp.int32, (tm, tm), 0)
    c = lax.broadcasted_iota(jnp.int32, (tm, tm), 1)
    tri = jnp.where(c <= r, 1.0, 0.0).astype(BF16)
    cs = jnp.dot(tri, pieces, preferred_element_type=F32)
    tot = cs + pltpu.roll(cs, LANES - N_HEADS, axis=1) + pltpu.roll(cs, LANES - 2 * N_HEADS, axis=1)
    dc = jnp.where(lane < N_HEADS, tot + carry_scr[...], 0.0)
    carry_scr[...] = dc[tm - 1:tm, :]
    hi, mid, lo = _split3(dc * LOG2E)
    d3 = (hi + pltpu.roll(mid, N_HEADS, axis=1) + pltpu.roll(lo, 2 * N_HEADS, axis=1)
          + jnp.where(lane == 3 * N_HEADS, 1.0, 0.0)).astype(BF16)
    d_out[...] = d3

    kg2 = kg_ref[...]
    ones_rows = jnp.where(
        lax.broadcasted_iota(jnp.int32, (V_ROWS - HEAD_DIM, tm), 0) == 0, 1.0, 0.0)
    heads_per_chunk = MXU_N // HEAD_DIM
    for c4 in range(d // MXU_N):
        kc =jnp.dot(h_scr[...], wk_ref[:, c4 * MXU_N:(c4 + 1) * MXU_N], preferred_element_type=F32)
        vc = jnp.dot(h_scr[...], wv_ref[:, c4 * MXU_N:(c4 + 1) * MXU_N], preferred_element_type=F32)
        tails = jnp.dot(d3, sel_ref[:, c4 * heads_per_chunk * LANES:(c4 + 1) * heads_per_chunk * LANES],
                        preferred_element_type=F32)
        for half in range(2):
            kn = _head_pair_norm(kc[:, half * LANES:(half + 1) * LANES], kg2)
            vt = vc[:, half * LANES:(half + 1) * LANES].T
            for sub, src in ((0, kn), (1, pltpu.roll(kn, HEAD_DIM, axis=1))):
                loc = 2 * half + sub
                hd = c4 * heads_per_chunk + loc
                k_out[hd] = _with_tail(src, tails[:, loc * LANES:(loc + 1) * LANES]).astype(BF16)
                vt_out[hd] = jnp.concatenate(
                    [vt[sub * HEAD_DIM:(sub + 1) * HEAD_DIM, :], ones_rows], axis=0).astype(BF16)


def _kv_project(x, kvmod, norm_g, w_k, w_v, w_f3, b_f3, kg2):
    b, s, d = x.shape
    tm = TOKEN_TILE
    return pl.pallas_call(
        _kv_body,
        grid=(b, s // tm),
        in_specs=[
            pl.BlockSpec((None, tm, d), lambda i, j: (i, j, 0)),
            pl.BlockSpec((None, 2, d), lambda i, j: (i, 0, 0)),
            _const_spec((1, d)),
            _const_spec((d, d)), _const_spec((d, d)), _const_spec((d, LANES)),
            _const_spec((1, LANES)), _const_spec((1, LANES)), _const_spec((LANES, N_HEADS * LANES)),
        ],
        out_specs=[
            pl.BlockSpec((None, N_HEADS, tm, LANES), lambda i, j: (i, 0, j, 0)),
            pl.BlockSpec((None, N_HEADS, None, V_ROWS, tm), lambda i, j: (i, 0, j, 0, 0)),
            pl.BlockSpec((None, tm, LANES), lambda i, j: (i, j, 0)),
        ],
        out_shape=[
            jax.ShapeDtypeStruct((b, N_HEADS, s, LANES), BF16),
            jax.ShapeDtypeStruct((b, N_HEADS, s // tm, V_ROWS, tm), BF16),
            jax.ShapeDtypeStruct((b, s, LANES), BF16),
        ],
        scratch_shapes=[pltpu.VMEM((tm, d), BF16), pltpu.VMEM((1, LANES), F32)],
        compiler_params=_params(2),
        name="kv_project",
    )(x, kvmod, norm_g, w_k, w_v, w_f3, b_f3, kg2, _decay_selectors(query_side=False))


def _q_body(x_ref, mod_ref, pre_ref, wq_ref, qg_ref, d_ref, sel_ref, q_out, h_scr):
    tm, d = x_ref.shape
    h_scr[...] = _prenorm(x_ref[...], pre_ref[...], mod_ref[0:1, :], mod_ref[1:2, :]).astype(BF16)
    d3 = d_ref[...]
    qg2 = qg_ref[...]
    heads_per_chunk = MXU_N // HEAD_DIM
    for c4 in range(d // MXU_N):
        qc = jnp.dot(h_scr[...], wq_ref[:, c4 * MXU_N:(c4 + 1) * MXU_N], preferred_element_type=F32)
        tails = jnp.dot(d3, sel_ref[:, c4 * heads_per_chunk * LANES:(c4 + 1) * heads_per_chunk * LANES],
                        preferred_element_type=F32)
        for half in range(2):
            qn = _head_pair_norm(qc[:, half * LANES:(half + 1) * LANES], qg2)
            for sub, src in ((0, qn), (1, pltpu.roll(qn, HEAD_DIM, axis=1))):
                loc = 2 * half + sub
                qa = _with_tail(src, tails[:, loc * LANES:(loc + 1) * LANES])
                q_out[c4 * heads_per_chunk + loc] = qa.astype(BF16)


def _q_project(x, mod, pre_g, w_q, qg2, d3):
    b, s, d = x.shape
    tm = TOKEN_TILE
    return pl.pallas_call(
        _q_body,
        grid=(b, s // tm),
        in_specs=[
            pl.BlockSpec((None, tm, d), lambda i, j: (i, j, 0)),
            pl.BlockSpec((None, 6, d), lambda i, j: (i, 0, 0)),
            _const_spec((1, d)), _const_spec((d, d)), _const_spec((1, LANES)),
            pl.BlockSpec((None, tm, LANES), lambda i, j: (i, j, 0)),
            _const_spec((LANES, N_HEADS * LANES)),
        ],
        out_specs=pl.BlockSpec((None, N_HEADS, tm, LANES), lambda i, j: (i, 0, j, 0)),
        out_shape=jax.ShapeDtypeStruct((b, N_HEADS, s, LANES), BF16),
        scratch_shapes=[pltpu.VMEM((tm, d), BF16)],
        compiler_params=_params(2),
        name="q_project",
    )(x, mod, pre_g, w_q, qg2, d3, _decay_selectors(query_side=True))


def _attn_body(q_ref, k_ref, vt_ref, o_ref, acc_scr, *s_scr):
    n_h = q_ref.shape[0]
    tk = vt_ref.shape[3]
    tq = 2 * tk
    nq = q_ref.shape[1] // tq
    heads = range(n_h)
    halves = (slice(0, tk), slice(tk, tq))
    kpos = lax.broadcasted_iota(jnp.int32, (tk, tk), 0)
    qpos = lax.broadcasted_iota(jnp.int32, (tk, tk), 1)
    tri = kpos <= qpos
    LO, HI = 0, 1

    def scores(qt, j, slot, g, valid):
        cms = []
        for h in heads:
            q = q_ref[h, pl.ds(pl.multiple_of(qt * tq + g * tk, tk), tk), :]
            k = k_ref[h, pl.ds(pl.multiple_of(j * tk, tk), tk), :]
            s = lax.dot_general(k, q, (((1,), (1,)), ((), ())), preferred_element_type=F32)
            if valid is not None:
                s = jnp.where(valid, s, -jnp.inf)
            s_scr[2 * h + slot][:, halves[g]] = s
            cms.append(jnp.max(s, axis=0, keepdims=True))
        return tuple(cms)

    def accumulate(j, slot, g, cmax, ml):
        out = []
        for h in heads:
            m_new = jnp.maximum(ml[h], cmax[h])
            alpha = jnp.exp2(ml[h] - m_new)
            p = jnp.exp2(s_scr[2 * h + slot][:, halves[g]] - m_new)
            pv = jnp.dot(vt_ref[h, j], p.astype(BF16), preferred_element_type=F32)
            acc_scr[h, :, halves[g]] = alpha * acc_scr[h, :, halves[g]] + pv
            out.append(m_new)
        return tuple(out)

    def pair(qi, p, carry, diag):
        (ml_lo, ml_hi), (c0_lo, c0_hi) = carry
        c1_lo = scores(qi, 2 * p + 1, 1, LO, None)
        ml_lo = accumulate(2 * p, 0, LO, c0_lo, ml_lo)
        c1_hi = scores(qi, 2 * p + 1, 1, HI, None)
        ml_hi = accumulate(2 * p, 0, HI, c0_hi, ml_hi)
        c0_lo = scores(qi, 2 * p + 2, 0, LO, tri if diag else None)
        ml_lo = accumulate(2 * p + 1, 1, LO, c1_lo, ml_lo)
        c0_hi = scores(qi, 2 * p + 2, 0, HI, None)
        ml_hi = accumulate(2 * p + 1, 1, HI, c1_hi, ml_hi)
        return (ml_lo, ml_hi), (c0_lo, c0_hi)

    def query_tile(qi, cm0):
        acc_scr[...] = jnp.zeros_like(acc_scr)
        fresh = tuple(jnp.full((1, tk), -jnp.inf, F32) for _ in heads)
        n_loop = jnp.maximum(qi - 1, 0)

        def two_pairs(t, c):
            return pair(qi, 2 * t + 1, pair(qi, 2 * t, c, False), False)

        carry = lax.fori_loop(0, n_loop // 2, two_pairs, ((fresh, fresh), cm0))

        def finish(carry, n_front):
            if n_front == 2:
                carry = pair(qi, qi - 2, carry, False)
            if n_front >= 1:
                carry = pair(qi, qi - 1, carry, True)
            (ml_lo, ml_hi), (c0_lo, c0_hi) = carry
            ml_lo = accumulate(2 * qi, 0, LO, c0_lo, ml_lo)
            c1_hi = scores(qi, 2 * qi + 1, 1, HI, tri)
            ml_hi = accumulate(2 * qi, 0, HI, c0_hi, ml_hi)
            nxt = jnp.minimum(qi + 1, nq - 1)
            n_lo = scores(nxt, 0, 0, LO, None)
            accumulate(2 * qi + 1, 1, HI, c1_hi, ml_hi)
            n_hi = scores(nxt, 0, 0, HI, None)
            for g in (LO, HI):
                o_t = jnp.concatenate(
                    [acc_scr[h, 0:HEAD_DIM, halves[g]] / acc_scr[h, HEAD_DIM:HEAD_DIM + 1, halves[g]]
                     for h in heads], axis=0)
                o_ref[pl.ds(pl.multiple_of((2 * qi + g) * tk, tk), tk), :] = o_t.T
            return n_lo, n_hi

        n_front = jnp.minimum(qi, 1) + (n_loop & 1)
        return lax.switch(n_front, [functools.partial(finish, n_front=n) for n in range(3)], carry)

    lax.fori_loop(0, nq, query_tile, (scores(0, 0, 0, LO, tri), scores(0, 0, 0, HI, None)))


def _fox_attention(q_aug, k_aug, vt_aug):
    b, h, s, _ = k_aug.shape
    tk = TOKEN_TILE
    nk = s // tk
    hp = ATTN_HEADS_PER_STEP
    return pl.pallas_call(
        _attn_body,
        grid=(b, h // hp),
        in_specs=[
            pl.BlockSpec((None, hp, s, LANES), lambda i, j: (i, j, 0, 0)),
            pl.BlockSpec((None, hp, s, LANES), lambda i, j: (i, j, 0, 0)),
            pl.BlockSpec((None, hp, nk, V_ROWS, tk), lambda i, j: (i, j, 0, 0, 0)),
        ],
        out_specs=pl.BlockSpec((None, s, hp * HEAD_DIM), lambda i, j: (i, 0, j)),
        out_shape=jax.ShapeDtypeStruct((b, s, h * HEAD_DIM), F32),
        scratch_shapes=[pltpu.VMEM((hp, V_ROWS, ATTN_Q_TILE), F32)]
        + [pltpu.VMEM((tk, ATTN_Q_TILE), F32)] * (2 * hp),
        compiler_params=_params(2),
        name="fox_attention",
    )(q_aug, k_aug, vt_aug)


def _out_body(x_ref, mod_ref, pre_ref, post_ref, a_ref, wg_ref, wo_ref, o_ref):
    x = x_ref[...]
    h = _prenorm(x, pre_ref[...], mod_ref[0:1, :], mod_ref[1:2, :]).astype(BF16)
    gate = _sigmoid(jnp.dot(h, wg_ref[...], preferred_element_type=F32))
    y = jnp.dot((a_ref[...] * gate).astype(BF16), wo_ref[...], preferred_element_type=F32)
    o_ref[...] = _gated_residual(x, y, post_ref[...], mod_ref[2:3, :])


def _fox_output(x, mod, pre_g, post_g, o_t, w_g, w_o):
    b, s, d = x.shape
    tm = WIDE_TOKEN_TILE
    tok = pl.BlockSpec((None, tm, d), lambda i, j: (i, j, 0))
    return pl.pallas_call(
        _out_body,
        grid=(b, s // tm),
        in_specs=[
            tok,
            pl.BlockSpec((None, 6, d), lambda i, j: (i, 0, 0)),
            _const_spec((1, d)), _const_spec((1, d)),
            tok,
            _const_spec((d, d)), _const_spec((d, d)),
        ],
        out_specs=tok,
        out_shape=jax.ShapeDtypeStruct((b, s, d), F32),
        compiler_params=_params(2),
        name="fox_output",
    )(x, mod, pre_g, post_g, o_t, w_g, w_o)


def kernel(x, c, ada_w, ada_b, pre_mix_g, post_mix_g, pre_ffn_g, post_ffn_g, ffn_w_gu, ffn_w_down,
           a_w_in, a_b_in, a_ln_g, a_ln_b, a_w_s, a_b_s, a_w_out, kv_ada_w, kv_ada_b, kv_norm_g,
           kv_w, kv_b_f, k_norm_g, b_w_qg, b_q_norm_g, b_w_o):
    b, s, d = x.shape
    depth = ada_w.shape[0]
    n_a = a_w_in.shape[0]
    assert d == N_HEADS * HEAD_DIM and s % ATTN_Q_TILE == 0

    c8 = jnp.pad(c, ((0, 8 - b), (0, 0)))
    mods = _modulation(c8, ada_w, ada_b)[:, :b].reshape(depth, b, 6, d)
    kvmod = _modulation(c8, kv_ada_w[None], kv_ada_b[None])[0, :b].reshape(b, 2, d)

    row = lambda v: v.reshape(1, -1)
    q_scale = LOG2E * HEAD_DIM ** -0.5

    for layer in range(depth):
        mod = mods[layer]
        if layer < n_a:
            i = layer
            bs_full = jnp.repeat(a_b_s[i].T, CHUNK, axis=1)
            x = _gmlp_layer(x, mod, row(pre_mix_g[layer]), row(post_mix_g[layer]),
                            _lane_padded(a_w_in[i]), row(a_b_in[i]), row(a_ln_g[i]), row(a_ln_b[i]),
                            a_w_s[i], bs_full, _lane_padded(a_w_out[i]))
        else:
            j = layer - n_a
            qg2 = row(jnp.tile(b_q_norm_g[j] * q_scale, 2))
            q_aug = _q_project(x, mod, row(pre_mix_g[layer]), b_w_qg[j][:, :d].astype(BF16), qg2, d3)
            o_t = _fox_attention(q_aug, k_aug, vt_aug)
            x = _fox_output(x, mod, row(pre_mix_g[layer]), row(post_mix_g[layer]), o_t,
                            b_w_qg[j][:, d:].astype(BF16), b_w_o[j].astype(BF16))
        x = _ffn_layer(x, mod, row(pre_ffn_g[layer]), row(post_ffn_g[layer]),
                       ffn_w_gu[layer].astype(BF16), ffn_w_down[layer].astype(BF16))
        if layer == n_a - 1:
            w_f = kv_w[:, 2 * d:]
            pad = jnp.zeros((d, LANES - 3 * N_HEADS), F32)
            w_f3 = jnp.concatenate([w_f, w_f, w_f, pad], axis=1).astype(BF16)
            b_f3 = row(jnp.concatenate([kv_b_f, kv_b_f, kv_b_f, jnp.zeros((LANES - 3 * N_HEADS,), F32)]))
            k_aug, vt_aug, d3 = _kv_project(
                x, kvmod, row(kv_norm_g), kv_w[:, :d].astype(BF16), kv_w[:, d:2 * d].astype(BF16),
                w_f3, b_f3, row(jnp.tile(k_norm_g, 2)))
    return x
```

```python
import functools
import math

import jax
import jax.numpy as jnp
from jax import lax
from jax.experimental import pallas as pl
from jax.experimental.pallas import tpu as pltpu

F32 = jnp.float32
BF16 = jnp.bfloat16

EPS = 1e-6
N_HEADS = 16
HEAD_DIM = 64
CHUNK = 128
GROUPS = 16
LOG2E = 1.4426950408889634
LANES = 128
MXU_N = 256
TOKEN_TILE = 512
WIDE_TOKEN_TILE = 2 * TOKEN_TILE
ATTN_Q_TILE = 2 * TOKEN_TILE
ATTN_HEADS_PER_STEP = 2
BF16_SUBLANES = 16
V_ROWS = HEAD_DIM + BF16_SUBLANES
V7X_COMPILER_VMEM_RESERVE = 8 * 1024 * 1024


def _sigmoid(x):
    return 1.0 / (1.0 + jnp.exp(-x))


def _unit_rms(x):
    return x * lax.rsqrt(jnp.mean(x * x, axis=-1, keepdims=True) + EPS)


def _prenorm(x, g, shift, scale):
    return _unit_rms(x) * (g * (1.0 + scale)) + shift


def _gated_residual(x, y, g, gate):
    return x + _unit_rms(y) * (g * gate)


def _split3(x):
    hi = x.astype(BF16).astype(F32)
    r = x - hi
    mid = r.astype(BF16).astype(F32)
    lo = (r - mid).astype(BF16).astype(F32)
    return hi, mid, lo


def _const_spec(shape):
    n = len(shape)
    return pl.BlockSpec(shape, lambda *_: (0,) * n, pipeline_mode=pl.Buffered(1))


def _lane_padded(w):
    return jnp.pad(w.astype(BF16), ((0, 0), (0, LANES)))


def _block_bytes(spec, dtype):
    buffers = 2 if spec.pipeline_mode is None else spec.pipeline_mode.buffer_count
    return math.prod(1 if n is None else n for n in spec.block_shape) * jnp.dtype(dtype).itemsize * buffers


def _pallas(body, *, grid, in_specs, out_specs, out_shape, scratch_shapes=(), name):
    multi = isinstance(out_shape, (list, tuple))
    outs = zip(out_specs, out_shape) if multi else [(out_specs, out_shape)]
    need = sum(_block_bytes(s, o.dtype) for s, o in outs)
    need += sum(math.prod(m.shape) * jnp.dtype(m.dtype).itemsize for m in scratch_shapes)

    def call(*inputs):
        windows = sum(_block_bytes(s, a.dtype) for s, a in zip(in_specs, inputs, strict=True))
        return pl.pallas_call(
            body, grid=grid, in_specs=in_specs, out_specs=out_specs, out_shape=out_shape,
            scratch_shapes=list(scratch_shapes),
            compiler_params=pltpu.CompilerParams(
                dimension_semantics=("arbitrary",) * len(grid),
                vmem_limit_bytes=need + windows + V7X_COMPILER_VMEM_RESERVE),
            name=name)(*inputs)

    return call


def _mod_body(c_ref, w_ref, b_ref, o_ref):
    c = c_ref[...]
    ca = (c * _sigmoid(c)).astype(BF16)
    o_ref[...] = jnp.dot(ca, w_ref[...].astype(BF16), preferred_element_type=F32) + b_ref[...]


def _modulation(c8, w, b):
    n_l, d, n = w.shape
    nb = 1024
    return _pallas(
        _mod_body,
        grid=(n_l, n // nb),
        in_specs=[
            pl.BlockSpec((8, d), lambda l, j: (0, 0)),
            pl.BlockSpec((None, d, nb), lambda l, j: (l, 0, j)),
            pl.BlockSpec((None, 1, nb), lambda l, j: (l, 0, j)),
        ],
        out_specs=pl.BlockSpec((None, 8, nb), lambda l, j: (l, 0, j)),
        out_shape=jax.ShapeDtypeStruct((n_l, 8, n), F32),
        name="modulation",
    )(c8, w, b.reshape(n_l, 1, n))


def _gmlp_body(x_ref, mod_ref, pre_ref, post_ref, win_ref, bin_ref, lng_ref, lnb_ref, ws_ref,
               bs_ref, wout_ref, o_ref, h_scr, u_scr, v_scr, vn_scr, y_scr):
    tm, d = x_ref.shape
    gw = u_scr.shape[1]
    x = x_ref[...]
    h_scr[...] = _prenorm(x, pre_ref[...], mod_ref[0:1, :], mod_ref[1:2, :]).astype(BF16)

    k_gelu = 0.7978845608028654
    nc = 512
    for c0 in range(0, 2 * gw, nc):
        z = jnp.dot(h_scr[...], win_ref[:, c0:c0 + nc], preferred_element_type=F32)
        z = z + bin_ref[:, c0:c0 + nc]
        z = z * (0.5 * (1.0 + jnp.tanh(k_gelu * (z + 0.044715 * (z * z * z)))))
        if c0 < gw:
            u_scr[:, c0:c0 + nc] = z
        else:
            v_scr[:, c0 - gw:c0 - gw + nc] = z

    v = v_scr[...]
    mu = jnp.mean(v, axis=-1, keepdims=True)
    vc = v - mu
    rstd = lax.rsqrt(jnp.mean(vc * vc, axis=-1, keepdims=True) + EPS)
    vn_scr[...] = (vc * rstd * lng_ref[...] + lnb_ref[...]).astype(BF16)

    n_chunks = tm // CHUNK
    row = lax.broadcasted_iota(jnp.int32, (CHUNK, CHUNK), 0)
    col = lax.broadcasted_iota(jnp.int32, (CHUNK, CHUNK), 1)
    causal = col <= row
    for g in range(GROUPS):
        gs = slice(g * CHUNK, (g + 1) * CHUNK)
        ws = jnp.where(causal, ws_ref[g], 0.0).astype(BF16)
        rhs = jnp.concatenate(
            [vn_scr[c * CHUNK:(c + 1) * CHUNK, gs] for c in range(n_chunks)], axis=1)
        sp = jnp.dot(ws, rhs, preferred_element_type=F32)
        for c in range(n_chunks):
            cs = slice(c * CHUNK, (c + 1) * CHUNK)
            y_scr[cs, gs] = (u_scr[cs, gs] * (sp[:, cs] + bs_ref[:, gs])).astype(BF16)

    y = jnp.dot(y_scr[...], wout_ref[:, 0:d], preferred_element_type=F32)
    o_ref[...] = _gated_residual(x, y, post_ref[...], mod_ref[2:3, :])


def _gmlp_layer(x, mod, pre_g, post_g, w_in, b_in, ln_g, ln_b, w_s, bs_full, w_out):
    b, s, d = x.shape
    gw = w_out.shape[0]
    tm = TOKEN_TILE
    tok = pl.BlockSpec((None, tm, d), lambda i, j: (i, j, 0))
    return _pallas(
        _gmlp_body,
        grid=(b, s // tm),
        in_specs=[
            tok,
            pl.BlockSpec((None, 6, d), lambda i, j: (i, 0, 0)),
            _const_spec((1, d)), _const_spec((1, d)),
            _const_spec(w_in.shape), _const_spec((1, 2 * gw)),
            _const_spec((1, gw)), _const_spec((1, gw)),
            _const_spec((GROUPS, CHUNK, CHUNK)), _const_spec((CHUNK, gw)),
            _const_spec(w_out.shape),
        ],
        out_specs=tok,
        out_shape=jax.ShapeDtypeStruct((b, s, d), F32),
        scratch_shapes=[
            pltpu.VMEM((tm, d), BF16), pltpu.VMEM((tm, gw), F32), pltpu.VMEM((tm, gw), F32),
            pltpu.VMEM((tm, gw), BF16), pltpu.VMEM((tm, gw), BF16),
        ],
        name="gmlp_mixer",
    )(x, mod, pre_g, post_g, w_in, b_in, ln_g, ln_b, w_s, bs_full, w_out)


def _ffn_body(x_ref, mod_ref, pre_ref, post_ref, wgu_ref, wd_ref, o_ref, h_scr, a_scr):
    f = a_scr.shape[1]
    x = x_ref[...]
    h_scr[...] = _prenorm(x, pre_ref[...], mod_ref[3:4, :], mod_ref[4:5, :]).astype(BF16)
    for c0 in range(0, f, MXU_N):
        g = jnp.dot(h_scr[...], wgu_ref[:, c0:c0 + MXU_N], preferred_element_type=F32)
        u = jnp.dot(h_scr[...], wgu_ref[:, f + c0:f + c0 + MXU_N], preferred_element_type=F32)
        a_scr[:, c0:c0 + MXU_N] = (g * _sigmoid(g) * u).astype(BF16)
    y = jnp.dot(a_scr[...], wd_ref[...], preferred_element_type=F32)
    o_ref[...] = _gated_residual(x, y, post_ref[...], mod_ref[5:6, :])


def _ffn_layer(x, mod, pre_g, post_g, w_gu, w_down):
    b, s, d = x.shape
    f = w_down.shape[0]
    tm = WIDE_TOKEN_TILE
    tok = pl.BlockSpec((None, tm, d), lambda i, j: (i, j, 0))
    return _pallas(
        _ffn_body,
        grid=(b, s // tm),
        in_specs=[
            tok,
            pl.BlockSpec((None, 6, d), lambda i, j: (i, 0, 0)),
            _const_spec((1, d)), _const_spec((1, d)),
            _const_spec((d, 2 * f)), _const_spec((f, d)),
        ],
        out_specs=tok,
        out_shape=jax.ShapeDtypeStruct((b, s, d), F32),
        scratch_shapes=[pltpu.VMEM((tm, d), BF16), pltpu.VMEM((tm, f), BF16)],
        name="swiglu_ffn",
    )(x, mod, pre_g, post_g, w_gu, w_down)


def _head_pair_norm(t, gain2):
    lane = lax.broadcasted_iota(jnp.int32, t.shape, 1)
    sq = t * t
    first = lane < HEAD_DIM
    s0 = jnp.sum(jnp.where(first, sq, 0.0), axis=-1, keepdims=True)
    s1 = jnp.sum(jnp.where(first, 0.0, sq), axis=-1, keepdims=True)
    rs = jnp.where(first, lax.rsqrt(s0 / HEAD_DIM + EPS), lax.rsqrt(s1 / HEAD_DIM + EPS))
    return t * rs * gain2


def _decay_selectors(query_side):
    row = lax.broadcasted_iota(jnp.int32, (LANES, N_HEADS * LANES), 0)
    col = lax.broadcasted_iota(jnp.int32, (LANES, N_HEADS * LANES), 1)
    head, off = col // LANES, col % LANES - HEAD_DIM
    first, second = (off >= 0) & (off < 3), (off >= 3) & (off < 6)
    piece_cols, piece_idx = (first, off) if query_side else (second, off - 3)
    one_cols = second if query_side else first
    is_piece = piece_cols & (row == piece_idx * N_HEADS + head)
    is_one = one_cols & (row == 3 * N_HEADS)
    sel = jnp.where(is_piece, 1.0 if query_side else -1.0, jnp.where(is_one, 1.0, 0.0))
    return sel.astype(BF16)


def _with_tail(t, tail):
    lane = lax.broadcasted_iota(jnp.int32, t.shape, 1)
    return jnp.where(lane < HEAD_DIM, t, tail)


def _kv_body(x_ref, mod_ref, g_ref, wk_ref, wv_ref, wf_ref, bf_ref, kg_ref, sel_ref,
             k_out, vt_out, d_out, h_scr, carry_scr):
    tm, d = x_ref.shape

    @pl.when(pl.program_id(1) == 0)
    def _():
        carry_scr[...] = jnp.zeros_like(carry_scr)

    h_scr[...] = _prenorm(x_ref[...], g_ref[...], mod_ref[0:1, :], mod_ref[1:2, :]).astype(BF16)

    lane = lax.broadcasted_iota(jnp.int32, (tm, LANES), 1)
    fl = jnp.dot(h_scr[...], wf_ref[...], preferred_element_type=F32) + bf_ref[...]
    z = -fl
    ls = -(jnp.maximum(z, 0.0) + jnp.log1p(jnp.exp(-jnp.abs(z))))
    hi, mid, lo = _split3(ls)
    pieces = jnp.where(lane < N_HEADS, hi, jnp.where(lane < 2 * N_HEADS, mid, jnp.where(
        lane < 3 * N_HEADS, lo, 0.0))).astype(BF16)
    r = lax.broadcasted_iota(jnp.int32, (tm, tm), 0)
    c = lax.broadcasted_iota(jnp.int32, (tm, tm), 1)
    tri = jnp.where(c <= r, 1.0, 0.0).astype(BF16)
    cs = jnp.dot(tri, pieces, preferred_element_type=F32)
    tot = cs + pltpu.roll(cs, LANES - N_HEADS, axis=1) + pltpu.roll(cs, LANES - 2 * N_HEADS, axis=1)
    dc = jnp.where(lane < N_HEADS, tot + carry_scr[...], 0.0)
    carry_scr[...] = dc[tm - 1:tm, :]
    hi, mid, lo = _split3(dc * LOG2E)
    d3 = (hi + pltpu.roll(mid, N_HEADS, axis=1) + pltpu.roll(lo, 2 * N_HEADS, axis=1)
          + jnp.where(lane == 3 * N_HEADS, 1.0, 0.0)).astype(BF16)
    d_out[...] = d3

    kg2 = kg_ref[...]
    ones_rows = jnp.where(
        lax.broadcasted_iota(jnp.int32, (V_ROWS - HEAD_DIM, tm), 0) == 0, 1.0, 0.0)
    heads_per_chunk = MXU_N // HEAD_DIM
    for c4 in range(d // MXU_N):
        kc =jnp.dot(h_scr[...], wk_ref[:, c4 * MXU_N:(c4 + 1) * MXU_N], preferred_element_type=F32)
        vc = jnp.dot(h_scr[...], wv_ref[:, c4 * MXU_N:(c4 + 1) * MXU_N], preferred_element_type=F32)
        tails = jnp.dot(d3, sel_ref[:, c4 * heads_per_chunk * LANES:(c4 + 1) * heads_per_chunk * LANES],
                        preferred_element_type=F32)
        for half in range(2):
            kn = _head_pair_norm(kc[:, half * LANES:(half + 1) * LANES], kg2)
            vt = vc[:, half * LANES:(half + 1) * LANES].T
            for sub, src in ((0, kn), (1, pltpu.roll(kn, HEAD_DIM, axis=1))):
                loc = 2 * half + sub
                hd = c4 * heads_per_chunk + loc
                k_out[hd] = _with_tail(src, tails[:, loc * LANES:(loc + 1) * LANES]).astype(BF16)
                vt_out[hd] = jnp.concatenate(
                    [vt[sub * HEAD_DIM:(sub + 1) * HEAD_DIM, :], ones_rows], axis=0).astype(BF16)


def _kv_project(x, kvmod, norm_g, w_k, w_v, w_f3, b_f3, kg2):
    b, s, d = x.shape
    tm = TOKEN_TILE
    return _pallas(
        _kv_body,
        grid=(b, s // tm),
        in_specs=[
            pl.BlockSpec((None, tm, d), lambda i, j: (i, j, 0)),
            pl.BlockSpec((None, 2, d), lambda i, j: (i, 0, 0)),
            _const_spec((1, d)),
            _const_spec((d, d)), _const_spec((d, d)), _const_spec((d, LANES)),
            _const_spec((1, LANES)), _const_spec((1, LANES)), _const_spec((LANES, N_HEADS * LANES)),
        ],
        out_specs=[
            pl.BlockSpec((None, N_HEADS, tm, LANES), lambda i, j: (i, 0, j, 0)),
            pl.BlockSpec((None, N_HEADS, None, V_ROWS, tm), lambda i, j: (i, 0, j, 0, 0)),
            pl.BlockSpec((None, tm, LANES), lambda i, j: (i, j, 0)),
        ],
        out_shape=[
            jax.ShapeDtypeStruct((b, N_HEADS, s, LANES), BF16),
            jax.ShapeDtypeStruct((b, N_HEADS, s // tm, V_ROWS, tm), BF16),
            jax.ShapeDtypeStruct((b, s, LANES), BF16),
        ],
        scratch_shapes=[pltpu.VMEM((tm, d), BF16), pltpu.VMEM((1, LANES), F32)],
        name="kv_project",
    )(x, kvmod, norm_g, w_k, w_v, w_f3, b_f3, kg2, _decay_selectors(query_side=False))


def _q_body(x_ref, mod_ref, pre_ref, wq_ref, qg_ref, d_ref, sel_ref, q_out, h_scr):
    tm, d = x_ref.shape
    h_scr[...] = _prenorm(x_ref[...], pre_ref[...], mod_ref[0:1, :], mod_ref[1:2, :]).astype(BF16)
    d3 = d_ref[...]
    qg2 = qg_ref[...]
    heads_per_chunk = MXU_N // HEAD_DIM
    for c4 in range(d // MXU_N):
        qc = jnp.dot(h_scr[...], wq_ref[:, c4 * MXU_N:(c4 + 1) * MXU_N], preferred_element_type=F32)
        tails = jnp.dot(d3, sel_ref[:, c4 * heads_per_chunk * LANES:(c4 + 1) * heads_per_chunk * LANES],
                        preferred_element_type=F32)
        for half in range(2):
            qn = _head_pair_norm(qc[:, half * LANES:(half + 1) * LANES], qg2)
            for sub, src in ((0, qn), (1, pltpu.roll(qn, HEAD_DIM, axis=1))):
                loc = 2 * half + sub
                qa = _with_tail(src, tails[:, loc * LANES:(loc + 1) * LANES])
                q_out[c4 * heads_per_chunk + loc] = qa.astype(BF16)


def _q_project(x, mod, pre_g, w_q, qg2, d3):
    b, s, d = x.shape
    tm = TOKEN_TILE
    return _pallas(
        _q_body,
        grid=(b, s // tm),
        in_specs=[
            pl.BlockSpec((None, tm, d), lambda i, j: (i, j, 0)),
            pl.BlockSpec((None, 6, d), lambda i, j: (i, 0, 0)),
            _const_spec((1, d)), _const_spec((d, d)), _const_spec((1, LANES)),
            pl.BlockSpec((None, tm, LANES), lambda i, j: (i, j, 0)),
            _const_spec((LANES, N_HEADS * LANES)),
        ],
        out_specs=pl.BlockSpec((None, N_HEADS, tm, LANES), lambda i, j: (i, 0, j, 0)),
        out_shape=jax.ShapeDtypeStruct((b, N_HEADS, s, LANES), BF16),
        scratch_shapes=[pltpu.VMEM((tm, d), BF16)],
        name="q_project",
    )(x, mod, pre_g, w_q, qg2, d3, _decay_selectors(query_side=True))


def _attn_body(q_ref, k_ref, vt_ref, o_ref, acc_scr, *s_scr):
    n_h = q_ref.shape[0]
    tk = vt_ref.shape[3]
    tq = 2 * tk
    nq = q_ref.shape[1] // tq
    heads = range(n_h)
    halves = (slice(0, tk), slice(tk, tq))
    kpos = lax.broadcasted_iota(jnp.int32, (tk, tk), 0)
    qpos = lax.broadcasted_iota(jnp.int32, (tk, tk), 1)
    tri = kpos <= qpos
    LO, HI = 0, 1

    def scores(qt, j, slot, g, valid):
        cms = []
        for h in heads:
            q = q_ref[h, pl.ds(pl.multiple_of(qt * tq + g * tk, tk), tk), :]
            k = k_ref[h, pl.ds(pl.multiple_of(j * tk, tk), tk), :]
            s = lax.dot_general(k, q, (((1,), (1,)), ((), ())), preferred_element_type=F32)
            if valid is not None:
                s = jnp.where(valid, s, -jnp.inf)
            s_scr[2 * h + slot][:, halves[g]] = s
            cms.append(jnp.max(s, axis=0, keepdims=True))
        return tuple(cms)

    def accumulate(j, slot, g, cmax, ml):
        out = []
        for h in heads:
            m_new = jnp.maximum(ml[h], cmax[h])
            alpha = jnp.exp2(ml[h] - m_new)
            p = jnp.exp2(s_scr[2 * h + slot][:, halves[g]] - m_new)
            pv = jnp.dot(vt_ref[h, j], p.astype(BF16), preferred_element_type=F32)
            acc_scr[h, :, halves[g]] = alpha * acc_scr[h, :, halves[g]] + pv
            out.append(m_new)
        return tuple(out)

    def pair(qi, p, carry, diag):
        (ml_lo, ml_hi), (c0_lo, c0_hi) = carry
        c1_lo = scores(qi, 2 * p + 1, 1, LO, None)
        ml_lo = accumulate(2 * p, 0, LO, c0_lo, ml_lo)
        c1_hi = scores(qi, 2 * p + 1, 1, HI, None)
        ml_hi = accumulate(2 * p, 0, HI, c0_hi, ml_hi)
        c0_lo = scores(qi, 2 * p + 2, 0, LO, tri if diag else None)
        ml_lo = accumulate(2 * p + 1, 1, LO, c1_lo, ml_lo)
        c0_hi = scores(qi, 2 * p + 2, 0, HI, None)
        ml_hi = accumulate(2 * p + 1, 1, HI, c1_hi, ml_hi)
        return (ml_lo, ml_hi), (c0_lo, c0_hi)

    def query_tile(qi, cm0):
        acc_scr[...] = jnp.zeros_like(acc_scr)
        fresh = tuple(jnp.full((1, tk), -jnp.inf, F32) for _ in heads)
        n_loop = jnp.maximum(qi - 1, 0)

        def two_pairs(t, c):
            return pair(qi, 2 * t + 1, pair(qi, 2 * t, c, False), False)

        carry = lax.fori_loop(0, n_loop // 2, two_pairs, ((fresh, fresh), cm0))

        def finish(carry, n_front):
            if n_front == 2:
                carry = pair(qi, qi - 2, carry, False)
            if n_front >= 1:
                carry = pair(qi, qi - 1, carry, True)
            (ml_lo, ml_hi), (c0_lo, c0_hi) = carry
            ml_lo = accumulate(2 * qi, 0, LO, c0_lo, ml_lo)
            c1_hi = scores(qi, 2 * qi + 1, 1, HI, tri)
            ml_hi = accumulate(2 * qi, 0, HI, c0_hi, ml_hi)
            nxt = jnp.minimum(qi + 1, nq - 1)
            n_lo = scores(nxt, 0, 0, LO, None)
            accumulate(2 * qi + 1, 1, HI, c1_hi, ml_hi)
            n_hi = scores(nxt, 0, 0, HI, None)
            for g in (LO, HI):
                o_t = jnp.concatenate(
                    [acc_scr[h, 0:HEAD_DIM, halves[g]] / acc_scr[h, HEAD_DIM:HEAD_DIM + 1, halves[g]]
                     for h in heads], axis=0)
                o_ref[pl.ds(pl.multiple_of((2 * qi + g) * tk, tk), tk), :] = o_t.T
            return n_lo, n_hi

        n_front = jnp.minimum(qi, 1) + (n_loop & 1)
        return lax.switch(n_front, [functools.partial(finish, n_front=n) for n in range(3)], carry)

    lax.fori_loop(0, nq, query_tile, (scores(0, 0, 0, LO, tri), scores(0, 0, 0, HI, None)))


def _fox_attention(q_aug, k_aug, vt_aug):
    b, h, s, _ = k_aug.shape
    tk = TOKEN_TILE
    nk = s // tk
    hp = ATTN_HEADS_PER_STEP
    return _pallas(
        _attn_body,
        grid=(b, h // hp),
        in_specs=[
            pl.BlockSpec((None, hp, s, LANES), lambda i, j: (i, j, 0, 0)),
            pl.BlockSpec((None, hp, s, LANES), lambda i, j: (i, j, 0, 0)),
            pl.BlockSpec((None, hp, nk, V_ROWS, tk), lambda i, j: (i, j, 0, 0, 0)),
        ],
        out_specs=pl.BlockSpec((None, s, hp * HEAD_DIM), lambda i, j: (i, 0, j)),
        out_shape=jax.ShapeDtypeStruct((b, s, h * HEAD_DIM), F32),
        scratch_shapes=[pltpu.VMEM((hp, V_ROWS, ATTN_Q_TILE), F32)]
        + [pltpu.VMEM((tk, ATTN_Q_TILE), F32)] * (2 * hp),
        name="fox_attention",
    )(q_aug, k_aug, vt_aug)


def _out_body(x_ref, mod_ref, pre_ref, post_ref, a_ref, wg_ref, wo_ref, o_ref):
    x = x_ref[...]
    h = _prenorm(x, pre_ref[...], mod_ref[0:1, :], mod_ref[1:2, :]).astype(BF16)
    gate = _sigmoid(jnp.dot(h, wg_ref[...], preferred_element_type=F32))
    y = jnp.dot((a_ref[...] * gate).astype(BF16), wo_ref[...], preferred_element_type=F32)
    o_ref[...] = _gated_residual(x, y, post_ref[...], mod_ref[2:3, :])


def _fox_output(x, mod, pre_g, post_g, o_t, w_g, w_o):
    b, s, d = x.shape
    tm = WIDE_TOKEN_TILE
    tok = pl.BlockSpec((None, tm, d), lambda i, j: (i, j, 0))
    return _pallas(
        _out_body,
        grid=(b, s // tm),
        in_specs=[
            tok,
            pl.BlockSpec((None, 6, d), lambda i, j: (i, 0, 0)),
            _const_spec((1, d)), _const_spec((1, d)),
            tok,
            _const_spec((d, d)), _const_spec((d, d)),
        ],
        out_specs=tok,
        out_shape=jax.ShapeDtypeStruct((b, s, d), F32),
        name="fox_output",
    )(x, mod, pre_g, post_g, o_t, w_g, w_o)


def kernel(x, c, ada_w, ada_b, pre_mix_g, post_mix_g, pre_ffn_g, post_ffn_g, ffn_w_gu, ffn_w_down,
           a_w_in, a_b_in, a_ln_g, a_ln_b, a_w_s, a_b_s, a_w_out, kv_ada_w, kv_ada_b, kv_norm_g,
           kv_w, kv_b_f, k_norm_g, b_w_qg, b_q_norm_g, b_w_o):
    b, s, d = x.shape
    depth = ada_w.shape[0]
    n_a = a_w_in.shape[0]
    assert d == N_HEADS * HEAD_DIM and s % ATTN_Q_TILE == 0

    c8 = jnp.pad(c, ((0, 8 - b), (0, 0)))
    mods = _modulation(c8, ada_w, ada_b)[:, :b].reshape(depth, b, 6, d)
    kvmod = _modulation(c8, kv_ada_w[None], kv_ada_b[None])[0, :b].reshape(b, 2, d)

    row = lambda v: v.reshape(1, -1)
    q_scale = LOG2E * HEAD_DIM ** -0.5

    for layer in range(depth):
        mod = mods[layer]
        if layer < n_a:
            i = layer
            bs_full = jnp.repeat(a_b_s[i].T, CHUNK, axis=1)
            x = _gmlp_layer(x, mod, row(pre_mix_g[layer]), row(post_mix_g[layer]),
                            _lane_padded(a_w_in[i]), row(a_b_in[i]), row(a_ln_g[i]), row(a_ln_b[i]),
                            a_w_s[i], bs_full, _lane_padded(a_w_out[i]))
        else:
            j = layer - n_a
            qg2 = row(jnp.tile(b_q_norm_g[j] * q_scale, 2))
            q_aug = _q_project(x, mod, row(pre_mix_g[layer]), b_w_qg[j][:, :d].astype(BF16), qg2, d3)
            o_t = _fox_attention(q_aug, k_aug, vt_aug)
            x = _fox_output(x, mod, row(pre_mix_g[layer]), row(post_mix_g[layer]), o_t,
                            b_w_qg[j][:, d:].astype(BF16), b_w_o[j].astype(BF16))
        x = _ffn_layer(x, mod, row(pre_ffn_g[layer]), row(post_ffn_g[layer]),
                       ffn_w_gu[layer].astype(BF16), ffn_w_down[layer].astype(BF16))
        if layer == n_a - 1:
            w_f = kv_w[:, 2 * d:]
            pad = jnp.zeros((d, LANES - 3 * N_HEADS), F32)
            w_f3 = jnp.concatenate([w_f, w_f, w_f, pad], axis=1).astype(BF16)
            b_f3 = row(jnp.concatenate([kv_b_f, kv_b_f, kv_b_f, jnp.zeros((LANES - 3 * N_HEADS,), F32)]))
            k_aug, vt_aug, d3 = _kv_project(
                x, kvmod, row(kv_norm_g), kv_w[:, :d].astype(BF16), kv_w[:, d:2 * d].astype(BF16),
                w_f3, b_f3, row(jnp.tile(k_norm_g, 2)))
    return x
```

```python
import functools
import math

import jax
import jax.numpy as jnp
from jax import lax
from jax.experimental import pallas as pl
from jax.experimental.pallas import tpu as pltpu

F32 = jnp.float32
BF16 = jnp.bfloat16

EPS = 1e-6
N_HEADS = 16
HEAD_DIM = 64
CHUNK = 128
GROUPS = 16
LOG2E = 1.4426950408889634
LANES = 128
MXU_N = 256
TOKEN_TILE = 512
WIDE_TOKEN_TILE = 2 * TOKEN_TILE
ATTN_Q_TILE = 2 * TOKEN_TILE
ATTN_HEADS_PER_STEP = 2
BF16_SUBLANES = 16
V_ROWS = HEAD_DIM + BF16_SUBLANES
V7X_COMPILER_VMEM_RESERVE = 6 * 1024 * 1024


def _sigmoid(x):
    return 1.0 / (1.0 + jnp.exp(-x))


def _unit_rms(x):
    return x * lax.rsqrt(jnp.mean(x * x, axis=-1, keepdims=True) + EPS)


def _prenorm(x, g, shift, scale):
    return _unit_rms(x) * (g * (1.0 + scale)) + shift


def _gated_residual(x, y, g, gate):
    return x + _unit_rms(y) * (g * gate)


def _split3(x):
    hi = x.astype(BF16).astype(F32)
    r = x - hi
    mid = r.astype(BF16).astype(F32)
    lo = (r - mid).astype(BF16).astype(F32)
    return hi, mid, lo


def _const_spec(shape):
    n = len(shape)
    return pl.BlockSpec(shape, lambda *_: (0,) * n, pipeline_mode=pl.Buffered(1))


def _slab_spec(block, index):
    return pl.BlockSpec(block, lambda *_: index, pipeline_mode=pl.Buffered(1))


def _lane_padded(w):
    return jnp.pad(w.astype(BF16), ((0, 0),) * (w.ndim - 1) + ((0, LANES),))


def _block_bytes(spec, dtype):
    buffers = 2 if spec.pipeline_mode is None else spec.pipeline_mode.buffer_count
    return math.prod(1 if n is None else n for n in spec.block_shape) * jnp.dtype(dtype).itemsize * buffers


def _pallas(body, *, grid, in_specs, out_specs, out_shape, scratch_shapes=(), name):
    multi = isinstance(out_shape, (list, tuple))
    outs = zip(out_specs, out_shape) if multi else [(out_specs, out_shape)]
    need = sum(_block_bytes(s, o.dtype) for s, o in outs)
    need += sum(math.prod(m.shape) * jnp.dtype(m.dtype).itemsize for m in scratch_shapes)

    def call(*inputs):
        windows = sum(_block_bytes(s, a.dtype) for s, a in zip(in_specs, inputs, strict=True))
        return pl.pallas_call(
            body, grid=grid, in_specs=in_specs, out_specs=out_specs, out_shape=out_shape,
            scratch_shapes=list(scratch_shapes),
            compiler_params=pltpu.CompilerParams(
                dimension_semantics=("arbitrary",) * len(grid),
                vmem_limit_bytes=need + windows + V7X_COMPILER_VMEM_RESERVE),
            name=name)(*inputs)

    return call


def _mod_body(c_ref, w_ref, b_ref, o_ref):
    c = c_ref[...]
    ca = (c * _sigmoid(c)).astype(BF16)
    o_ref[...] = jnp.dot(ca, w_ref[...].astype(BF16), preferred_element_type=F32) + b_ref[...]


def _modulation(c8, w, b):
    n_l, d, n = w.shape
    nb = 1024
    return _pallas(
        _mod_body,
        grid=(n_l, n // nb),
        in_specs=[
            pl.BlockSpec((8, d), lambda l, j: (0, 0)),
            pl.BlockSpec((None, d, nb), lambda l, j: (l, 0, j)),
            pl.BlockSpec((None, 1, nb), lambda l, j: (l, 0, j)),
        ],
        out_specs=pl.BlockSpec((None, 8, nb), lambda l, j: (l, 0, j)),
        out_shape=jax.ShapeDtypeStruct((n_l, 8, n), F32),
        name="modulation",
    )(c8, w, b.reshape(n_l, 1, n))


def _gmlp_body(x_ref, mod_ref, pre_ref, post_ref, win_ref, bin_ref, lng_ref, lnb_ref, ws_ref,
               bs_ref, wout_ref, o_ref, h_scr, u_scr, v_scr, vn_scr, y_scr):
    tm, d = x_ref.shape
    gw = u_scr.shape[1]
    x = x_ref[...]
    h_scr[...] = _prenorm(x, pre_ref[...], mod_ref[0:1, :], mod_ref[1:2, :]).astype(BF16)

    k_gelu = 0.7978845608028654
    nc = 512
    for c0 in range(0, 2 * gw, nc):
        z = jnp.dot(h_scr[...], win_ref[:, c0:c0 + nc], preferred_element_type=F32)
        z = z + bin_ref[:, c0:c0 + nc]
        z = z * (0.5 * (1.0 + jnp.tanh(k_gelu * (z + 0.044715 * (z * z * z)))))
        if c0 < gw:
            u_scr[:, c0:c0 + nc] = z
        else:
            v_scr[:, c0 - gw:c0 - gw + nc] = z

    v = v_scr[...]
    mu = jnp.mean(v, axis=-1, keepdims=True)
    vc = v - mu
    rstd = lax.rsqrt(jnp.mean(vc * vc, axis=-1, keepdims=True) + EPS)
    vn_scr[...] = (vc * rstd * lng_ref[...] + lnb_ref[...]).astype(BF16)

    n_chunks = tm // CHUNK
    row = lax.broadcasted_iota(jnp.int32, (CHUNK, CHUNK), 0)
    col = lax.broadcasted_iota(jnp.int32, (CHUNK, CHUNK), 1)
    causal = col <= row
    for g in range(GROUPS):
        gs = slice(g * CHUNK, (g + 1) * CHUNK)
        ws = jnp.where(causal, ws_ref[g], 0.0).astype(BF16)
        rhs = jnp.concatenate(
            [vn_scr[c * CHUNK:(c + 1) * CHUNK, gs] for c in range(n_chunks)], axis=1)
        sp = jnp.dot(ws, rhs, preferred_element_type=F32)
        for c in range(n_chunks):
            cs = slice(c * CHUNK, (c + 1) * CHUNK)
            y_scr[cs, gs] = (u_scr[cs, gs] * (sp[:, cs] + bs_ref[:, gs])).astype(BF16)

    y = jnp.dot(y_scr[...], wout_ref[:, 0:d], preferred_element_type=F32)
    o_ref[...] = _gated_residual(x, y, post_ref[...], mod_ref[2:3, :])


def _gmlp_layer(x, mod, pre_g, post_g, w_in, b_in, ln_g, ln_b, w_s, bs_full, w_out, i):
    b, s, d = x.shape
    gw = w_out.shape[1]
    tm = TOKEN_TILE
    tok = pl.BlockSpec((None, tm, d), lambda i, j: (i, j, 0))
    return _pallas(
        _gmlp_body,
        grid=(b, s // tm),
        in_specs=[
            tok,
            pl.BlockSpec((None, 6, d), lambda i, j: (i, 0, 0)),
            _const_spec((1, d)), _const_spec((1, d)),
            _slab_spec((None,) + w_in.shape[1:], (i, 0, 0)), _const_spec((1, 2 * gw)),
            _const_spec((1, gw)), _const_spec((1, gw)),
            _const_spec((GROUPS, CHUNK, CHUNK)), _const_spec((CHUNK, gw)),
            _slab_spec((None,) + w_out.shape[1:], (i, 0, 0)),
        ],
        out_specs=tok,
        out_shape=jax.ShapeDtypeStruct((b, s, d), F32),
        scratch_shapes=[
            pltpu.VMEM((tm, d), BF16), pltpu.VMEM((tm, gw), F32), pltpu.VMEM((tm, gw), F32),
            pltpu.VMEM((tm, gw), BF16), pltpu.VMEM((tm, gw), BF16),
        ],
        name="gmlp_mixer",
    )(x, mod, pre_g, post_g, w_in, b_in, ln_g, ln_b, w_s, bs_full, w_out)


def _ffn_body(x_ref, mod_ref, pre_ref, post_ref, wgu_ref, wd_ref, o_ref, h_scr, a_scr):
    f = a_scr.shape[1]
    x = x_ref[...]
    h_scr[...] = _prenorm(x, pre_ref[...], mod_ref[3:4, :], mod_ref[4:5, :]).astype(BF16)
    for c0 in range(0, f, MXU_N):
        g = jnp.dot(h_scr[...], wgu_ref[:, c0:c0 + MXU_N], preferred_element_type=F32)
        u = jnp.dot(h_scr[...], wgu_ref[:, f + c0:f + c0 + MXU_N], preferred_element_type=F32)
        a_scr[:, c0:c0 + MXU_N] = (g * _sigmoid(g) * u).astype(BF16)
    y = jnp.dot(a_scr[...], wd_ref[...], preferred_element_type=F32)
    o_ref[...] = _gated_residual(x, y, post_ref[...], mod_ref[5:6, :])


def _ffn_layer(x, mod, pre_g, post_g, w_gu, w_down, layer):
    b, s, d = x.shape
    f = w_down.shape[1]
    tm = WIDE_TOKEN_TILE
    tok = pl.BlockSpec((None, tm, d), lambda i, j: (i, j, 0))
    return _pallas(
        _ffn_body,
        grid=(b, s // tm),
        in_specs=[
            tok,
            pl.BlockSpec((None, 6, d), lambda i, j: (i, 0, 0)),
            _const_spec((1, d)), _const_spec((1, d)),
            _slab_spec((None, d, 2 * f), (layer, 0, 0)), _slab_spec((None, f, d), (layer, 0, 0)),
        ],
        out_specs=tok,
        out_shape=jax.ShapeDtypeStruct((b, s, d), F32),
        scratch_shapes=[pltpu.VMEM((tm, d), BF16), pltpu.VMEM((tm, f), BF16)],
        name="swiglu_ffn",
    )(x, mod, pre_g, post_g, w_gu, w_down)


def _head_pair_norm(t, gain2):
    lane = lax.broadcasted_iota(jnp.int32, t.shape, 1)
    sq = t * t
    first = lane < HEAD_DIM
    s0 = jnp.sum(jnp.where(first, sq, 0.0), axis=-1, keepdims=True)
    s1 = jnp.sum(jnp.where(first, 0.0, sq), axis=-1, keepdims=True)
    rs = jnp.where(first, lax.rsqrt(s0 / HEAD_DIM + EPS), lax.rsqrt(s1 / HEAD_DIM + EPS))
    return t * rs * gain2


def _decay_selectors(query_side):
    row = lax.broadcasted_iota(jnp.int32, (LANES, N_HEADS * LANES), 0)
    col = lax.broadcasted_iota(jnp.int32, (LANES, N_HEADS * LANES), 1)
    head, off = col // LANES, col % LANES - HEAD_DIM
    first, second = (off >= 0) & (off < 3), (off >= 3) & (off < 6)
    piece_cols, piece_idx = (first, off) if query_side else (second, off - 3)
    one_cols = second if query_side else first
    is_piece = piece_cols & (row == piece_idx * N_HEADS + head)
    is_one = one_cols & (row == 3 * N_HEADS)
    sel = jnp.where(is_piece, 1.0 if query_side else -1.0, jnp.where(is_one, 1.0, 0.0))
    return sel.astype(BF16)


def _with_tail(t, tail):
    lane = lax.broadcasted_iota(jnp.int32, t.shape, 1)
    return jnp.where(lane < HEAD_DIM, t, tail)


def _kv_body(x_ref, mod_ref, g_ref, wk_ref, wv_ref, wf_ref, bf_ref, kg_ref, sel_ref,
             k_out, vt_out, d_out, h_scr, carry_scr):
    tm, d = x_ref.shape

    @pl.when(pl.program_id(1) == 0)
    def _():
        carry_scr[...] = jnp.zeros_like(carry_scr)

    h_scr[...] = _prenorm(x_ref[...], g_ref[...], mod_ref[0:1, :], mod_ref[1:2, :]).astype(BF16)

    lane = lax.broadcasted_iota(jnp.int32, (tm, LANES), 1)
    fl = jnp.dot(h_scr[...], wf_ref[...], preferred_element_type=F32) + bf_ref[...]
    z = -fl
    ls = -(jnp.maximum(z, 0.0) + jnp.log1p(jnp.exp(-jnp.abs(z))))
    hi, mid, lo = _split3(ls)
    pieces = jnp.where(lane < N_HEADS, hi, jnp.where(lane < 2 * N_HEADS, mid, jnp.where(
        lane < 3 * N_HEADS, lo, 0.0))).astype(BF16)
    r = lax.broadcasted_iota(jnp.int32, (tm, tm), 0)
    c = lax.broadcasted_iota(jnp.int32, (tm, tm), 1)
    tri = jnp.where(c <= r, 1.0, 0.0).astype(BF16)
    cs = jnp.dot(tri, pieces, preferred_element_type=F32)
    tot = cs + pltpu.roll(cs, LANES - N_HEADS, axis=1) + pltpu.roll(cs, LANES - 2 * N_HEADS, axis=1)
    dc = jnp.where(lane < N_HEADS, tot + carry_scr[...], 0.0)
    carry_scr[...] = dc[tm - 1:tm, :]
    hi, mid, lo = _split3(dc * LOG2E)
    d3 = (hi + pltpu.roll(mid, N_HEADS, axis=1) + pltpu.roll(lo, 2 * N_HEADS, axis=1)
          + jnp.where(lane == 3 * N_HEADS, 1.0, 0.0)).astype(BF16)
    d_out[...] = d3

    kg2 = kg_ref[...]
    ones_rows = jnp.where(
        lax.broadcasted_iota(jnp.int32, (V_ROWS - HEAD_DIM, tm), 0) == 0, 1.0, 0.0)
    heads_per_chunk = MXU_N // HEAD_DIM
    for c4 in range(d // MXU_N):
        kc =jnp.dot(h_scr[...], wk_ref[:, c4 * MXU_N:(c4 + 1) * MXU_N], preferred_element_type=F32)
        vc = jnp.dot(h_scr[...], wv_ref[:, c4 * MXU_N:(c4 + 1) * MXU_N], preferred_element_type=F32)
        tails = jnp.dot(d3, sel_ref[:, c4 * heads_per_chunk * LANES:(c4 + 1) * heads_per_chunk * LANES],
                        preferred_element_type=F32)
        for half in range(2):
            kn = _head_pair_norm(kc[:, half * LANES:(half + 1) * LANES], kg2)
            vt = vc[:, half * LANES:(half + 1) * LANES].T
            for sub, src in ((0, kn), (1, pltpu.roll(kn, HEAD_DIM, axis=1))):
                loc = 2 * half + sub
                hd = c4 * heads_per_chunk + loc
                k_out[hd] = _with_tail(src, tails[:, loc * LANES:(loc + 1) * LANES]).astype(BF16)
                vt_out[hd] = jnp.concatenate(
                    [vt[sub * HEAD_DIM:(sub + 1) * HEAD_DIM, :], ones_rows], axis=0).astype(BF16)


def _kv_project(x, kvmod, norm_g, w_k, w_v, w_f3, b_f3, kg2):
    b, s, d = x.shape
    tm = TOKEN_TILE
    return _pallas(
        _kv_body,
        grid=(b, s // tm),
        in_specs=[
            pl.BlockSpec((None, tm, d), lambda i, j: (i, j, 0)),
            pl.BlockSpec((None, 2, d), lambda i, j: (i, 0, 0)),
            _const_spec((1, d)),
            _const_spec((d, d)), _const_spec((d, d)), _const_spec((d, LANES)),
            _const_spec((1, LANES)), _const_spec((1, LANES)), _const_spec((LANES, N_HEADS * LANES)),
        ],
        out_specs=[
            pl.BlockSpec((None, N_HEADS, tm, LANES), lambda i, j: (i, 0, j, 0)),
            pl.BlockSpec((None, N_HEADS, None, V_ROWS, tm), lambda i, j: (i, 0, j, 0, 0)),
            pl.BlockSpec((None, tm, LANES), lambda i, j: (i, j, 0)),
        ],
        out_shape=[
            jax.ShapeDtypeStruct((b, N_HEADS, s, LANES), BF16),
            jax.ShapeDtypeStruct((b, N_HEADS, s // tm, V_ROWS, tm), BF16),
            jax.ShapeDtypeStruct((b, s, LANES), BF16),
        ],
        scratch_shapes=[pltpu.VMEM((tm, d), BF16), pltpu.VMEM((1, LANES), F32)],
        name="kv_project",
    )(x, kvmod, norm_g, w_k, w_v, w_f3, b_f3, kg2, _decay_selectors(query_side=False))


def _q_body(x_ref, mod_ref, pre_ref, wq_ref, qg_ref, d_ref, sel_ref, q_out, h_scr):
    tm, d = x_ref.shape
    h_scr[...] = _prenorm(x_ref[...], pre_ref[...], mod_ref[0:1, :], mod_ref[1:2, :]).astype(BF16)
    d3 = d_ref[...]
    qg2 = qg_ref[...]
    heads_per_chunk = MXU_N // HEAD_DIM
    for c4 in range(d // MXU_N):
        qc = jnp.dot(h_scr[...], wq_ref[:, c4 * MXU_N:(c4 + 1) * MXU_N], preferred_element_type=F32)
        tails = jnp.dot(d3, sel_ref[:, c4 * heads_per_chunk * LANES:(c4 + 1) * heads_per_chunk * LANES],
                        preferred_element_type=F32)
        for half in range(2):
            qn = _head_pair_norm(qc[:, half * LANES:(half + 1) * LANES], qg2)
            for sub, src in ((0, qn), (1, pltpu.roll(qn, HEAD_DIM, axis=1))):
                loc = 2 * half + sub
                qa = _with_tail(src, tails[:, loc * LANES:(loc + 1) * LANES])
                q_out[c4 * heads_per_chunk + loc] = qa.astype(BF16)


def _q_project(x, mod, pre_g, w_q, qg2, d3):
    b, s, d = x.shape
    tm = TOKEN_TILE
    return _pallas(
        _q_body,
        grid=(b, s // tm),
        in_specs=[
            pl.BlockSpec((None, tm, d), lambda i, j: (i, j, 0)),
            pl.BlockSpec((None, 6, d), lambda i, j: (i, 0, 0)),
            _const_spec((1, d)), _const_spec((d, d)), _const_spec((1, LANES)),
            pl.BlockSpec((None, tm, LANES), lambda i, j: (i, j, 0)),
            _const_spec((LANES, N_HEADS * LANES)),
        ],
        out_specs=pl.BlockSpec((None, N_HEADS, tm, LANES), lambda i, j: (i, 0, j, 0)),
        out_shape=jax.ShapeDtypeStruct((b, N_HEADS, s, LANES), BF16),
        scratch_shapes=[pltpu.VMEM((tm, d), BF16)],
        name="q_project",
    )(x, mod, pre_g, w_q, qg2, d3, _decay_selectors(query_side=True))


def _attn_body(q_ref, k_ref, vt_ref, o_ref, acc_scr, *s_scr):
    n_h = q_ref.shape[0]
    tk = vt_ref.shape[3]
    tq = 2 * tk
    nq = q_ref.shape[1] // tq
    heads = range(n_h)
    halves = (slice(0, tk), slice(tk, tq))
    kpos = lax.broadcasted_iota(jnp.int32, (tk, tk), 0)
    qpos = lax.broadcasted_iota(jnp.int32, (tk, tk), 1)
    tri = kpos <= qpos
    LO, HI = 0, 1

    def scores(qt, j, slot, g, valid):
        cms = []
        for h in heads:
            q = q_ref[h, pl.ds(pl.multiple_of(qt * tq + g * tk, tk), tk), :]
            k = k_ref[h, pl.ds(pl.multiple_of(j * tk, tk), tk), :]
            s = lax.dot_general(k, q, (((1,), (1,)), ((), ())), preferred_element_type=F32)
            if valid is not None:
                s = jnp.where(valid, s, -jnp.inf)
            s_scr[2 * h + slot][:, halves[g]] = s
            cms.append(jnp.max(s, axis=0, keepdims=True))
        return tuple(cms)

    def accumulate(j, slot, g, cmax, ml):
        out = []
        for h in heads:
            m_new = jnp.maximum(ml[h], cmax[h])
            alpha = jnp.exp2(ml[h] - m_new)
            p = jnp.exp2(s_scr[2 * h + slot][:, halves[g]] - m_new)
            pv = jnp.dot(vt_ref[h, j], p.astype(BF16), preferred_element_type=F32)
            acc_scr[h, :, halves[g]] = alpha * acc_scr[h, :, halves[g]] + pv
            out.append(m_new)
        return tuple(out)

    def pair(qi, p, carry, diag):
        (ml_lo, ml_hi), (c0_lo, c0_hi) = carry
        c1_lo = scores(qi, 2 * p + 1, 1, LO, None)
        ml_lo = accumulate(2 * p, 0, LO, c0_lo, ml_lo)
        c1_hi = scores(qi, 2 * p + 1, 1, HI, None)
        ml_hi = accumulate(2 * p, 0, HI, c0_hi, ml_hi)
        c0_lo = scores(qi, 2 * p + 2, 0, LO, tri if diag else None)
        ml_lo = accumulate(2 * p + 1, 1, LO, c1_lo, ml_lo)
        c0_hi = scores(qi, 2 * p + 2, 0, HI, None)
        ml_hi = accumulate(2 * p + 1, 1, HI, c1_hi, ml_hi)
        return (ml_lo, ml_hi), (c0_lo, c0_hi)

    def query_tile(qi, cm0):
        acc_scr[...] = jnp.zeros_like(acc_scr)
        fresh = tuple(jnp.full((1, tk), -jnp.inf, F32) for _ in heads)
        n_loop = jnp.maximum(qi - 1, 0)

        def two_pairs(t, c):
            return pair(qi, 2 * t + 1, pair(qi, 2 * t, c, False), False)

        carry = lax.fori_loop(0, n_loop // 2, two_pairs, ((fresh, fresh), cm0))

        def finish(carry, n_front):
            if n_front == 2:
                carry = pair(qi, qi - 2, carry, False)
            if n_front >= 1:
                carry = pair(qi, qi - 1, carry, True)
            (ml_lo, ml_hi), (c0_lo, c0_hi) = carry
            ml_lo = accumulate(2 * qi, 0, LO, c0_lo, ml_lo)
            c1_hi = scores(qi, 2 * qi + 1, 1, HI, tri)
            ml_hi = accumulate(2 * qi, 0, HI, c0_hi, ml_hi)
            nxt = jnp.minimum(qi + 1, nq - 1)
            n_lo = scores(nxt, 0, 0, LO, None)
            accumulate(2 * qi + 1, 1, HI, c1_hi, ml_hi)
            n_hi = scores(nxt, 0, 0, HI, None)
            for g in (LO, HI):
                o_t = jnp.concatenate(
                    [acc_scr[h, 0:HEAD_DIM, halves[g]] / acc_scr[h, HEAD_DIM:HEAD_DIM + 1, halves[g]]
                     for h in heads], axis=0)
                o_ref[pl.ds(pl.multiple_of((2 * qi + g) * tk, tk), tk), :] = o_t.T
            return n_lo, n_hi

        n_front = jnp.minimum(qi, 1) + (n_loop & 1)
        return lax.switch(n_front, [functools.partial(finish, n_front=n) for n in range(3)], carry)

    lax.fori_loop(0, nq, query_tile, (scores(0, 0, 0, LO, tri), scores(0, 0, 0, HI, None)))


def _fox_attention(q_aug, k_aug, vt_aug):
    b, h, s, _ = k_aug.shape
    tk = TOKEN_TILE
    nk = s // tk
    hp = ATTN_HEADS_PER_STEP
    return _pallas(
        _attn_body,
        grid=(b, h // hp),
        in_specs=[
            pl.BlockSpec((None, hp, s, LANES), lambda i, j: (i, j, 0, 0)),
            pl.BlockSpec((None, hp, s, LANES), lambda i, j: (i, j, 0, 0)),
            pl.BlockSpec((None, hp, nk, V_ROWS, tk), lambda i, j: (i, j, 0, 0, 0)),
        ],
        out_specs=pl.BlockSpec((None, s, hp * HEAD_DIM), lambda i, j: (i, 0, j)),
        out_shape=jax.ShapeDtypeStruct((b, s, h * HEAD_DIM), F32),
        scratch_shapes=[pltpu.VMEM((hp, V_ROWS, ATTN_Q_TILE), F32)]
        + [pltpu.VMEM((tk, ATTN_Q_TILE), F32)] * (2 * hp),
        name="fox_attention",
    )(q_aug, k_aug, vt_aug)


def _out_body(x_ref, mod_ref, pre_ref, post_ref, a_ref, wg_ref, wo_ref, o_ref):
    x = x_ref[...]
    h = _prenorm(x, pre_ref[...], mod_ref[0:1, :], mod_ref[1:2, :]).astype(BF16)
    gate = _sigmoid(jnp.dot(h, wg_ref[...], preferred_element_type=F32))
    y = jnp.dot((a_ref[...] * gate).astype(BF16), wo_ref[...], preferred_element_type=F32)
    o_ref[...] = _gated_residual(x, y, post_ref[...], mod_ref[2:3, :])


def _fox_output(x, mod, pre_g, post_g, o_t, w_g, w_o):
    b, s, d = x.shape
    tm = WIDE_TOKEN_TILE
    tok = pl.BlockSpec((None, tm, d), lambda i, j: (i, j, 0))
    return _pallas(
        _out_body,
        grid=(b, s // tm),
        in_specs=[
            tok,
            pl.BlockSpec((None, 6, d), lambda i, j: (i, 0, 0)),
            _const_spec((1, d)), _const_spec((1, d)),
            tok,
            _const_spec((d, d)), _const_spec((d, d)),
        ],
        out_specs=tok,
        out_shape=jax.ShapeDtypeStruct((b, s, d), F32),
        name="fox_output",
    )(x, mod, pre_g, post_g, o_t, w_g, w_o)


def kernel(x, c, ada_w, ada_b, pre_mix_g, post_mix_g, pre_ffn_g, post_ffn_g, ffn_w_gu, ffn_w_down,
           a_w_in, a_b_in, a_ln_g, a_ln_b, a_w_s, a_b_s, a_w_out, kv_ada_w, kv_ada_b, kv_norm_g,
           kv_w, kv_b_f, k_norm_g, b_w_qg, b_q_norm_g, b_w_o):
    b, s, d = x.shape
    depth = ada_w.shape[0]
    n_a = a_w_in.shape[0]
    assert d == N_HEADS * HEAD_DIM and s % ATTN_Q_TILE == 0

    c8 = jnp.pad(c, ((0, 8 - b), (0, 0)))
    mods = _modulation(c8, ada_w, ada_b)[:, :b].reshape(depth, b, 6, d)
    kvmod = _modulation(c8, kv_ada_w[None], kv_ada_b[None])[0, :b].reshape(b, 2, d)

    row = lambda v: v.reshape(1, -1)
    q_scale = LOG2E * HEAD_DIM ** -0.5

    ffn_gu, ffn_down = ffn_w_gu.astype(BF16), ffn_w_down.astype(BF16)
    gmlp_in, gmlp_out = _lane_padded(a_w_in), _lane_padded(a_w_out)
    fox_qg, fox_o, kvf = b_w_qg.astype(BF16), b_w_o.astype(BF16), kv_w.astype(BF16)

    for layer in range(depth):
        mod = mods[layer]
        if layer < n_a:
            i = layer
            bs_full = jnp.repeat(a_b_s[i].T, CHUNK, axis=1)
            x = _gmlp_layer(x, mod, row(pre_mix_g[layer]), row(post_mix_g[layer]), gmlp_in,
                            row(a_b_in[i]), row(a_ln_g[i]), row(a_ln_b[i]), a_w_s[i], bs_full, gmlp_out, i)
        else:
            j = layer - n_a
            qg2 = row(jnp.tile(b_q_norm_g[j] * q_scale, 2))
            q_aug = _q_project(x, mod, row(pre_mix_g[layer]), fox_qg[j][:, :d], qg2, d3)
            o_t = _fox_attention(q_aug, k_aug, vt_aug)
            x = _fox_output(x, mod, row(pre_mix_g[layer]), row(post_mix_g[layer]), o_t,
                            fox_qg[j][:, d:], fox_o[j])
        x = _ffn_layer(x, mod, row(pre_ffn_g[layer]), row(post_ffn_g[layer]), ffn_gu, ffn_down, layer)
        if layer == n_a - 1:
            w_f = kv_w[:, 2 * d:]
            pad = jnp.zeros((d, LANES - 3 * N_HEADS), F32)
            w_f3 = jnp.concatenate([w_f, w_f, w_f, pad], axis=1).astype(BF16)
            b_f3 = row(jnp.concatenate([kv_b_f, kv_b_f, kv_b_f, jnp.zeros((LANES - 3 * N_HEADS,), F32)]))
            k_aug, vt_aug, d3 = _kv_project(
                x, kvmod, row(kv_norm_g), kvf[:, :d], kvf[:, d:2 * d], w_f3, b_f3,
                row(jnp.tile(k_norm_g, 2)))
    return x
```

```python
import functools
import math

import jax
import jax.numpy as jnp
from jax import lax
from jax.experimental import pallas as pl
from jax.experimental.pallas import tpu as pltpu

F32 = jnp.float32
BF16 = jnp.bfloat16

EPS = 1e-6
N_HEADS = 16
HEAD_DIM = 64
CHUNK = 128
GROUPS = 16
LOG2E = 1.4426950408889634
LANES = 128
MXU_N = 256
TOKEN_TILE = 512
WIDE_TOKEN_TILE = 2 * TOKEN_TILE
ATTN_Q_TILE = 2 * TOKEN_TILE
ATTN_HEADS_PER_STEP = 2
F32_SUBLANES = 8
BF16_SUBLANES = 16
MOD_COLS = 1024
V_ROWS = HEAD_DIM + BF16_SUBLANES
V7X_COMPILER_VMEM_RESERVE = 6 * 1024 * 1024


def _sigmoid(x):
    return 1.0 / (1.0 + jnp.exp(-x))


def _unit_rms(x):
    return x * lax.rsqrt(jnp.mean(x * x, axis=-1, keepdims=True) + EPS)


def _prenorm(x, g, shift, scale):
    return _unit_rms(x) * (g * (1.0 + scale)) + shift


def _gated_residual(x, y, g, gate):
    return x + _unit_rms(y) * (g * gate)


def _split3(x):
    hi = x.astype(BF16).astype(F32)
    r = x - hi
    mid = r.astype(BF16).astype(F32)
    lo = (r - mid).astype(BF16).astype(F32)
    return hi, mid, lo


def _const_spec(shape):
    n = len(shape)
    return pl.BlockSpec(shape, lambda *_: (0,) * n, pipeline_mode=pl.Buffered(1))


def _slab_spec(block, index):
    return pl.BlockSpec(block, lambda *_: index, pipeline_mode=pl.Buffered(1))


def _lane_padded(w):
    return jnp.pad(w, ((0, 0),) * (w.ndim - 1) + ((0, LANES),)).astype(BF16)


def _block_bytes(spec, dtype):
    buffers = 2 if spec.pipeline_mode is None else spec.pipeline_mode.buffer_count
    return math.prod(1 if n is None else n for n in spec.block_shape) * jnp.dtype(dtype).itemsize * buffers


def _pallas(body, *, grid, in_specs, out_specs, out_shape, scratch_shapes=(), name):
    multi = isinstance(out_shape, (list, tuple))
    outs = zip(out_specs, out_shape) if multi else [(out_specs, out_shape)]
    need = sum(_block_bytes(s, o.dtype) for s, o in outs)
    need += sum(math.prod(m.shape) * jnp.dtype(m.dtype).itemsize for m in scratch_shapes)

    def call(*inputs):
        windows = sum(_block_bytes(s, a.dtype) for s, a in zip(in_specs, inputs, strict=True))
        return pl.pallas_call(
            body, grid=grid, in_specs=in_specs, out_specs=out_specs, out_shape=out_shape,
            scratch_shapes=list(scratch_shapes),
            compiler_params=pltpu.CompilerParams(
                dimension_semantics=("arbitrary",) * len(grid),
                vmem_limit_bytes=need + windows + V7X_COMPILER_VMEM_RESERVE),
            name=name)(*inputs)

    return call


def _mod_body(c_ref, w_ref, b_ref, o_ref):
    c = c_ref[...]
    ca = (c * _sigmoid(c)).astype(BF16)
    o_ref[...] = jnp.dot(ca, w_ref[...].astype(BF16), preferred_element_type=F32) + b_ref[...]


def _modulation(c, w, b):
    n_l, d, n = w.shape
    batch = c.shape[0]
    rows = -(-batch // F32_SUBLANES) * F32_SUBLANES
    nb = MOD_COLS
    out = _pallas(
        _mod_body,
        grid=(n_l, n // nb),
        in_specs=[
            pl.BlockSpec((rows, d), lambda l, j: (0, 0)),
            pl.BlockSpec((None, d, nb), lambda l, j: (l, 0, j)),
            pl.BlockSpec((None, 1, nb), lambda l, j: (l, 0, j)),
        ],
        out_specs=pl.BlockSpec((None, rows, nb), lambda l, j: (l, 0, j)),
        out_shape=jax.ShapeDtypeStruct((n_l, rows, n), F32),
        name="modulation",
    )(jnp.pad(c, ((0, rows - batch), (0, 0))), w, b.reshape(n_l, 1, n))
    return out[:, :batch]


def _gmlp_body(x_ref, mod_ref, pre_ref, post_ref, win_ref, bin_ref, lng_ref, lnb_ref, ws_ref,
               bs_ref, wout_ref, o_ref, h_scr, u_scr, v_scr, vn_scr, y_scr):
    tm, d = x_ref.shape
    gw = u_scr.shape[1]
    x = x_ref[...]
    h_scr[...] = _prenorm(x, pre_ref[...], mod_ref[0:1, :], mod_ref[1:2, :]).astype(BF16)

    k_gelu = 0.7978845608028654
    nc = 2 * MXU_N
    for c0 in range(0, 2 * gw, nc):
        z = jnp.dot(h_scr[...], win_ref[:, c0:c0 + nc], preferred_element_type=F32)
        z = z + bin_ref[:, c0:c0 + nc]
        z = z * (0.5 * (1.0 + jnp.tanh(k_gelu * (z + 0.044715 * (z * z * z)))))
        if c0 < gw:
            u_scr[:, c0:c0 + nc] = z
        else:
            v_scr[:, c0 - gw:c0 - gw + nc] = z

    v = v_scr[...]
    mu = jnp.mean(v, axis=-1, keepdims=True)
    vc = v - mu
    rstd = lax.rsqrt(jnp.mean(vc * vc, axis=-1, keepdims=True) + EPS)
    vn_scr[...] = (vc * rstd * lng_ref[...] + lnb_ref[...]).astype(BF16)

    n_chunks = tm // CHUNK
    row = lax.broadcasted_iota(jnp.int32, (CHUNK, CHUNK), 0)
    col = lax.broadcasted_iota(jnp.int32, (CHUNK, CHUNK), 1)
    causal = col <= row
    for g in range(GROUPS):
        gs = slice(g * CHUNK, (g + 1) * CHUNK)
        ws = jnp.where(causal, ws_ref[g], 0.0).astype(BF16)
        rhs = jnp.concatenate(
            [vn_scr[c * CHUNK:(c + 1) * CHUNK, gs] for c in range(n_chunks)], axis=1)
        sp = jnp.dot(ws, rhs, preferred_element_type=F32)
        for c in range(n_chunks):
            cs = slice(c * CHUNK, (c + 1) * CHUNK)
            y_scr[cs, gs] = (u_scr[cs, gs] * (sp[:, cs] + bs_ref[:, gs])).astype(BF16)

    y = jnp.dot(y_scr[...], wout_ref[:, 0:d], preferred_element_type=F32)
    o_ref[...] = _gated_residual(x, y, post_ref[...], mod_ref[2:3, :])


def _gmlp_layer(x, mod, pre_g, post_g, w_in, b_in, ln_g, ln_b, w_s, bs_full, w_out, layer):
    b, s, d = x.shape
    gw = w_out.shape[1]
    tm = TOKEN_TILE
    tok = pl.BlockSpec((None, tm, d), lambda i, j: (i, j, 0))
    return _pallas(
        _gmlp_body,
        grid=(b, s // tm),
        in_specs=[
            tok,
            pl.BlockSpec((None, 6, d), lambda i, j: (i, 0, 0)),
            _const_spec((1, d)), _const_spec((1, d)),
            _slab_spec((None,) + w_in.shape[1:], (layer, 0, 0)), _const_spec((1, 2 * gw)),
            _const_spec((1, gw)), _const_spec((1, gw)),
            _const_spec((GROUPS, CHUNK, CHUNK)), _const_spec((CHUNK, gw)),
            _slab_spec((None,) + w_out.shape[1:], (layer, 0, 0)),
        ],
        out_specs=tok,
        out_shape=jax.ShapeDtypeStruct((b, s, d), F32),
        scratch_shapes=[
            pltpu.VMEM((tm, d), BF16), pltpu.VMEM((tm, gw), F32), pltpu.VMEM((tm, gw), F32),
            pltpu.VMEM((tm, gw), BF16), pltpu.VMEM((tm, gw), BF16),
        ],
        name="gmlp_mixer",
    )(x, mod, pre_g, post_g, w_in, b_in, ln_g, ln_b, w_s, bs_full, w_out)


def _ffn_body(x_ref, mod_ref, pre_ref, post_ref, wgu_ref, wd_ref, o_ref, h_scr, a_scr):
    f = a_scr.shape[1]
    x = x_ref[...]
    h_scr[...] = _prenorm(x, pre_ref[...], mod_ref[3:4, :], mod_ref[4:5, :]).astype(BF16)
    for c0 in range(0, f, MXU_N):
        g = jnp.dot(h_scr[...], wgu_ref[:, c0:c0 + MXU_N], preferred_element_type=F32)
        u = jnp.dot(h_scr[...], wgu_ref[:, f + c0:f + c0 + MXU_N], preferred_element_type=F32)
        a_scr[:, c0:c0 + MXU_N] = (g * _sigmoid(g) * u).astype(BF16)
    y = jnp.dot(a_scr[...], wd_ref[...], preferred_element_type=F32)
    o_ref[...] = _gated_residual(x, y, post_ref[...], mod_ref[5:6, :])


def _ffn_layer(x, mod, pre_g, post_g, w_gu, w_down, layer):
    b, s, d = x.shape
    f = w_down.shape[1]
    tm = WIDE_TOKEN_TILE
    tok = pl.BlockSpec((None, tm, d), lambda i, j: (i, j, 0))
    return _pallas(
        _ffn_body,
        grid=(b, s // tm),
        in_specs=[
            tok,
            pl.BlockSpec((None, 6, d), lambda i, j: (i, 0, 0)),
            _const_spec((1, d)), _const_spec((1, d)),
            _slab_spec((None, d, 2 * f), (layer, 0, 0)), _slab_spec((None, f, d), (layer, 0, 0)),
        ],
        out_specs=tok,
        out_shape=jax.ShapeDtypeStruct((b, s, d), F32),
        scratch_shapes=[pltpu.VMEM((tm, d), BF16), pltpu.VMEM((tm, f), BF16)],
        name="swiglu_ffn",
    )(x, mod, pre_g, post_g, w_gu, w_down)


def _head_pair_norm(t, gain2):
    lane = lax.broadcasted_iota(jnp.int32, t.shape, 1)
    sq = t * t
    first = lane < HEAD_DIM
    s0 = jnp.sum(jnp.where(first, sq, 0.0), axis=-1, keepdims=True)
    s1 = jnp.sum(jnp.where(first, 0.0, sq), axis=-1, keepdims=True)
    rs = jnp.where(first, lax.rsqrt(s0 / HEAD_DIM + EPS), lax.rsqrt(s1 / HEAD_DIM + EPS))
    return t * rs * gain2


def _decay_selectors(query_side):
    row = lax.broadcasted_iota(jnp.int32, (LANES, N_HEADS * LANES), 0)
    col = lax.broadcasted_iota(jnp.int32, (LANES, N_HEADS * LANES), 1)
    head, off = col // LANES, col % LANES - HEAD_DIM
    first, second = (off >= 0) & (off < 3), (off >= 3) & (off < 6)
    piece_cols, piece_idx = (first, off) if query_side else (second, off - 3)
    one_cols = second if query_side else first
    is_piece = piece_cols & (row == piece_idx * N_HEADS + head)
    is_one = one_cols & (row == 3 * N_HEADS)
    sel = jnp.where(is_piece, 1.0 if query_side else -1.0, jnp.where(is_one, 1.0, 0.0))
    return sel.astype(BF16)


def _with_tail(t, tail):
    lane = lax.broadcasted_iota(jnp.int32, t.shape, 1)
    return jnp.where(lane < HEAD_DIM, t, tail)


def _kv_body(x_ref, mod_ref, g_ref, wk_ref, wv_ref, wf_ref, bf_ref, kg_ref, sel_ref,
             k_out, vt_out, d_out, h_scr, carry_scr):
    tm, d = x_ref.shape

    @pl.when(pl.program_id(1) == 0)
    def _():
        carry_scr[...] = jnp.zeros_like(carry_scr)

    h_scr[...] = _prenorm(x_ref[...], g_ref[...], mod_ref[0:1, :], mod_ref[1:2, :]).astype(BF16)

    lane = lax.broadcasted_iota(jnp.int32, (tm, LANES), 1)
    fl = jnp.dot(h_scr[...], wf_ref[...], preferred_element_type=F32) + bf_ref[...]
    z = -fl
    ls = -(jnp.maximum(z, 0.0) + jnp.log1p(jnp.exp(-jnp.abs(z))))
    hi, mid, lo = _split3(ls)
    pieces = jnp.where(lane < N_HEADS, hi, jnp.where(lane < 2 * N_HEADS, mid, jnp.where(
        lane < 3 * N_HEADS, lo, 0.0))).astype(BF16)
    r = lax.broadcasted_iota(jnp.int32, (tm, tm), 0)
    c = lax.broadcasted_iota(jnp.int32, (tm, tm), 1)
    tri = jnp.where(c <= r, 1.0, 0.0).astype(BF16)
    cs = jnp.dot(tri, pieces, preferred_element_type=F32)
    tot = cs + pltpu.roll(cs, LANES - N_HEADS, axis=1) + pltpu.roll(cs, LANES - 2 * N_HEADS, axis=1)
    dc = jnp.where(lane < N_HEADS, tot + carry_scr[...], 0.0)
    carry_scr[...] = dc[tm - 1:tm, :]
    hi, mid, lo = _split3(dc * LOG2E)
    d3 = (hi + pltpu.roll(mid, N_HEADS, axis=1) + pltpu.roll(lo, 2 * N_HEADS, axis=1)
          + jnp.where(lane == 3 * N_HEADS, 1.0, 0.0)).astype(BF16)
    d_out[...] = d3

    kg2 = kg_ref[...]
    ones_rows = jnp.where(
        lax.broadcasted_iota(jnp.int32, (V_ROWS - HEAD_DIM, tm), 0) == 0, 1.0, 0.0)
    heads_per_chunk = MXU_N // HEAD_DIM
    for c4 in range(d // MXU_N):
        kc = jnp.dot(h_scr[...], wk_ref[:, c4 * MXU_N:(c4 + 1) * MXU_N], preferred_element_type=F32)
        vc = jnp.dot(h_scr[...], wv_ref[:, c4 * MXU_N:(c4 + 1) * MXU_N], preferred_element_type=F32)
        tails = jnp.dot(d3, sel_ref[:, c4 * heads_per_chunk * LANES:(c4 + 1) * heads_per_chunk * LANES],
                        preferred_element_type=F32)
        for half in range(2):
            kn = _head_pair_norm(kc[:, half * LANES:(half + 1) * LANES], kg2)
            vt = vc[:, half * LANES:(half + 1) * LANES].T
            for sub, src in ((0, kn), (1, pltpu.roll(kn, HEAD_DIM, axis=1))):
                loc = 2 * half + sub
                hd = c4 * heads_per_chunk + loc
                k_out[hd] = _with_tail(src, tails[:, loc * LANES:(loc + 1) * LANES]).astype(BF16)
                vt_out[hd] = jnp.concatenate(
                    [vt[sub * HEAD_DIM:(sub + 1) * HEAD_DIM, :], ones_rows], axis=0).astype(BF16)


def _kv_project(x, kvmod, norm_g, w_k, w_v, w_f3, b_f3, kg2):
    b, s, d = x.shape
    tm = TOKEN_TILE
    return _pallas(
        _kv_body,
        grid=(b, s // tm),
        in_specs=[
            pl.BlockSpec((None, tm, d), lambda i, j: (i, j, 0)),
            pl.BlockSpec((None, 2, d), lambda i, j: (i, 0, 0)),
            _const_spec((1, d)),
            _const_spec((d, d)), _const_spec((d, d)), _const_spec((d, LANES)),
            _const_spec((1, LANES)), _const_spec((1, LANES)), _const_spec((LANES, N_HEADS * LANES)),
        ],
        out_specs=[
            pl.BlockSpec((None, N_HEADS, tm, LANES), lambda i, j: (i, 0, j, 0)),
            pl.BlockSpec((None, N_HEADS, None, V_ROWS, tm), lambda i, j: (i, 0, j, 0, 0)),
            pl.BlockSpec((None, tm, LANES), lambda i, j: (i, j, 0)),
        ],
        out_shape=[
            jax.ShapeDtypeStruct((b, N_HEADS, s, LANES), BF16),
            jax.ShapeDtypeStruct((b, N_HEADS, s // tm, V_ROWS, tm), BF16),
            jax.ShapeDtypeStruct((b, s, LANES), BF16),
        ],
        scratch_shapes=[pltpu.VMEM((tm, d), BF16), pltpu.VMEM((1, LANES), F32)],
        name="kv_project",
    )(x, kvmod, norm_g, w_k, w_v, w_f3, b_f3, kg2, _decay_selectors(query_side=False))


def _q_body(x_ref, mod_ref, pre_ref, wq_ref, qg_ref, d_ref, sel_ref, q_out, h_scr):
    tm, d = x_ref.shape
    h_scr[...] = _prenorm(x_ref[...], pre_ref[...], mod_ref[0:1, :], mod_ref[1:2, :]).astype(BF16)
    d3 = d_ref[...]
    qg2 = qg_ref[...]
    heads_per_chunk = MXU_N // HEAD_DIM
    for c4 in range(d // MXU_N):
        qc = jnp.dot(h_scr[...], wq_ref[:, c4 * MXU_N:(c4 + 1) * MXU_N], preferred_element_type=F32)
        tails = jnp.dot(d3, sel_ref[:, c4 * heads_per_chunk * LANES:(c4 + 1) * heads_per_chunk * LANES],
                        preferred_element_type=F32)
        for half in range(2):
            qn = _head_pair_norm(qc[:, half * LANES:(half + 1) * LANES], qg2)
            for sub, src in ((0, qn), (1, pltpu.roll(qn, HEAD_DIM, axis=1))):
                loc = 2 * half + sub
                qa = _with_tail(src, tails[:, loc * LANES:(loc + 1) * LANES])
                q_out[c4 * heads_per_chunk + loc] = qa.astype(BF16)


def _q_project(x, mod, pre_g, w_q, qg2, d3):
    b, s, d = x.shape
    tm = TOKEN_TILE
    return _pallas(
        _q_body,
        grid=(b, s // tm),
        in_specs=[
            pl.BlockSpec((None, tm, d), lambda i, j: (i, j, 0)),
            pl.BlockSpec((None, 6, d), lambda i, j: (i, 0, 0)),
            _const_spec((1, d)), _const_spec((d, d)), _const_spec((1, LANES)),
            pl.BlockSpec((None, tm, LANES), lambda i, j: (i, j, 0)),
            _const_spec((LANES, N_HEADS * LANES)),
        ],
        out_specs=pl.BlockSpec((None, N_HEADS, tm, LANES), lambda i, j: (i, 0, j, 0)),
        out_shape=jax.ShapeDtypeStruct((b, N_HEADS, s, LANES), BF16),
        scratch_shapes=[pltpu.VMEM((tm, d), BF16)],
        name="q_project",
    )(x, mod, pre_g, w_q, qg2, d3, _decay_selectors(query_side=True))


def _attn_body(q_ref, k_ref, vt_ref, o_ref, acc_scr, *s_scr):
    n_h = q_ref.shape[0]
    tk = vt_ref.shape[3]
    tq = 2 * tk
    nq = q_ref.shape[1] // tq
    heads = range(n_h)
    halves = (slice(0, tk), slice(tk, tq))
    kpos = lax.broadcasted_iota(jnp.int32, (tk, tk), 0)
    qpos = lax.broadcasted_iota(jnp.int32, (tk, tk), 1)
    tri = kpos <= qpos
    LO, HI = 0, 1

    def scores(qt, j, slot, g, valid):
        cms = []
        for h in heads:
            q = q_ref[h, pl.ds(pl.multiple_of(qt * tq + g * tk, tk), tk), :]
            k = k_ref[h, pl.ds(pl.multiple_of(j * tk, tk), tk), :]
            s = lax.dot_general(k, q, (((1,), (1,)), ((), ())), preferred_element_type=F32)
            if valid is not None:
                s = jnp.where(valid, s, -jnp.inf)
            s_scr[2 * h + slot][:, halves[g]] = s
            cms.append(jnp.max(s, axis=0, keepdims=True))
        return tuple(cms)

    def accumulate(j, slot, g, cmax, ml):
        out = []
        for h in heads:
            m_new = jnp.maximum(ml[h], cmax[h])
            alpha = jnp.exp2(ml[h] - m_new)
            p = jnp.exp2(s_scr[2 * h + slot][:, halves[g]] - m_new)
            pv = jnp.dot(vt_ref[h, j], p.astype(BF16), preferred_element_type=F32)
            acc_scr[h, :, halves[g]] = alpha * acc_scr[h, :, halves[g]] + pv
            out.append(m_new)
        return tuple(out)

    def pair(qi, p, carry, diag):
        (ml_lo, ml_hi), (c0_lo, c0_hi) = carry
        c1_lo = scores(qi, 2 * p + 1, 1, LO, None)
        ml_lo = accumulate(2 * p, 0, LO, c0_lo, ml_lo)
        c1_hi = scores(qi, 2 * p + 1, 1, HI, None)
        ml_hi = accumulate(2 * p, 0, HI, c0_hi, ml_hi)
        c0_lo = scores(qi, 2 * p + 2, 0, LO, tri if diag else None)
        ml_lo = accumulate(2 * p + 1, 1, LO, c1_lo, ml_lo)
        c0_hi = scores(qi, 2 * p + 2, 0, HI, None)
        ml_hi = accumulate(2 * p + 1, 1, HI, c1_hi, ml_hi)
        return (ml_lo, ml_hi), (c0_lo, c0_hi)

    def query_tile(qi, cm0):
        acc_scr[...] = jnp.zeros_like(acc_scr)
        fresh = tuple(jnp.full((1, tk), -jnp.inf, F32) for _ in heads)
        n_loop = jnp.maximum(qi - 1, 0)

        def two_pairs(t, c):
            return pair(qi, 2 * t + 1, pair(qi, 2 * t, c, False), False)

        carry = lax.fori_loop(0, n_loop // 2, two_pairs, ((fresh, fresh), cm0))

        def finish(carry, n_front):
            if n_front == 2:
                carry = pair(qi, qi - 2, carry, False)
            if n_front >= 1:
                carry = pair(qi, qi - 1, carry, True)
            (ml_lo, ml_hi), (c0_lo, c0_hi) = carry
            ml_lo = accumulate(2 * qi, 0, LO, c0_lo, ml_lo)
            c1_hi = scores(qi, 2 * qi + 1, 1, HI, tri)
            ml_hi = accumulate(2 * qi, 0, HI, c0_hi, ml_hi)
            nxt = jnp.minimum(qi + 1, nq - 1)
            n_lo = scores(nxt, 0, 0, LO, None)
            accumulate(2 * qi + 1, 1, HI, c1_hi, ml_hi)
            n_hi = scores(nxt, 0, 0, HI, None)
            for g in (LO, HI):
                o_t = jnp.concatenate(
                    [acc_scr[h, 0:HEAD_DIM, halves[g]] / acc_scr[h, HEAD_DIM:HEAD_DIM + 1, halves[g]]
                     for h in heads], axis=0)
                o_ref[pl.ds(pl.multiple_of((2 * qi + g) * tk, tk), tk), :] = o_t.T
            return n_lo, n_hi

        n_front = jnp.minimum(qi, 1) + (n_loop & 1)
        return lax.switch(n_front, [functools.partial(finish, n_front=n) for n in range(3)], carry)

    lax.fori_loop(0, nq, query_tile, (scores(0, 0, 0, LO, tri), scores(0, 0, 0, HI, None)))


def _fox_attention(q_aug, k_aug, vt_aug):
    b, h, s, _ = k_aug.shape
    tk = TOKEN_TILE
    nk = s // tk
    hp = ATTN_HEADS_PER_STEP
    return _pallas(
        _attn_body,
        grid=(b, h // hp),
        in_specs=[
            pl.BlockSpec((None, hp, s, LANES), lambda i, j: (i, j, 0, 0)),
            pl.BlockSpec((None, hp, s, LANES), lambda i, j: (i, j, 0, 0)),
            pl.BlockSpec((None, hp, nk, V_ROWS, tk), lambda i, j: (i, j, 0, 0, 0)),
        ],
        out_specs=pl.BlockSpec((None, s, hp * HEAD_DIM), lambda i, j: (i, 0, j)),
        out_shape=jax.ShapeDtypeStruct((b, s, h * HEAD_DIM), F32),
        scratch_shapes=[pltpu.VMEM((hp, V_ROWS, ATTN_Q_TILE), F32)]
        + [pltpu.VMEM((tk, ATTN_Q_TILE), F32)] * (2 * hp),
        name="fox_attention",
    )(q_aug, k_aug, vt_aug)


def _out_body(x_ref, mod_ref, pre_ref, post_ref, a_ref, wg_ref, wo_ref, o_ref):
    x = x_ref[...]
    h = _prenorm(x, pre_ref[...], mod_ref[0:1, :], mod_ref[1:2, :]).astype(BF16)
    gate = _sigmoid(jnp.dot(h, wg_ref[...], preferred_element_type=F32))
    y = jnp.dot((a_ref[...] * gate).astype(BF16), wo_ref[...], preferred_element_type=F32)
    o_ref[...] = _gated_residual(x, y, post_ref[...], mod_ref[2:3, :])


def _fox_output(x, mod, pre_g, post_g, o_t, w_g, w_o):
    b, s, d = x.shape
    tm = WIDE_TOKEN_TILE
    tok = pl.BlockSpec((None, tm, d), lambda i, j: (i, j, 0))
    return _pallas(
        _out_body,
        grid=(b, s // tm),
        in_specs=[
            tok,
            pl.BlockSpec((None, 6, d), lambda i, j: (i, 0, 0)),
            _const_spec((1, d)), _const_spec((1, d)),
            tok,
            _const_spec((d, d)), _const_spec((d, d)),
        ],
        out_specs=tok,
        out_shape=jax.ShapeDtypeStruct((b, s, d), F32),
        name="fox_output",
    )(x, mod, pre_g, post_g, o_t, w_g, w_o)


def kernel(x, c, ada_w, ada_b, pre_mix_g, post_mix_g, pre_ffn_g, post_ffn_g, ffn_w_gu, ffn_w_down,
           a_w_in, a_b_in, a_ln_g, a_ln_b, a_w_s, a_b_s, a_w_out, kv_ada_w, kv_ada_b, kv_norm_g,
           kv_w, kv_b_f, k_norm_g, b_w_qg, b_q_norm_g, b_w_o):
    b, s, d = x.shape
    depth = ada_w.shape[0]
    n_a = a_w_in.shape[0]
    assert d == N_HEADS * HEAD_DIM and s % ATTN_Q_TILE == 0

    mods = _modulation(c, ada_w, ada_b).reshape(depth, b, 6, d)
    kvmod = _modulation(c, kv_ada_w[None], kv_ada_b[None]).reshape(b, 2, d)

    row = lambda v: v.reshape(1, -1)
    q_scale = LOG2E * HEAD_DIM ** -0.5

    ffn_gu, ffn_down = ffn_w_gu.astype(BF16), ffn_w_down.astype(BF16)
    gmlp_in, gmlp_out = _lane_padded(a_w_in), _lane_padded(a_w_out)
    fox_qg, fox_o, kvf = b_w_qg.astype(BF16), b_w_o.astype(BF16), kv_w.astype(BF16)

    for layer in range(depth):
        mod = mods[layer]
        if layer < n_a:
            i = layer
            bs_full = jnp.repeat(a_b_s[i].T, CHUNK, axis=1)
            x = _gmlp_layer(x, mod, row(pre_mix_g[layer]), row(post_mix_g[layer]), gmlp_in,
                            row(a_b_in[i]), row(a_ln_g[i]), row(a_ln_b[i]), a_w_s[i], bs_full, gmlp_out, i)
        else:
            j = layer - n_a
            qg2 = row(jnp.tile(b_q_norm_g[j] * q_scale, 2))
            q_aug = _q_project(x, mod, row(pre_mix_g[layer]), fox_qg[j][:, :d], qg2, d3)
            o_t = _fox_attention(q_aug, k_aug, vt_aug)
            x = _fox_output(x, mod, row(pre_mix_g[layer]), row(post_mix_g[layer]), o_t,
                            fox_qg[j][:, d:], fox_o[j])
        x = _ffn_layer(x, mod, row(pre_ffn_g[layer]), row(post_ffn_g[layer]), ffn_gu, ffn_down, layer)
        if layer == n_a - 1:
            w_f = kv_w[:, 2 * d:]
            pad = jnp.zeros((d, LANES - 3 * N_HEADS), F32)
            w_f3 = jnp.concatenate([w_f, w_f, w_f, pad], axis=1).astype(BF16)
            b_f3 = row(jnp.concatenate([kv_b_f, kv_b_f, kv_b_f, jnp.zeros((LANES - 3 * N_HEADS,), F32)]))
            k_aug, vt_aug, d3 = _kv_project(
                x, kvmod, row(kv_norm_g), kvf[:, :d], kvf[:, d:2 * d], w_f3, b_f3,
                row(jnp.tile(k_norm_g, 2)))
    return x
```

```python
import functools
import math

import jax
import jax.numpy as jnp
from jax import lax
from jax.experimental import pallas as pl
from jax.experimental.pallas import tpu as pltpu

F32 = jnp.float32
BF16 = jnp.bfloat16

EPS = 1e-6
N_HEADS = 16
HEAD_DIM = 64
CHUNK = 128
GROUPS = 16
LOG2E = 1.4426950408889634
LANES = 128
MXU_N = 256
TOKEN_TILE = 512
WIDE_TOKEN_TILE = 2 * TOKEN_TILE
ATTN_Q_TILE = 2 * TOKEN_TILE
ATTN_HEADS_PER_STEP = 2
F32_SUBLANES = 8
BF16_SUBLANES = 16
MOD_COLS = 1024
V_ROWS = HEAD_DIM + BF16_SUBLANES
V7X_COMPILER_VMEM_RESERVE = 6 * 1024 * 1024


def _sigmoid(x):
    return 1.0 / (1.0 + jnp.exp(-x))


def _unit_rms(x):
    return x * lax.rsqrt(jnp.mean(x * x, axis=-1, keepdims=True) + EPS)


def _prenorm(x, g, shift, scale):
    return _unit_rms(x) * (g * (1.0 + scale)) + shift


def _gated_residual(x, y, g, gate):
    return x + _unit_rms(y) * (g * gate)


def _split3(x):
    hi = x.astype(BF16).astype(F32)
    r = x - hi
    mid = r.astype(BF16).astype(F32)
    lo = (r - mid).astype(BF16).astype(F32)
    return hi, mid, lo


def _const_spec(shape):
    n = len(shape)
    return pl.BlockSpec(shape, lambda *_: (0,) * n, pipeline_mode=pl.Buffered(1))


def _slab_spec(block, index):
    return pl.BlockSpec(block, lambda *_: index, pipeline_mode=pl.Buffered(1))


def _lane_padded(w):
    return jnp.pad(w, ((0, 0),) * (w.ndim - 1) + ((0, LANES),)).astype(BF16)


def _block_bytes(spec, dtype):
    buffers = 2 if spec.pipeline_mode is None else spec.pipeline_mode.buffer_count
    return math.prod(1 if n is None else n for n in spec.block_shape) * jnp.dtype(dtype).itemsize * buffers


def _pallas(body, *, grid, in_specs, out_specs, out_shape, scratch_shapes=(), name):
    multi = isinstance(out_shape, (list, tuple))
    outs = zip(out_specs, out_shape) if multi else [(out_specs, out_shape)]
    need = sum(_block_bytes(s, o.dtype) for s, o in outs)
    need += sum(math.prod(m.shape) * jnp.dtype(m.dtype).itemsize for m in scratch_shapes)

    def call(*inputs):
        windows = sum(_block_bytes(s, a.dtype) for s, a in zip(in_specs, inputs, strict=True))
        return pl.pallas_call(
            body, grid=grid, in_specs=in_specs, out_specs=out_specs, out_shape=out_shape,
            scratch_shapes=list(scratch_shapes),
            compiler_params=pltpu.CompilerParams(
                dimension_semantics=("arbitrary",) * len(grid),
                vmem_limit_bytes=need + windows + V7X_COMPILER_VMEM_RESERVE),
            name=name)(*inputs)

    return call


def _mod_body(c_ref, w_ref, b_ref, o_ref):
    c = c_ref[...]
    ca = (c * _sigmoid(c)).astype(BF16)
    o_ref[...] = jnp.dot(ca, w_ref[...].astype(BF16), preferred_element_type=F32) + b_ref[...]


def _modulation(c, w, b):
    n_l, d, n = w.shape
    batch = c.shape[0]
    rows = -(-batch // F32_SUBLANES) * F32_SUBLANES
    nb = MOD_COLS
    out = _pallas(
        _mod_body,
        grid=(n_l, n // nb),
        in_specs=[
            pl.BlockSpec((rows, d), lambda l, j: (0, 0)),
            pl.BlockSpec((None, d, nb), lambda l, j: (l, 0, j)),
            pl.BlockSpec((None, 1, nb), lambda l, j: (l, 0, j)),
        ],
        out_specs=pl.BlockSpec((None, rows, nb), lambda l, j: (l, 0, j)),
        out_shape=jax.ShapeDtypeStruct((n_l, rows, n), F32),
        name="modulation",
    )(jnp.pad(c, ((0, rows - batch), (0, 0))), w, b.reshape(n_l, 1, n))
    return out[:, :batch]


def _gmlp_body(x_ref, mod_ref, pre_ref, post_ref, win_ref, bin_ref, lng_ref, lnb_ref, ws_ref,
               bs_ref, wout_ref, o_ref, h_scr, u_scr, v_scr, vn_scr, y_scr):
    tm, d = x_ref.shape
    gw = u_scr.shape[1]
    x = x_ref[...]
    h_scr[...] = _prenorm(x, pre_ref[...], mod_ref[0:1, :], mod_ref[1:2, :]).astype(BF16)

    k_gelu = 0.7978845608028654
    nc = 2 * MXU_N
    for c0 in range(0, 2 * gw, nc):
        z = jnp.dot(h_scr[...], win_ref[:, c0:c0 + nc], preferred_element_type=F32)
        z = z + bin_ref[:, c0:c0 + nc]
        z = z * (0.5 * (1.0 + jnp.tanh(k_gelu * (z + 0.044715 * (z * z * z)))))
        if c0 < gw:
            u_scr[:, c0:c0 + nc] = z
        else:
            v_scr[:, c0 - gw:c0 - gw + nc] = z

    v = v_scr[...]
    mu = jnp.mean(v, axis=-1, keepdims=True)
    vc = v - mu
    rstd = lax.rsqrt(jnp.mean(vc * vc, axis=-1, keepdims=True) + EPS)
    vn_scr[...] = (vc * rstd * lng_ref[...] + lnb_ref[...]).astype(BF16)

    n_chunks = tm // CHUNK
    row = lax.broadcasted_iota(jnp.int32, (CHUNK, CHUNK), 0)
    col = lax.broadcasted_iota(jnp.int32, (CHUNK, CHUNK), 1)
    causal = col <= row
    for g in range(GROUPS):
        gs = slice(g * CHUNK, (g + 1) * CHUNK)
        ws = jnp.where(causal, ws_ref[g], 0.0).astype(BF16)
        rhs = jnp.concatenate(
            [vn_scr[c * CHUNK:(c + 1) * CHUNK, gs] for c in range(n_chunks)], axis=1)
        sp = jnp.dot(ws, rhs, preferred_element_type=F32)
        for c in range(n_chunks):
            cs = slice(c * CHUNK, (c + 1) * CHUNK)
            y_scr[cs, gs] = (u_scr[cs, gs] * (sp[:, cs] + bs_ref[:, gs])).astype(BF16)

    y = jnp.dot(y_scr[...], wout_ref[:, 0:d], preferred_element_type=F32)
    o_ref[...] = _gated_residual(x, y, post_ref[...], mod_ref[2:3, :])


def _gmlp_layer(x, mod, pre_g, post_g, w_in, b_in, ln_g, ln_b, w_s, bs_full, w_out, layer):
    b, s, d = x.shape
    gw = w_out.shape[1]
    tm = TOKEN_TILE
    tok = pl.BlockSpec((None, tm, d), lambda i, j: (i, j, 0))
    return _pallas(
        _gmlp_body,
        grid=(b, s // tm),
        in_specs=[
            tok,
            pl.BlockSpec((None, 6, d), lambda i, j: (i, 0, 0)),
            _const_spec((1, d)), _const_spec((1, d)),
            _slab_spec((None,) + w_in.shape[1:], (layer, 0, 0)), _const_spec((1, 2 * gw)),
            _const_spec((1, gw)), _const_spec((1, gw)),
            _const_spec((GROUPS, CHUNK, CHUNK)), _const_spec((CHUNK, gw)),
            _slab_spec((None,) + w_out.shape[1:], (layer, 0, 0)),
        ],
        out_specs=tok,
        out_shape=jax.ShapeDtypeStruct((b, s, d), F32),
        scratch_shapes=[
            pltpu.VMEM((tm, d), BF16), pltpu.VMEM((tm, gw), F32), pltpu.VMEM((tm, gw), F32),
            pltpu.VMEM((tm, gw), BF16), pltpu.VMEM((tm, gw), BF16),
        ],
        name="gmlp_mixer",
    )(x, mod, pre_g, post_g, w_in, b_in, ln_g, ln_b, w_s, bs_full, w_out)


def _ffn_body(x_ref, mod_ref, pre_ref, post_ref, wgu_ref, wd_ref, o_ref, h_scr, a_scr):
    f = a_scr.shape[1]
    x = x_ref[...]
    h_scr[...] = _prenorm(x, pre_ref[...], mod_ref[3:4, :], mod_ref[4:5, :]).astype(BF16)
    for c0 in range(0, f, MXU_N):
        g = jnp.dot(h_scr[...], wgu_ref[:, c0:c0 + MXU_N], preferred_element_type=F32)
        u = jnp.dot(h_scr[...], wgu_ref[:, f + c0:f + c0 + MXU_N], preferred_element_type=F32)
        a_scr[:, c0:c0 + MXU_N] = (g * _sigmoid(g) * u).astype(BF16)
    y = jnp.dot(a_scr[...], wd_ref[...], preferred_element_type=F32)
    o_ref[...] = _gated_residual(x, y, post_ref[...], mod_ref[5:6, :])


def _ffn_layer(x, mod, pre_g, post_g, w_gu, w_down, layer):
    b, s, d = x.shape
    f = w_down.shape[1]
    tm = WIDE_TOKEN_TILE
    tok = pl.BlockSpec((None, tm, d), lambda i, j: (i, j, 0))
    return _pallas(
        _ffn_body,
        grid=(b, s // tm),
        in_specs=[
            tok,
            pl.BlockSpec((None, 6, d), lambda i, j: (i, 0, 0)),
            _const_spec((1, d)), _const_spec((1, d)),
            _slab_spec((None, d, 2 * f), (layer, 0, 0)), _slab_spec((None, f, d), (layer, 0, 0)),
        ],
        out_specs=tok,
        out_shape=jax.ShapeDtypeStruct((b, s, d), F32),
        scratch_shapes=[pltpu.VMEM((tm, d), BF16), pltpu.VMEM((tm, f), BF16)],
        name="swiglu_ffn",
    )(x, mod, pre_g, post_g, w_gu, w_down)


def _head_pair_norm(t, gain2):
    lane = lax.broadcasted_iota(jnp.int32, t.shape, 1)
    sq = t * t
    first = lane < HEAD_DIM
    s0 = jnp.sum(jnp.where(first, sq, 0.0), axis=-1, keepdims=True)
    s1 = jnp.sum(jnp.where(first, 0.0, sq), axis=-1, keepdims=True)
    rs = jnp.where(first, lax.rsqrt(s0 / HEAD_DIM + EPS), lax.rsqrt(s1 / HEAD_DIM + EPS))
    return t * rs * gain2


def _decay_selectors(query_side):
    row = lax.broadcasted_iota(jnp.int32, (LANES, N_HEADS * LANES), 0)
    col = lax.broadcasted_iota(jnp.int32, (LANES, N_HEADS * LANES), 1)
    head, off = col // LANES, col % LANES - HEAD_DIM
    first, second = (off >= 0) & (off < 3), (off >= 3) & (off < 6)
    piece_cols, piece_idx = (first, off) if query_side else (second, off - 3)
    one_cols = second if query_side else first
    is_piece = piece_cols & (row == piece_idx * N_HEADS + head)
    is_one = one_cols & (row == 3 * N_HEADS)
    sel = jnp.where(is_piece, 1.0 if query_side else -1.0, jnp.where(is_one, 1.0, 0.0))
    return sel.astype(BF16)


def _with_tail(t, tail):
    lane = lax.broadcasted_iota(jnp.int32, t.shape, 1)
    return jnp.where(lane < HEAD_DIM, t, tail)


def _kv_body(x_ref, mod_ref, g_ref, wk_ref, wv_ref, wf_ref, bf_ref, kg_ref, sel_ref,
             k_out, vt_out, d_out, h_scr, carry_scr):
    tm, d = x_ref.shape

    @pl.when(pl.program_id(1) == 0)
    def _():
        carry_scr[...] = jnp.zeros_like(carry_scr)

    h_scr[...] = _prenorm(x_ref[...], g_ref[...], mod_ref[0:1, :], mod_ref[1:2, :]).astype(BF16)

    lane = lax.broadcasted_iota(jnp.int32, (tm, LANES), 1)
    fl = jnp.dot(h_scr[...], wf_ref[...], preferred_element_type=F32) + bf_ref[...]
    z = -fl
    ls = -(jnp.maximum(z, 0.0) + jnp.log1p(jnp.exp(-jnp.abs(z))))
    hi, mid, lo = _split3(ls)
    pieces = jnp.where(lane < N_HEADS, hi, jnp.where(lane < 2 * N_HEADS, mid, jnp.where(
        lane < 3 * N_HEADS, lo, 0.0))).astype(BF16)
    r = lax.broadcasted_iota(jnp.int32, (tm, tm), 0)
    c = lax.broadcasted_iota(jnp.int32, (tm, tm), 1)
    tri = jnp.where(c <= r, 1.0, 0.0).astype(BF16)
    cs = jnp.dot(tri, pieces, preferred_element_type=F32)
    tot = cs + pltpu.roll(cs, LANES - N_HEADS, axis=1) + pltpu.roll(cs, LANES - 2 * N_HEADS, axis=1)
    dc = jnp.where(lane < N_HEADS, tot + carry_scr[...], 0.0)
    carry_scr[...] = dc[tm - 1:tm, :]
    hi, mid, lo = _split3(dc * LOG2E)
    d3 = (hi + pltpu.roll(mid, N_HEADS, axis=1) + pltpu.roll(lo, 2 * N_HEADS, axis=1)
          + jnp.where(lane == 3 * N_HEADS, 1.0, 0.0)).astype(BF16)
    d_out[...] = d3

    kg2 = kg_ref[...]
    ones_rows = jnp.where(
        lax.broadcasted_iota(jnp.int32, (V_ROWS - HEAD_DIM, tm), 0) == 0, 1.0, 0.0)
    heads_per_chunk = MXU_N // HEAD_DIM
    for c4 in range(d // MXU_N):
        kc = jnp.dot(h_scr[...], wk_ref[:, c4 * MXU_N:(c4 + 1) * MXU_N], preferred_element_type=F32)
        vc = jnp.dot(h_scr[...], wv_ref[:, c4 * MXU_N:(c4 + 1) * MXU_N], preferred_element_type=F32)
        tails = jnp.dot(d3, sel_ref[:, c4 * heads_per_chunk * LANES:(c4 + 1) * heads_per_chunk * LANES],
                        preferred_element_type=F32)
        for half in range(2):
            kn = _head_pair_norm(kc[:, half * LANES:(half + 1) * LANES], kg2)
            vt = vc[:, half * LANES:(half + 1) * LANES].T
            for sub, src in ((0, kn), (1, pltpu.roll(kn, HEAD_DIM, axis=1))):
                loc = 2 * half + sub
                hd = c4 * heads_per_chunk + loc
                k_out[hd] = _with_tail(src, tails[:, loc * LANES:(loc + 1) * LANES]).astype(BF16)
                vt_out[hd] = jnp.concatenate(
                    [vt[sub * HEAD_DIM:(sub + 1) * HEAD_DIM, :], ones_rows], axis=0).astype(BF16)


def _kv_project(x, kvmod, norm_g, w_k, w_v, w_f3, b_f3, kg2):
    b, s, d = x.shape
    tm = TOKEN_TILE
    return _pallas(
        _kv_body,
        grid=(b, s // tm),
        in_specs=[
            pl.BlockSpec((None, tm, d), lambda i, j: (i, j, 0)),
            pl.BlockSpec((None, 2, d), lambda i, j: (i, 0, 0)),
            _const_spec((1, d)),
            _const_spec((d, d)), _const_spec((d, d)), _const_spec((d, LANES)),
            _const_spec((1, LANES)), _const_spec((1, LANES)), _const_spec((LANES, N_HEADS * LANES)),
        ],
        out_specs=[
            pl.BlockSpec((None, N_HEADS, tm, LANES), lambda i, j: (i, 0, j, 0)),
            pl.BlockSpec((None, N_HEADS, None, V_ROWS, tm), lambda i, j: (i, 0, j, 0, 0)),
            pl.BlockSpec((None, tm, LANES), lambda i, j: (i, j, 0)),
        ],
        out_shape=[
            jax.ShapeDtypeStruct((b, N_HEADS, s, LANES), BF16),
            jax.ShapeDtypeStruct((b, N_HEADS, s // tm, V_ROWS, tm), BF16),
            jax.ShapeDtypeStruct((b, s, LANES), BF16),
        ],
        scratch_shapes=[pltpu.VMEM((tm, d), BF16), pltpu.VMEM((1, LANES), F32)],
        name="kv_project",
    )(x, kvmod, norm_g, w_k, w_v, w_f3, b_f3, kg2, _decay_selectors(query_side=False))


def _q_body(x_ref, mod_ref, pre_ref, wq_ref, qg_ref, d_ref, sel_ref, q_out, h_scr):
    tm, d = x_ref.shape
    h_scr[...] = _prenorm(x_ref[...], pre_ref[...], mod_ref[0:1, :], mod_ref[1:2, :]).astype(BF16)
    d3 = d_ref[...]
    qg2 = qg_ref[...]
    heads_per_chunk = MXU_N // HEAD_DIM
    for c4 in range(d // MXU_N):
        qc = jnp.dot(h_scr[...], wq_ref[:, c4 * MXU_N:(c4 + 1) * MXU_N], preferred_element_type=F32)
        tails = jnp.dot(d3, sel_ref[:, c4 * heads_per_chunk * LANES:(c4 + 1) * heads_per_chunk * LANES],
                        preferred_element_type=F32)
        for half in range(2):
            qn = _head_pair_norm(qc[:, half * LANES:(half + 1) * LANES], qg2)
            for sub, src in ((0, qn), (1, pltpu.roll(qn, HEAD_DIM, axis=1))):
                loc = 2 * half + sub
                qa = _with_tail(src, tails[:, loc * LANES:(loc + 1) * LANES])
                q_out[c4 * heads_per_chunk + loc] = qa.astype(BF16)


def _q_project(x, mod, pre_g, w_q, qg2, d3):
    b, s, d = x.shape
    tm = TOKEN_TILE
    return _pallas(
        _q_body,
        grid=(b, s // tm),
        in_specs=[
            pl.BlockSpec((None, tm, d), lambda i, j: (i, j, 0)),
            pl.BlockSpec((None, 6, d), lambda i, j: (i, 0, 0)),
            _const_spec((1, d)), _const_spec((d, d)), _const_spec((1, LANES)),
            pl.BlockSpec((None, tm, LANES), lambda i, j: (i, j, 0)),
            _const_spec((LANES, N_HEADS * LANES)),
        ],
        out_specs=pl.BlockSpec((None, N_HEADS, tm, LANES), lambda i, j: (i, 0, j, 0)),
        out_shape=jax.ShapeDtypeStruct((b, N_HEADS, s, LANES), BF16),
        scratch_shapes=[pltpu.VMEM((tm, d), BF16)],
        name="q_project",
    )(x, mod, pre_g, w_q, qg2, d3, _decay_selectors(query_side=True))


def _attn_body(q_ref, k_ref, vt_ref, o_ref, acc_scr, *s_scr):
    n_h = q_ref.shape[0]
    tk = vt_ref.shape[3]
    tq = 2 * tk
    nq = q_ref.shape[1] // tq
    heads = range(n_h)
    halves = (slice(0, tk), slice(tk, tq))
    kpos = lax.broadcasted_iota(jnp.int32, (tk, tk), 0)
    qpos = lax.broadcasted_iota(jnp.int32, (tk, tk), 1)
    tri = kpos <= qpos
    LO, HI = 0, 1

    def scores_head(h, qt, j, slot, g, valid):
        q = q_ref[h, pl.ds(pl.multiple_of(qt * tq + g * tk, tk), tk), :]
        k = k_ref[h, pl.ds(pl.multiple_of(j * tk, tk), tk), :]
        s = lax.dot_general(k, q, (((1,), (1,)), ((), ())), preferred_element_type=F32)
        if valid is not None:
            s = jnp.where(valid, s, -jnp.inf)
        s_scr[2 * h + slot][:, halves[g]] = s
        return jnp.max(s, axis=0, keepdims=True)

    def accumulate_head(h, j, slot, g, cmax, m):
        m_new = jnp.maximum(m, cmax)
        alpha = jnp.exp2(m - m_new)
        p = jnp.exp2(s_scr[2 * h + slot][:, halves[g]] - m_new)
        pv = jnp.dot(vt_ref[h, j], p.astype(BF16), preferred_element_type=F32)
        acc_scr[h, :, halves[g]] = alpha * acc_scr[h, :, halves[g]] + pv
        return m_new

    def scores(qt, j, slot, g, valid):
        return tuple(scores_head(h, qt, j, slot, g, valid) for h in heads)

    def accumulate(j, slot, g, cmax, ml):
        return tuple(accumulate_head(h, j, slot, g, cmax[h], ml[h]) for h in heads)

    def pair(qi, p, carry, diag):
        (ml_lo, ml_hi), (c0_lo, c0_hi) = carry
        out = []
        for h in heads:
            m_lo, m_hi = ml_lo[h], ml_hi[h]
            c1_lo = scores_head(h, qi, 2 * p + 1, 1, LO, None)
            m_lo = accumulate_head(h, 2 * p, 0, LO, c0_lo[h], m_lo)
            c1_hi = scores_head(h, qi, 2 * p + 1, 1, HI, None)
            m_hi = accumulate_head(h, 2 * p, 0, HI, c0_hi[h], m_hi)
            n_lo = scores_head(h, qi, 2 * p + 2, 0, LO, tri if diag else None)
            m_lo = accumulate_head(h, 2 * p + 1, 1, LO, c1_lo, m_lo)
            n_hi = scores_head(h, qi, 2 * p + 2, 0, HI, None)
            m_hi = accumulate_head(h, 2 * p + 1, 1, HI, c1_hi, m_hi)
            out.append((m_lo, m_hi, n_lo, n_hi))
        m_lo, m_hi, n_lo, n_hi = (tuple(o[i] for o in out) for i in range(4))
        return (m_lo, m_hi), (n_lo, n_hi)

    def query_tile(qi, cm0):
        acc_scr[...] = jnp.zeros_like(acc_scr)
        fresh = tuple(jnp.full((1, tk), -jnp.inf, F32) for _ in heads)
        n_loop = jnp.maximum(qi - 1, 0)

        def two_pairs(t, c):
            return pair(qi, 2 * t + 1, pair(qi, 2 * t, c, False), False)

        carry = lax.fori_loop(0, n_loop // 2, two_pairs, ((fresh, fresh), cm0))

        def finish(carry, n_front):
            if n_front == 2:
                carry = pair(qi, qi - 2, carry, False)
            if n_front >= 1:
                carry = pair(qi, qi - 1, carry, True)
            (ml_lo, ml_hi), (c0_lo, c0_hi) = carry
            ml_lo = accumulate(2 * qi, 0, LO, c0_lo, ml_lo)
            c1_hi = scores(qi, 2 * qi + 1, 1, HI, tri)
            ml_hi = accumulate(2 * qi, 0, HI, c0_hi, ml_hi)
            nxt = jnp.minimum(qi + 1, nq - 1)
            n_lo = scores(nxt, 0, 0, LO, None)
            accumulate(2 * qi + 1, 1, HI, c1_hi, ml_hi)
            n_hi = scores(nxt, 0, 0, HI, None)
            for g in (LO, HI):
                o_t = jnp.concatenate(
                    [acc_scr[h, 0:HEAD_DIM, halves[g]] / acc_scr[h, HEAD_DIM:HEAD_DIM + 1, halves[g]]
                     for h in heads], axis=0)
                o_ref[pl.ds(pl.multiple_of((2 * qi + g) * tk, tk), tk), :] = o_t.T
            return n_lo, n_hi

        n_front = jnp.minimum(qi, 1) + (n_loop & 1)
        return lax.switch(n_front, [functools.partial(finish, n_front=n) for n in range(3)], carry)

    lax.fori_loop(0, nq, query_tile, (scores(0, 0, 0, LO, tri), scores(0, 0, 0, HI, None)))


def _fox_attention(q_aug, k_aug, vt_aug):
    b, h, s, _ = k_aug.shape
    tk = TOKEN_TILE
    nk = s // tk
    hp = ATTN_HEADS_PER_STEP
    return _pallas(
        _attn_body,
        grid=(b, h // hp),
        in_specs=[
            pl.BlockSpec((None, hp, s, LANES), lambda i, j: (i, j, 0, 0)),
            pl.BlockSpec((None, hp, s, LANES), lambda i, j: (i, j, 0, 0)),
            pl.BlockSpec((None, hp, nk, V_ROWS, tk), lambda i, j: (i, j, 0, 0, 0)),
        ],
        out_specs=pl.BlockSpec((None, s, hp * HEAD_DIM), lambda i, j: (i, 0, j)),
        out_shape=jax.ShapeDtypeStruct((b, s, h * HEAD_DIM), F32),
        scratch_shapes=[pltpu.VMEM((hp, V_ROWS, ATTN_Q_TILE), F32)]
        + [pltpu.VMEM((tk, ATTN_Q_TILE), F32)] * (2 * hp),
        name="fox_attention",
    )(q_aug, k_aug, vt_aug)


def _out_body(x_ref, mod_ref, pre_ref, post_ref, a_ref, wg_ref, wo_ref, o_ref):
    x = x_ref[...]
    h = _prenorm(x, pre_ref[...], mod_ref[0:1, :], mod_ref[1:2, :]).astype(BF16)
    gate = _sigmoid(jnp.dot(h, wg_ref[...], preferred_element_type=F32))
    y = jnp.dot((a_ref[...] * gate).astype(BF16), wo_ref[...], preferred_element_type=F32)
    o_ref[...] = _gated_residual(x, y, post_ref[...], mod_ref[2:3, :])


def _fox_output(x, mod, pre_g, post_g, o_t, w_g, w_o):
    b, s, d = x.shape
    tm = WIDE_TOKEN_TILE
    tok = pl.BlockSpec((None, tm, d), lambda i, j: (i, j, 0))
    return _pallas(
        _out_body,
        grid=(b, s // tm),
        in_specs=[
            tok,
            pl.BlockSpec((None, 6, d), lambda i, j: (i, 0, 0)),
            _const_spec((1, d)), _const_spec((1, d)),
            tok,
            _const_spec((d, d)), _const_spec((d, d)),
        ],
        out_specs=tok,
        out_shape=jax.ShapeDtypeStruct((b, s, d), F32),
        name="fox_output",
    )(x, mod, pre_g, post_g, o_t, w_g, w_o)


def kernel(x, c, ada_w, ada_b, pre_mix_g, post_mix_g, pre_ffn_g, post_ffn_g, ffn_w_gu, ffn_w_down,
           a_w_in, a_b_in, a_ln_g, a_ln_b, a_w_s, a_b_s, a_w_out, kv_ada_w, kv_ada_b, kv_norm_g,
           kv_w, kv_b_f, k_norm_g, b_w_qg, b_q_norm_g, b_w_o):
    b, s, d = x.shape
    depth = ada_w.shape[0]
    n_a = a_w_in.shape[0]
    assert d == N_HEADS * HEAD_DIM and s % ATTN_Q_TILE == 0

    mods = _modulation(c, ada_w, ada_b).reshape(depth, b, 6, d)
    kvmod = _modulation(c, kv_ada_w[None], kv_ada_b[None]).reshape(b, 2, d)

    row = lambda v: v.reshape(1, -1)
    q_scale = LOG2E * HEAD_DIM ** -0.5

    ffn_gu, ffn_down = ffn_w_gu.astype(BF16), ffn_w_down.astype(BF16)
    gmlp_in, gmlp_out = _lane_padded(a_w_in), _lane_padded(a_w_out)
    fox_qg, fox_o, kvf = b_w_qg.astype(BF16), b_w_o.astype(BF16), kv_w.astype(BF16)

    for layer in range(depth):
        mod = mods[layer]
        if layer < n_a:
            i = layer
            bs_full = jnp.repeat(a_b_s[i].T, CHUNK, axis=1)
            x = _gmlp_layer(x, mod, row(pre_mix_g[layer]), row(post_mix_g[layer]), gmlp_in,
                            row(a_b_in[i]), row(a_ln_g[i]), row(a_ln_b[i]), a_w_s[i], bs_full, gmlp_out, i)
        else:
            j = layer - n_a
            qg2 = row(jnp.tile(b_q_norm_g[j] * q_scale, 2))
            q_aug = _q_project(x, mod, row(pre_mix_g[layer]), fox_qg[j][:, :d], qg2, d3)
            o_t = _fox_attention(q_aug, k_aug, vt_aug)
            x = _fox_output(x, mod, row(pre_mix_g[layer]), row(post_mix_g[layer]), o_t,
                            fox_qg[j][:, d:], fox_o[j])
        x = _ffn_layer(x, mod, row(pre_ffn_g[layer]), row(post_ffn_g[layer]), ffn_gu, ffn_down, layer)
        if layer == n_a - 1:
            w_f = kv_w[:, 2 * d:]
            pad = jnp.zeros((d, LANES - 3 * N_HEADS), F32)
            w_f3 = jnp.concatenate([w_f, w_f, w_f, pad], axis=1).astype(BF16)
            b_f3 = row(jnp.concatenate([kv_b_f, kv_b_f, kv_b_f, jnp.zeros((LANES - 3 * N_HEADS,), F32)]))
            k_aug, vt_aug, d3 = _kv_project(
                x, kvmod, row(kv_norm_g), kvf[:, :d], kvf[:, d:2 * d], w_f3, b_f3,
                row(jnp.tile(k_norm_g, 2)))
    return x
```

```python
import functools
import math

import jax
import jax.numpy as jnp
from jax import lax
from jax.experimental import pallas as pl
from jax.experimental.pallas import tpu as pltpu

F32 = jnp.float32
BF16 = jnp.bfloat16

EPS = 1e-6
N_HEADS = 16
HEAD_DIM = 64
CHUNK = 128
GROUPS = 16
LOG2E = 1.4426950408889634
LANES = 128
MXU_N = 256
TOKEN_TILE = 512
WIDE_TOKEN_TILE = 2 * TOKEN_TILE
ATTN_Q_TILE = 2 * TOKEN_TILE
ATTN_HEADS_PER_STEP = 2
F32_SUBLANES = 8
BF16_SUBLANES = 16
MOD_COLS = 1024
V_ROWS = HEAD_DIM + BF16_SUBLANES
V7X_COMPILER_VMEM_RESERVE = 6 * 1024 * 1024


def _sigmoid(x):
    return 1.0 / (1.0 + jnp.exp(-x))


def _unit_rms(x):
    return x * lax.rsqrt(jnp.mean(x * x, axis=-1, keepdims=True) + EPS)


def _prenorm(x, g, shift, scale):
    return _unit_rms(x) * (g * (1.0 + scale)) + shift


def _gated_residual(x, y, g, gate):
    return x + _unit_rms(y) * (g * gate)


def _split3(x):
    hi = x.astype(BF16).astype(F32)
    r = x - hi
    mid = r.astype(BF16).astype(F32)
    lo = (r - mid).astype(BF16).astype(F32)
    return hi, mid, lo


def _const_spec(shape):
    n = len(shape)
    return pl.BlockSpec(shape, lambda *_: (0,) * n, pipeline_mode=pl.Buffered(1))


def _slab_spec(block, index):
    return pl.BlockSpec(block, lambda *_: index, pipeline_mode=pl.Buffered(1))


def _lane_padded(w):
    return jnp.pad(w, ((0, 0),) * (w.ndim - 1) + ((0, LANES),)).astype(BF16)


def _block_bytes(spec, dtype):
    buffers = 2 if spec.pipeline_mode is None else spec.pipeline_mode.buffer_count
    return math.prod(1 if n is None else n for n in spec.block_shape) * jnp.dtype(dtype).itemsize * buffers


def _pallas(body, *, grid, in_specs, out_specs, out_shape, scratch_shapes=(), name):
    multi = isinstance(out_shape, (list, tuple))
    outs = zip(out_specs, out_shape) if multi else [(out_specs, out_shape)]
    need = sum(_block_bytes(s, o.dtype) for s, o in outs)
    need += sum(math.prod(m.shape) * jnp.dtype(m.dtype).itemsize for m in scratch_shapes)

    def call(*inputs):
        windows = sum(_block_bytes(s, a.dtype) for s, a in zip(in_specs, inputs, strict=True))
        return pl.pallas_call(
            body, grid=grid, in_specs=in_specs, out_specs=out_specs, out_shape=out_shape,
            scratch_shapes=list(scratch_shapes),
            compiler_params=pltpu.CompilerParams(
                dimension_semantics=("arbitrary",) * len(grid),
                vmem_limit_bytes=need + windows + V7X_COMPILER_VMEM_RESERVE),
            name=name)(*inputs)

    return call


def _mod_body(c_ref, w_ref, b_ref, o_ref):
    c = c_ref[...]
    ca = (c * _sigmoid(c)).astype(BF16)
    o_ref[...] = jnp.dot(ca, w_ref[...].astype(BF16), preferred_element_type=F32) + b_ref[...]


def _modulation(c, w, b):
    n_l, d, n = w.shape
    batch = c.shape[0]
    rows = -(-batch // F32_SUBLANES) * F32_SUBLANES
    nb = MOD_COLS
    out = _pallas(
        _mod_body,
        grid=(n_l, n // nb),
        in_specs=[
            pl.BlockSpec((rows, d), lambda l, j: (0, 0)),
            pl.BlockSpec((None, d, nb), lambda l, j: (l, 0, j)),
            pl.BlockSpec((None, 1, nb), lambda l, j: (l, 0, j)),
        ],
        out_specs=pl.BlockSpec((None, rows, nb), lambda l, j: (l, 0, j)),
        out_shape=jax.ShapeDtypeStruct((n_l, rows, n), F32),
        name="modulation",
    )(jnp.pad(c, ((0, rows - batch), (0, 0))), w, b.reshape(n_l, 1, n))
    return out[:, :batch]


def _gmlp_body(x_ref, mod_ref, pre_ref, post_ref, win_ref, bin_ref, lng_ref, lnb_ref, ws_ref,
               bs_ref, wout_ref, o_ref, h_scr, u_scr, v_scr, vn_scr, y_scr):
    tm, d = x_ref.shape
    gw = u_scr.shape[1]
    x = x_ref[...]
    h_scr[...] = _prenorm(x, pre_ref[...], mod_ref[0:1, :], mod_ref[1:2, :]).astype(BF16)

    k_gelu = 0.7978845608028654
    nc = 2 * MXU_N
    for c0 in range(0, 2 * gw, nc):
        z = jnp.dot(h_scr[...], win_ref[:, c0:c0 + nc], preferred_element_type=F32)
        z = z + bin_ref[:, c0:c0 + nc]
        z = z * (0.5 * (1.0 + jnp.tanh(k_gelu * (z + 0.044715 * (z * z * z)))))
        if c0 < gw:
            u_scr[:, c0:c0 + nc] = z
        else:
            v_scr[:, c0 - gw:c0 - gw + nc] = z

    v = v_scr[...]
    mu = jnp.mean(v, axis=-1, keepdims=True)
    vc = v - mu
    rstd = lax.rsqrt(jnp.mean(vc * vc, axis=-1, keepdims=True) + EPS)
    vn_scr[...] = (vc * rstd * lng_ref[...] + lnb_ref[...]).astype(BF16)

    n_chunks = tm // CHUNK
    row = lax.broadcasted_iota(jnp.int32, (CHUNK, CHUNK), 0)
    col = lax.broadcasted_iota(jnp.int32, (CHUNK, CHUNK), 1)
    causal = col <= row
    for g in range(GROUPS):
        gs = slice(g * CHUNK, (g + 1) * CHUNK)
        ws = jnp.where(causal, ws_ref[g], 0.0).astype(BF16)
        rhs = jnp.concatenate(
            [vn_scr[c * CHUNK:(c + 1) * CHUNK, gs] for c in range(n_chunks)], axis=1)
        sp = jnp.dot(ws, rhs, preferred_element_type=F32)
        for c in range(n_chunks):
            cs = slice(c * CHUNK, (c + 1) * CHUNK)
            y_scr[cs, gs] = (u_scr[cs, gs] * (sp[:, cs] + bs_ref[:, gs])).astype(BF16)

    y = jnp.dot(y_scr[...], wout_ref[:, 0:d], preferred_element_type=F32)
    o_ref[...] = _gated_residual(x, y, post_ref[...], mod_ref[2:3, :])


def _gmlp_layer(x, mod, pre_g, post_g, w_in, b_in, ln_g, ln_b, w_s, bs_full, w_out, layer):
    b, s, d = x.shape
    gw = w_out.shape[1]
    tm = TOKEN_TILE
    tok = pl.BlockSpec((None, tm, d), lambda i, j: (i, j, 0))
    return _pallas(
        _gmlp_body,
        grid=(b, s // tm),
        in_specs=[
            tok,
            pl.BlockSpec((None, 6, d), lambda i, j: (i, 0, 0)),
            _const_spec((1, d)), _const_spec((1, d)),
            _slab_spec((None,) + w_in.shape[1:], (layer, 0, 0)), _const_spec((1, 2 * gw)),
            _const_spec((1, gw)), _const_spec((1, gw)),
            _const_spec((GROUPS, CHUNK, CHUNK)), _const_spec((CHUNK, gw)),
            _slab_spec((None,) + w_out.shape[1:], (layer, 0, 0)),
        ],
        out_specs=tok,
        out_shape=jax.ShapeDtypeStruct((b, s, d), F32),
        scratch_shapes=[
            pltpu.VMEM((tm, d), BF16), pltpu.VMEM((tm, gw), F32), pltpu.VMEM((tm, gw), F32),
            pltpu.VMEM((tm, gw), BF16), pltpu.VMEM((tm, gw), BF16),
        ],
        name="gmlp_mixer",
    )(x, mod, pre_g, post_g, w_in, b_in, ln_g, ln_b, w_s, bs_full, w_out)


def _ffn_body(x_ref, mod_ref, pre_ref, post_ref, wgu_ref, wd_ref, o_ref, h_scr, a_scr):
    f = a_scr.shape[1]
    x = x_ref[...]
    h_scr[...] = _prenorm(x, pre_ref[...], mod_ref[3:4, :], mod_ref[4:5, :]).astype(BF16)
    for c0 in range(0, f, MXU_N):
        g = jnp.dot(h_scr[...], wgu_ref[:, c0:c0 + MXU_N], preferred_element_type=F32)
        u = jnp.dot(h_scr[...], wgu_ref[:, f + c0:f + c0 + MXU_N], preferred_element_type=F32)
        a_scr[:, c0:c0 + MXU_N] = (g * _sigmoid(g) * u).astype(BF16)
    y = jnp.dot(a_scr[...], wd_ref[...], preferred_element_type=F32)
    o_ref[...] = _gated_residual(x, y, post_ref[...], mod_ref[5:6, :])


def _ffn_layer(x, mod, pre_g, post_g, w_gu, w_down, layer):
    b, s, d = x.shape
    f = w_down.shape[1]
    tm = WIDE_TOKEN_TILE
    tok = pl.BlockSpec((None, tm, d), lambda i, j: (i, j, 0))
    return _pallas(
        _ffn_body,
        grid=(b, s // tm),
        in_specs=[
            tok,
            pl.BlockSpec((None, 6, d), lambda i, j: (i, 0, 0)),
            _const_spec((1, d)), _const_spec((1, d)),
            _slab_spec((None, d, 2 * f), (layer, 0, 0)), _slab_spec((None, f, d), (layer, 0, 0)),
        ],
        out_specs=tok,
        out_shape=jax.ShapeDtypeStruct((b, s, d), F32),
        scratch_shapes=[pltpu.VMEM((tm, d), BF16), pltpu.VMEM((tm, f), BF16)],
        name="swiglu_ffn",
    )(x, mod, pre_g, post_g, w_gu, w_down)


def _head_pair_norm(t, gain2):
    lane = lax.broadcasted_iota(jnp.int32, t.shape, 1)
    sq = t * t
    first = lane < HEAD_DIM
    s0 = jnp.sum(jnp.where(first, sq, 0.0), axis=-1, keepdims=True)
    s1 = jnp.sum(jnp.where(first, 0.0, sq), axis=-1, keepdims=True)
    rs = jnp.where(first, lax.rsqrt(s0 / HEAD_DIM + EPS), lax.rsqrt(s1 / HEAD_DIM + EPS))
    return t * rs * gain2


def _decay_selectors(query_side):
    row = lax.broadcasted_iota(jnp.int32, (LANES, N_HEADS * LANES), 0)
    col = lax.broadcasted_iota(jnp.int32, (LANES, N_HEADS * LANES), 1)
    head, off = col // LANES, col % LANES - HEAD_DIM
    first, second = (off >= 0) & (off < 3), (off >= 3) & (off < 6)
    piece_cols, piece_idx = (first, off) if query_side else (second, off - 3)
    one_cols = second if query_side else first
    is_piece = piece_cols & (row == piece_idx * N_HEADS + head)
    is_one = one_cols & (row == 3 * N_HEADS)
    sel = jnp.where(is_piece, 1.0 if query_side else -1.0, jnp.where(is_one, 1.0, 0.0))
    return sel.astype(BF16)


def _with_tail(t, tail):
    lane = lax.broadcasted_iota(jnp.int32, t.shape, 1)
    return jnp.where(lane < HEAD_DIM, t, tail)


def _kv_body(x_ref, mod_ref, g_ref, wk_ref, wv_ref, wf_ref, bf_ref, kg_ref, sel_ref,
             k_out, vt_out, d_out, h_scr, carry_scr):
    tm, d = x_ref.shape

    @pl.when(pl.program_id(1) == 0)
    def _():
        carry_scr[...] = jnp.zeros_like(carry_scr)

    h_scr[...] = _prenorm(x_ref[...], g_ref[...], mod_ref[0:1, :], mod_ref[1:2, :]).astype(BF16)

    lane = lax.broadcasted_iota(jnp.int32, (tm, LANES), 1)
    fl = jnp.dot(h_scr[...], wf_ref[...], preferred_element_type=F32) + bf_ref[...]
    z = -fl
    ls = -(jnp.maximum(z, 0.0) + jnp.log1p(jnp.exp(-jnp.abs(z))))
    hi, mid, lo = _split3(ls)
    pieces = jnp.where(lane < N_HEADS, hi, jnp.where(lane < 2 * N_HEADS, mid, jnp.where(
        lane < 3 * N_HEADS, lo, 0.0))).astype(BF16)
    r = lax.broadcasted_iota(jnp.int32, (tm, tm), 0)
    c = lax.broadcasted_iota(jnp.int32, (tm, tm), 1)
    tri = jnp.where(c <= r, 1.0, 0.0).astype(BF16)
    cs = jnp.dot(tri, pieces, preferred_element_type=F32)
    tot = cs + pltpu.roll(cs, LANES - N_HEADS, axis=1) + pltpu.roll(cs, LANES - 2 * N_HEADS, axis=1)
    dc = jnp.where(lane < N_HEADS, tot + carry_scr[...], 0.0)
    carry_scr[...] = dc[tm - 1:tm, :]
    hi, mid, lo = _split3(dc * LOG2E)
    d3 = (hi + pltpu.roll(mid, N_HEADS, axis=1) + pltpu.roll(lo, 2 * N_HEADS, axis=1)
          + jnp.where(lane == 3 * N_HEADS, 1.0, 0.0)).astype(BF16)
    d_out[...] = d3

    kg2 = kg_ref[...]
    ones_rows = jnp.where(
        lax.broadcasted_iota(jnp.int32, (V_ROWS - HEAD_DIM, tm), 0) == 0, 1.0, 0.0)
    heads_per_chunk = MXU_N // HEAD_DIM
    for c4 in range(d // MXU_N):
        kc = jnp.dot(h_scr[...], wk_ref[:, c4 * MXU_N:(c4 + 1) * MXU_N], preferred_element_type=F32)
        vc = jnp.dot(h_scr[...], wv_ref[:, c4 * MXU_N:(c4 + 1) * MXU_N], preferred_element_type=F32)
        tails = jnp.dot(d3, sel_ref[:, c4 * heads_per_chunk * LANES:(c4 + 1) * heads_per_chunk * LANES],
                        preferred_element_type=F32)
        for half in range(2):
            kn = _head_pair_norm(kc[:, half * LANES:(half + 1) * LANES], kg2)
            vt = vc[:, half * LANES:(half + 1) * LANES].T
            for sub, src in ((0, kn), (1, pltpu.roll(kn, HEAD_DIM, axis=1))):
                loc = 2 * half + sub
                hd = c4 * heads_per_chunk + loc
                k_out[hd] = _with_tail(src, tails[:, loc * LANES:(loc + 1) * LANES]).astype(BF16)
                vt_out[hd] = jnp.concatenate(
                    [vt[sub * HEAD_DIM:(sub + 1) * HEAD_DIM, :], ones_rows], axis=0).astype(BF16)


def _kv_project(x, kvmod, norm_g, w_k, w_v, w_f3, b_f3, kg2):
    b, s, d = x.shape
    tm = TOKEN_TILE
    return _pallas(
        _kv_body,
        grid=(b, s // tm),
        in_specs=[
            pl.BlockSpec((None, tm, d), lambda i, j: (i, j, 0)),
            pl.BlockSpec((None, 2, d), lambda i, j: (i, 0, 0)),
            _const_spec((1, d)),
            _const_spec((d, d)), _const_spec((d, d)), _const_spec((d, LANES)),
            _const_spec((1, LANES)), _const_spec((1, LANES)), _const_spec((LANES, N_HEADS * LANES)),
        ],
        out_specs=[
            pl.BlockSpec((None, N_HEADS, tm, LANES), lambda i, j: (i, 0, j, 0)),
            pl.BlockSpec((None, N_HEADS, None, V_ROWS, tm), lambda i, j: (i, 0, j, 0, 0)),
            pl.BlockSpec((None, tm, LANES), lambda i, j: (i, j, 0)),
        ],
        out_shape=[
            jax.ShapeDtypeStruct((b, N_HEADS, s, LANES), BF16),
            jax.ShapeDtypeStruct((b, N_HEADS, s // tm, V_ROWS, tm), BF16),
            jax.ShapeDtypeStruct((b, s, LANES), BF16),
        ],
        scratch_shapes=[pltpu.VMEM((tm, d), BF16), pltpu.VMEM((1, LANES), F32)],
        name="kv_project",
    )(x, kvmod, norm_g, w_k, w_v, w_f3, b_f3, kg2, _decay_selectors(query_side=False))


def _q_body(x_ref, mod_ref, pre_ref, wq_ref, qg_ref, d_ref, sel_ref, q_out, h_scr):
    tm, d = x_ref.shape
    h_scr[...] = _prenorm(x_ref[...], pre_ref[...], mod_ref[0:1, :], mod_ref[1:2, :]).astype(BF16)
    d3 = d_ref[...]
    qg2 = qg_ref[...]
    heads_per_chunk = MXU_N // HEAD_DIM
    for c4 in range(d // MXU_N):
        qc = jnp.dot(h_scr[...], wq_ref[:, c4 * MXU_N:(c4 + 1) * MXU_N], preferred_element_type=F32)
        tails = jnp.dot(d3, sel_ref[:, c4 * heads_per_chunk * LANES:(c4 + 1) * heads_per_chunk * LANES],
                        preferred_element_type=F32)
        for half in range(2):
            qn = _head_pair_norm(qc[:, half * LANES:(half + 1) * LANES], qg2)
            for sub, src in ((0, qn), (1, pltpu.roll(qn, HEAD_DIM, axis=1))):
                loc = 2 * half + sub
                qa = _with_tail(src, tails[:, loc * LANES:(loc + 1) * LANES])
                q_out[c4 * heads_per_chunk + loc] = qa.astype(BF16)


def _q_project(x, mod, pre_g, w_q, qg2, d3):
    b, s, d = x.shape
    tm = TOKEN_TILE
    return _pallas(
        _q_body,
        grid=(b, s // tm),
        in_specs=[
            pl.BlockSpec((None, tm, d), lambda i, j: (i, j, 0)),
            pl.BlockSpec((None, 6, d), lambda i, j: (i, 0, 0)),
            _const_spec((1, d)), _const_spec((d, d)), _const_spec((1, LANES)),
            pl.BlockSpec((None, tm, LANES), lambda i, j: (i, j, 0)),
            _const_spec((LANES, N_HEADS * LANES)),
        ],
        out_specs=pl.BlockSpec((None, N_HEADS, tm, LANES), lambda i, j: (i, 0, j, 0)),
        out_shape=jax.ShapeDtypeStruct((b, N_HEADS, s, LANES), BF16),
        scratch_shapes=[pltpu.VMEM((tm, d), BF16)],
        name="q_project",
    )(x, mod, pre_g, w_q, qg2, d3, _decay_selectors(query_side=True))


def _attn_body(q_ref, k_ref, vt_ref, o_ref, acc_scr, *s_scr):
    n_h = q_ref.shape[0]
    tk = vt_ref.shape[3]
    tq = 2 * tk
    nq = q_ref.shape[1] // tq
    heads = range(n_h)
    halves = (slice(0, tk), slice(tk, tq))
    kpos = lax.broadcasted_iota(jnp.int32, (tk, tk), 0)
    qpos = lax.broadcasted_iota(jnp.int32, (tk, tk), 1)
    tri = kpos <= qpos
    LO, HI = 0, 1

    def scores_head(h, qt, j, slot, g, valid):
        q = q_ref[h, pl.ds(pl.multiple_of(qt * tq + g * tk, tk), tk), :]
        k = k_ref[h, pl.ds(pl.multiple_of(j * tk, tk), tk), :]
        s = lax.dot_general(k, q, (((1,), (1,)), ((), ())), preferred_element_type=F32)
        if valid is not None:
            s = jnp.where(valid, s, -jnp.inf)
        s_scr[2 * h + slot][:, halves[g]] = s
        return jnp.max(s, axis=0, keepdims=True)

    def accumulate_head(h, j, slot, g, cmax, m):
        m_new = jnp.maximum(m, cmax)
        alpha = jnp.exp2(m - m_new)
        p = jnp.exp2(s_scr[2 * h + slot][:, halves[g]] - m_new)
        pv = jnp.dot(vt_ref[h, j], p.astype(BF16), preferred_element_type=F32)
        acc_scr[h, :, halves[g]] = alpha * acc_scr[h, :, halves[g]] + pv
        return m_new

    def scores(qt, j, slot, g, valid):
        return tuple(scores_head(h, qt, j, slot, g, valid) for h in heads)

    def pair(qi, p, carry, diag):
        (ml_lo, ml_hi), (c0_lo, c0_hi) = carry
        out = []
        for h in heads:
            m_lo, m_hi = ml_lo[h], ml_hi[h]
            c1_lo = scores_head(h, qi, 2 * p + 1, 1, LO, None)
            m_lo = accumulate_head(h, 2 * p, 0, LO, c0_lo[h], m_lo)
            c1_hi = scores_head(h, qi, 2 * p + 1, 1, HI, None)
            m_hi = accumulate_head(h, 2 * p, 0, HI, c0_hi[h], m_hi)
            n_lo = scores_head(h, qi, 2 * p + 2, 0, LO, tri if diag else None)
            m_lo = accumulate_head(h, 2 * p + 1, 1, LO, c1_lo, m_lo)
            n_hi = scores_head(h, qi, 2 * p + 2, 0, HI, None)
            m_hi = accumulate_head(h, 2 * p + 1, 1, HI, c1_hi, m_hi)
            out.append((m_lo, m_hi, n_lo, n_hi))
        m_lo, m_hi, n_lo, n_hi = (tuple(o[i] for o in out) for i in range(4))
        return (m_lo, m_hi), (n_lo, n_hi)

    def query_tile(qi, cm0):
        acc_scr[...] = jnp.zeros_like(acc_scr)
        fresh = tuple(jnp.full((1, tk), -jnp.inf, F32) for _ in heads)
        n_loop = jnp.maximum(qi - 1, 0)

        def two_pairs(t, c):
            return pair(qi, 2 * t + 1, pair(qi, 2 * t, c, False), False)

        carry = lax.fori_loop(0, n_loop // 2, two_pairs, ((fresh, fresh), cm0))

        def finish(carry, n_front):
            if n_front == 2:
                carry = pair(qi, qi - 2, carry, False)
            if n_front >= 1:
                carry = pair(qi, qi - 1, carry, True)
            (ml_lo, ml_hi), (c0_lo, c0_hi) = carry
            nxt = jnp.minimum(qi + 1, nq - 1)
            n_lo, n_hi = [], []
            for h in heads:
                accumulate_head(h, 2 * qi, 0, LO, c0_lo[h], ml_lo[h])
                c1_hi = scores_head(h, qi, 2 * qi + 1, 1, HI, tri)
                m_hi = accumulate_head(h, 2 * qi, 0, HI, c0_hi[h], ml_hi[h])
                n_lo.append(scores_head(h, nxt, 0, 0, LO, None))
                accumulate_head(h, 2 * qi + 1, 1, HI, c1_hi, m_hi)
                n_hi.append(scores_head(h, nxt, 0, 0, HI, None))
            n_lo, n_hi = tuple(n_lo), tuple(n_hi)
            for g in (LO, HI):
                o_t = jnp.concatenate(
                    [acc_scr[h, 0:HEAD_DIM, halves[g]] / acc_scr[h, HEAD_DIM:HEAD_DIM + 1, halves[g]]
                     for h in heads], axis=0)
                o_ref[pl.ds(pl.multiple_of((2 * qi + g) * tk, tk), tk), :] = o_t.T
            return n_lo, n_hi

        n_front = jnp.minimum(qi, 1) + (n_loop & 1)
        return lax.switch(n_front, [functools.partial(finish, n_front=n) for n in range(3)], carry)

    lax.fori_loop(0, nq, query_tile, (scores(0, 0, 0, LO, tri), scores(0, 0, 0, HI, None)))


def _fox_attention(q_aug, k_aug, vt_aug):
    b, h, s, _ = k_aug.shape
    tk = TOKEN_TILE
    nk = s // tk
    hp = ATTN_HEADS_PER_STEP
    return _pallas(
        _attn_body,
        grid=(b, h // hp),
        in_specs=[
            pl.BlockSpec((None, hp, s, LANES), lambda i, j: (i, j, 0, 0)),
            pl.BlockSpec((None, hp, s, LANES), lambda i, j: (i, j, 0, 0)),
            pl.BlockSpec((None, hp, nk, V_ROWS, tk), lambda i, j: (i, j, 0, 0, 0)),
        ],
        out_specs=pl.BlockSpec((None, s, hp * HEAD_DIM), lambda i, j: (i, 0, j)),
        out_shape=jax.ShapeDtypeStruct((b, s, h * HEAD_DIM), F32),
        scratch_shapes=[pltpu.VMEM((hp, V_ROWS, ATTN_Q_TILE), F32)]
        + [pltpu.VMEM((tk, ATTN_Q_TILE), F32)] * (2 * hp),
        name="fox_attention",
    )(q_aug, k_aug, vt_aug)


def _out_body(x_ref, mod_ref, pre_ref, post_ref, a_ref, wg_ref, wo_ref, o_ref):
    x = x_ref[...]
    h = _prenorm(x, pre_ref[...], mod_ref[0:1, :], mod_ref[1:2, :]).astype(BF16)
    gate = _sigmoid(jnp.dot(h, wg_ref[...], preferred_element_type=F32))
    y = jnp.dot((a_ref[...] * gate).astype(BF16), wo_ref[...], preferred_element_type=F32)
    o_ref[...] = _gated_residual(x, y, post_ref[...], mod_ref[2:3, :])


def _fox_output(x, mod, pre_g, post_g, o_t, w_g, w_o):
    b, s, d = x.shape
    tm = WIDE_TOKEN_TILE
    tok = pl.BlockSpec((None, tm, d), lambda i, j: (i, j, 0))
    return _pallas(
        _out_body,
        grid=(b, s // tm),
        in_specs=[
            tok,
            pl.BlockSpec((None, 6, d), lambda i, j: (i, 0, 0)),
            _const_spec((1, d)), _const_spec((1, d)),
            tok,
            _const_spec((d, d)), _const_spec((d, d)),
        ],
        out_specs=tok,
        out_shape=jax.ShapeDtypeStruct((b, s, d), F32),
        name="fox_output",
    )(x, mod, pre_g, post_g, o_t, w_g, w_o)


def kernel(x, c, ada_w, ada_b, pre_mix_g, post_mix_g, pre_ffn_g, post_ffn_g, ffn_w_gu, ffn_w_down,
           a_w_in, a_b_in, a_ln_g, a_ln_b, a_w_s, a_b_s, a_w_out, kv_ada_w, kv_ada_b, kv_norm_g,
           kv_w, kv_b_f, k_norm_g, b_w_qg, b_q_norm_g, b_w_o):
    b, s, d = x.shape
    depth = ada_w.shape[0]
    n_a = a_w_in.shape[0]
    assert d == N_HEADS * HEAD_DIM and s % ATTN_Q_TILE == 0

    mods = _modulation(c, ada_w, ada_b).reshape(depth, b, 6, d)
    kvmod = _modulation(c, kv_ada_w[None], kv_ada_b[None]).reshape(b, 2, d)

    row = lambda v: v.reshape(1, -1)
    q_scale = LOG2E * HEAD_DIM ** -0.5

    ffn_gu, ffn_down = ffn_w_gu.astype(BF16), ffn_w_down.astype(BF16)
    gmlp_in, gmlp_out = _lane_padded(a_w_in), _lane_padded(a_w_out)
    fox_qg, fox_o, kvf = b_w_qg.astype(BF16), b_w_o.astype(BF16), kv_w.astype(BF16)

    for layer in range(depth):
        mod = mods[layer]
        if layer < n_a:
            i = layer
            bs_full = jnp.repeat(a_b_s[i].T, CHUNK, axis=1)
            x = _gmlp_layer(x, mod, row(pre_mix_g[layer]), row(post_mix_g[layer]), gmlp_in,
                            row(a_b_in[i]), row(a_ln_g[i]), row(a_ln_b[i]), a_w_s[i], bs_full, gmlp_out, i)
        else:
            j = layer - n_a
            qg2 = row(jnp.tile(b_q_norm_g[j] * q_scale, 2))
            q_aug = _q_project(x, mod, row(pre_mix_g[layer]), fox_qg[j][:, :d], qg2, d3)
            o_t = _fox_attention(q_aug, k_aug, vt_aug)
            x = _fox_output(x, mod, row(pre_mix_g[layer]), row(post_mix_g[layer]), o_t,
                            fox_qg[j][:, d:], fox_o[j])
        x = _ffn_layer(x, mod, row(pre_ffn_g[layer]), row(post_ffn_g[layer]), ffn_gu, ffn_down, layer)
        if layer == n_a - 1:
            w_f = kv_w[:, 2 * d:]
            pad = jnp.zeros((d, LANES - 3 * N_HEADS), F32)
            w_f3 = jnp.concatenate([w_f, w_f, w_f, pad], axis=1).astype(BF16)
            b_f3 = row(jnp.concatenate([kv_b_f, kv_b_f, kv_b_f, jnp.zeros((LANES - 3 * N_HEADS,), F32)]))
            k_aug, vt_aug, d3 = _kv_project(
                x, kvmod, row(kv_norm_g), kvf[:, :d], kvf[:, d:2 * d], w_f3, b_f3,
                row(jnp.tile(k_norm_g, 2)))
    return x
```

```python
import functools
import math

import jax
import jax.numpy as jnp
from jax import lax
from jax.experimental import pallas as pl
from jax.experimental.pallas import tpu as pltpu

F32 = jnp.float32
BF16 = jnp.bfloat16

EPS = 1e-6
N_HEADS = 16
HEAD_DIM = 64
CHUNK = 128
GROUPS = 16
LOG2E = 1.4426950408889634
LANES = 128
MXU_N = 256
TOKEN_TILE = 512
WIDE_TOKEN_TILE = 2 * TOKEN_TILE
ATTN_Q_TILE = 2 * TOKEN_TILE
ATTN_HEADS_PER_STEP = 2
F32_SUBLANES = 8
BF16_SUBLANES = 16
MOD_COLS = 1024
V_ROWS = HEAD_DIM + BF16_SUBLANES
V7X_COMPILER_VMEM_RESERVE = 6 * 1024 * 1024


def _sigmoid(x):
    return 1.0 / (1.0 + jnp.exp(-x))


def _unit_rms(x):
    return x * lax.rsqrt(jnp.mean(x * x, axis=-1, keepdims=True) + EPS)


def _prenorm(x, g, shift, scale):
    return _unit_rms(x) * (g * (1.0 + scale)) + shift


def _gated_residual(x, y, g, gate):
    return x + _unit_rms(y) * (g * gate)


def _split3(x):
    hi = x.astype(BF16).astype(F32)
    r = x - hi
    mid = r.astype(BF16).astype(F32)
    lo = (r - mid).astype(BF16).astype(F32)
    return hi, mid, lo


def _const_spec(shape):
    n = len(shape)
    return pl.BlockSpec(shape, lambda *_: (0,) * n, pipeline_mode=pl.Buffered(1))


def _slab_spec(block, index):
    return pl.BlockSpec(block, lambda *_: index, pipeline_mode=pl.Buffered(1))


def _lane_padded(w):
    return jnp.pad(w, ((0, 0),) * (w.ndim - 1) + ((0, LANES),)).astype(BF16)


def _block_bytes(spec, dtype):
    buffers = 2 if spec.pipeline_mode is None else spec.pipeline_mode.buffer_count
    return math.prod(1 if n is None else n for n in spec.block_shape) * jnp.dtype(dtype).itemsize * buffers


def _pallas(body, *, grid, in_specs, out_specs, out_shape, scratch_shapes=(), name):
    multi = isinstance(out_shape, (list, tuple))
    outs = zip(out_specs, out_shape) if multi else [(out_specs, out_shape)]
    need = sum(_block_bytes(s, o.dtype) for s, o in outs)
    need += sum(math.prod(m.shape) * jnp.dtype(m.dtype).itemsize for m in scratch_shapes)

    def call(*inputs):
        windows = sum(_block_bytes(s, a.dtype) for s, a in zip(in_specs, inputs, strict=True))
        return pl.pallas_call(
            body, grid=grid, in_specs=in_specs, out_specs=out_specs, out_shape=out_shape,
            scratch_shapes=list(scratch_shapes),
            compiler_params=pltpu.CompilerParams(
                dimension_semantics=("arbitrary",) * len(grid),
                vmem_limit_bytes=need + windows + V7X_COMPILER_VMEM_RESERVE),
            name=name)(*inputs)

    return call


def _mod_body(c_ref, w_ref, b_ref, o_ref):
    c = c_ref[...]
    ca = (c * _sigmoid(c)).astype(BF16)
    o_ref[...] = jnp.dot(ca, w_ref[...].astype(BF16), preferred_element_type=F32) + b_ref[...]


def _modulation(c, w, b):
    n_l, d, n = w.shape
    batch = c.shape[0]
    rows = -(-batch // F32_SUBLANES) * F32_SUBLANES
    nb = MOD_COLS
    out = _pallas(
        _mod_body,
        grid=(n_l, n // nb),
        in_specs=[
            pl.BlockSpec((rows, d), lambda l, j: (0, 0)),
            pl.BlockSpec((None, d, nb), lambda l, j: (l, 0, j)),
            pl.BlockSpec((None, 1, nb), lambda l, j: (l, 0, j)),
        ],
        out_specs=pl.BlockSpec((None, rows, nb), lambda l, j: (l, 0, j)),
        out_shape=jax.ShapeDtypeStruct((n_l, rows, n), F32),
        name="modulation",
    )(jnp.pad(c, ((0, rows - batch), (0, 0))), w, b.reshape(n_l, 1, n))
    return out[:, :batch]


def _gmlp_body(x_ref, mod_ref, pre_ref, post_ref, win_ref, bin_ref, lng_ref, lnb_ref, ws_ref,
               bs_ref, wout_ref, o_ref, h_scr, u_scr, v_scr, vn_scr, y_scr):
    tm, d = x_ref.shape
    gw = u_scr.shape[1]
    x = x_ref[...]
    h_scr[...] = _prenorm(x, pre_ref[...], mod_ref[0:1, :], mod_ref[1:2, :]).astype(BF16)

    k_gelu = 0.7978845608028654
    nc = 2 * MXU_N
    for c0 in range(0, 2 * gw, nc):
        z = jnp.dot(h_scr[...], win_ref[:, c0:c0 + nc], preferred_element_type=F32)
        z = z + bin_ref[:, c0:c0 + nc]
        z = z * (0.5 * (1.0 + jnp.tanh(k_gelu * (z + 0.044715 * (z * z * z)))))
        if c0 < gw:
            u_scr[:, c0:c0 + nc] = z
        else:
            v_scr[:, c0 - gw:c0 - gw + nc] = z

    v = v_scr[...]
    mu = jnp.mean(v, axis=-1, keepdims=True)
    vc = v - mu
    rstd = lax.rsqrt(jnp.mean(vc * vc, axis=-1, keepdims=True) + EPS)
    vn_scr[...] = (vc * rstd * lng_ref[...] + lnb_ref[...]).astype(BF16)

    n_chunks = tm // CHUNK
    row = lax.broadcasted_iota(jnp.int32, (CHUNK, CHUNK), 0)
    col = lax.broadcasted_iota(jnp.int32, (CHUNK, CHUNK), 1)
    causal = col <= row
    for g in range(GROUPS):
        gs = slice(g * CHUNK, (g + 1) * CHUNK)
        ws = jnp.where(causal, ws_ref[g], 0.0).astype(BF16)
        rhs = jnp.concatenate(
            [vn_scr[c * CHUNK:(c + 1) * CHUNK, gs] for c in range(n_chunks)], axis=1)
        sp = jnp.dot(ws, rhs, preferred_element_type=F32)
        for c in range(n_chunks):
            cs = slice(c * CHUNK, (c + 1) * CHUNK)
            y_scr[cs, gs] = (u_scr[cs, gs] * (sp[:, cs] + bs_ref[:, gs])).astype(BF16)

    y = jnp.dot(y_scr[...], wout_ref[:, 0:d], preferred_element_type=F32)
    o_ref[...] = _gated_residual(x, y, post_ref[...], mod_ref[2:3, :])


def _gmlp_layer(x, mod, pre_g, post_g, w_in, b_in, ln_g, ln_b, w_s, bs_full, w_out, layer):
    b, s, d = x.shape
    gw = w_out.shape[1]
    tm = TOKEN_TILE
    tok = pl.BlockSpec((None, tm, d), lambda i, j: (i, j, 0))
    return _pallas(
        _gmlp_body,
        grid=(b, s // tm),
        in_specs=[
            tok,
            pl.BlockSpec((None, 6, d), lambda i, j: (i, 0, 0)),
            _const_spec((1, d)), _const_spec((1, d)),
            _slab_spec((None,) + w_in.shape[1:], (layer, 0, 0)), _const_spec((1, 2 * gw)),
            _const_spec((1, gw)), _const_spec((1, gw)),
            _const_spec((GROUPS, CHUNK, CHUNK)), _const_spec((CHUNK, gw)),
            _slab_spec((None,) + w_out.shape[1:], (layer, 0, 0)),
        ],
        out_specs=tok,
        out_shape=jax.ShapeDtypeStruct((b, s, d), F32),
        scratch_shapes=[
            pltpu.VMEM((tm, d), BF16), pltpu.VMEM((tm, gw), F32), pltpu.VMEM((tm, gw), F32),
            pltpu.VMEM((tm, gw), BF16), pltpu.VMEM((tm, gw), BF16),
        ],
        name="gmlp_mixer",
    )(x, mod, pre_g, post_g, w_in, b_in, ln_g, ln_b, w_s, bs_full, w_out)


def _ffn_body(x_ref, mod_ref, pre_ref, post_ref, wgu_ref, wd_ref, o_ref, h_scr, a_scr):
    f = a_scr.shape[1]
    x = x_ref[...]
    h_scr[...] = _prenorm(x, pre_ref[...], mod_ref[3:4, :], mod_ref[4:5, :]).astype(BF16)
    for c0 in range(0, f, MXU_N):
        g = jnp.dot(h_scr[...], wgu_ref[:, c0:c0 + MXU_N], preferred_element_type=F32)
        u = jnp.dot(h_scr[...], wgu_ref[:, f + c0:f + c0 + MXU_N], preferred_element_type=F32)
        a_scr[:, c0:c0 + MXU_N] = (g * _sigmoid(g) * u).astype(BF16)
    y = jnp.dot(a_scr[...], wd_ref[...], preferred_element_type=F32)
    o_ref[...] = _gated_residual(x, y, post_ref[...], mod_ref[5:6, :])


def _ffn_layer(x, mod, pre_g, post_g, w_gu, w_down, layer):
    b, s, d = x.shape
    f = w_down.shape[1]
    tm = WIDE_TOKEN_TILE
    tok = pl.BlockSpec((None, tm, d), lambda i, j: (i, j, 0))
    return _pallas(
        _ffn_body,
        grid=(b, s // tm),
        in_specs=[
            tok,
            pl.BlockSpec((None, 6, d), lambda i, j: (i, 0, 0)),
            _const_spec((1, d)), _const_spec((1, d)),
            _slab_spec((None, d, 2 * f), (layer, 0, 0)), _slab_spec((None, f, d), (layer, 0, 0)),
        ],
        out_specs=tok,
        out_shape=jax.ShapeDtypeStruct((b, s, d), F32),
        scratch_shapes=[pltpu.VMEM((tm, d), BF16), pltpu.VMEM((tm, f), BF16)],
        name="swiglu_ffn",
    )(x, mod, pre_g, post_g, w_gu, w_down)


def _head_pair_norm(t, gain2):
    lane = lax.broadcasted_iota(jnp.int32, t.shape, 1)
    sq = t * t
    first = lane < HEAD_DIM
    s0 = jnp.sum(jnp.where(first, sq, 0.0), axis=-1, keepdims=True)
    s1 = jnp.sum(jnp.where(first, 0.0, sq), axis=-1, keepdims=True)
    rs = jnp.where(first, lax.rsqrt(s0 / HEAD_DIM + EPS), lax.rsqrt(s1 / HEAD_DIM + EPS))
    return t * rs * gain2


def _decay_selectors(query_side):
    row = lax.broadcasted_iota(jnp.int32, (LANES, N_HEADS * LANES), 0)
    col = lax.broadcasted_iota(jnp.int32, (LANES, N_HEADS * LANES), 1)
    head, off = col // LANES, col % LANES - HEAD_DIM
    first, second = (off >= 0) & (off < 3), (off >= 3) & (off < 6)
    piece_cols, piece_idx = (first, off) if query_side else (second, off - 3)
    one_cols = second if query_side else first
    is_piece = piece_cols & (row == piece_idx * N_HEADS + head)
    is_one = one_cols & (row == 3 * N_HEADS)
    sel = jnp.where(is_piece, 1.0 if query_side else -1.0, jnp.where(is_one, 1.0, 0.0))
    return sel.astype(BF16)


def _with_tail(t, tail):
    lane = lax.broadcasted_iota(jnp.int32, t.shape, 1)
    return jnp.where(lane < HEAD_DIM, t, tail)


def _kv_body(x_ref, mod_ref, g_ref, wk_ref, wv_ref, wf_ref, bf_ref, kg_ref, sel_ref,
             k_out, vt_out, d_out, h_scr, carry_scr):
    tm, d = x_ref.shape

    @pl.when(pl.program_id(1) == 0)
    def _():
        carry_scr[...] = jnp.zeros_like(carry_scr)

    h_scr[...] = _prenorm(x_ref[...], g_ref[...], mod_ref[0:1, :], mod_ref[1:2, :]).astype(BF16)

    lane = lax.broadcasted_iota(jnp.int32, (tm, LANES), 1)
    fl = jnp.dot(h_scr[...], wf_ref[...], preferred_element_type=F32) + bf_ref[...]
    z = -fl
    ls = -(jnp.maximum(z, 0.0) + jnp.log1p(jnp.exp(-jnp.abs(z))))
    hi, mid, lo = _split3(ls)
    pieces = jnp.where(lane < N_HEADS, hi, jnp.where(lane < 2 * N_HEADS, mid, jnp.where(
        lane < 3 * N_HEADS, lo, 0.0))).astype(BF16)
    r = lax.broadcasted_iota(jnp.int32, (tm, tm), 0)
    c = lax.broadcasted_iota(jnp.int32, (tm, tm), 1)
    tri = jnp.where(c <= r, 1.0, 0.0).astype(BF16)
    cs = jnp.dot(tri, pieces, preferred_element_type=F32)
    tot = cs + pltpu.roll(cs, LANES - N_HEADS, axis=1) + pltpu.roll(cs, LANES - 2 * N_HEADS, axis=1)
    dc = jnp.where(lane < N_HEADS, tot + carry_scr[...], 0.0)
    carry_scr[...] = dc[tm - 1:tm, :]
    hi, mid, lo = _split3(dc * LOG2E)
    d3 = (hi + pltpu.roll(mid, N_HEADS, axis=1) + pltpu.roll(lo, 2 * N_HEADS, axis=1)
          + jnp.where(lane == 3 * N_HEADS, 1.0, 0.0)).astype(BF16)
    d_out[...] = d3

    kg2 = kg_ref[...]
    ones_rows = jnp.where(
        lax.broadcasted_iota(jnp.int32, (V_ROWS - HEAD_DIM, tm), 0) == 0, 1.0, 0.0)
    heads_per_chunk = MXU_N // HEAD_DIM
    for c4 in range(d // MXU_N):
        kc = jnp.dot(h_scr[...], wk_ref[:, c4 * MXU_N:(c4 + 1) * MXU_N], preferred_element_type=F32)
        vc = jnp.dot(h_scr[...], wv_ref[:, c4 * MXU_N:(c4 + 1) * MXU_N], preferred_element_type=F32)
        tails = jnp.dot(d3, sel_ref[:, c4 * heads_per_chunk * LANES:(c4 + 1) * heads_per_chunk * LANES],
                        preferred_element_type=F32)
        for half in range(2):
            kn = _head_pair_norm(kc[:, half * LANES:(half + 1) * LANES], kg2)
            vt = vc[:, half * LANES:(half + 1) * LANES].T
            for sub, src in ((0, kn), (1, pltpu.roll(kn, HEAD_DIM, axis=1))):
                loc = 2 * half + sub
                hd = c4 * heads_per_chunk + loc
                k_out[hd] = _with_tail(src, tails[:, loc * LANES:(loc + 1) * LANES]).astype(BF16)
                vt_out[hd] = jnp.concatenate(
                    [vt[sub * HEAD_DIM:(sub + 1) * HEAD_DIM, :], ones_rows], axis=0).astype(BF16)


def _q_body(x_ref, mod_ref, pre_ref, wq_ref, qg_ref, d_ref, sel_ref, q_out, h_scr):
    tm, d = x_ref.shape
    h_scr[...] = _prenorm(x_ref[...], pre_ref[...], mod_ref[0:1, :], mod_ref[1:2, :]).astype(BF16)
    d3 = d_ref[...]
    qg2 = qg_ref[...]
    heads_per_chunk = MXU_N // HEAD_DIM
    for c4 in range(d // MXU_N):
        qc = jnp.dot(h_scr[...], wq_ref[:, c4 * MXU_N:(c4 + 1) * MXU_N], preferred_element_type=F32)
        tails = jnp.dot(d3, sel_ref[:, c4 * heads_per_chunk * LANES:(c4 + 1) * heads_per_chunk * LANES],
                        preferred_element_type=F32)
        for half in range(2):
            qn = _head_pair_norm(qc[:, half * LANES:(half + 1) * LANES], qg2)
            for sub, src in ((0, qn), (1, pltpu.roll(qn, HEAD_DIM, axis=1))):
                loc = 2 * half + sub
                qa = _with_tail(src, tails[:, loc * LANES:(loc + 1) * LANES])
                q_out[c4 * heads_per_chunk + loc] = qa.astype(BF16)


def _kvq_body(x_ref, kvmod_ref, g_ref, wk_ref, wv_ref, wf_ref, bf_ref, kg_ref, selk_ref,
              mod_ref, pre_ref, wq_ref, qg_ref, selq_ref,
              k_out, vt_out, d_out, q_out, h_scr, carry_scr, hq_scr):
    _kv_body(x_ref, kvmod_ref, g_ref, wk_ref, wv_ref, wf_ref, bf_ref, kg_ref, selk_ref,
             k_out, vt_out, d_out, h_scr, carry_scr)
    _q_body(x_ref, mod_ref, pre_ref, wq_ref, qg_ref, d_out, selq_ref, q_out, hq_scr)


def _kvq_project(x, kvmod, norm_g, w_k, w_v, w_f3, b_f3, kg2, mod, pre_g, w_q, qg2):
    b, s, d = x.shape
    tm = TOKEN_TILE
    head_tiles = pl.BlockSpec((None, N_HEADS, tm, LANES), lambda i, j: (i, 0, j, 0))
    return _pallas(
        _kvq_body,
        grid=(b, s // tm),
        in_specs=[
            pl.BlockSpec((None, tm, d), lambda i, j: (i, j, 0)),
            pl.BlockSpec((None, 2, d), lambda i, j: (i, 0, 0)),
            _const_spec((1, d)),
            _const_spec((d, d)), _const_spec((d, d)), _const_spec((d, LANES)),
            _const_spec((1, LANES)), _const_spec((1, LANES)), _const_spec((LANES, N_HEADS * LANES)),
            pl.BlockSpec((None, 6, d), lambda i, j: (i, 0, 0)),
            _const_spec((1, d)), _const_spec((d, d)), _const_spec((1, LANES)),
            _const_spec((LANES, N_HEADS * LANES)),
        ],
        out_specs=[
            head_tiles,
            pl.BlockSpec((None, N_HEADS, None, V_ROWS, tm), lambda i, j: (i, 0, j, 0, 0)),
            pl.BlockSpec((None, tm, LANES), lambda i, j: (i, j, 0)),
            head_tiles,
        ],
        out_shape=[
            jax.ShapeDtypeStruct((b, N_HEADS, s, LANES), BF16),
            jax.ShapeDtypeStruct((b, N_HEADS, s // tm, V_ROWS, tm), BF16),
            jax.ShapeDtypeStruct((b, s, LANES), BF16),
            jax.ShapeDtypeStruct((b, N_HEADS, s, LANES), BF16),
        ],
        scratch_shapes=[pltpu.VMEM((tm, d), BF16), pltpu.VMEM((1, LANES), F32), pltpu.VMEM((tm, d), BF16)],
        name="kvq_project",
    )(x, kvmod, norm_g, w_k, w_v, w_f3, b_f3, kg2, _decay_selectors(query_side=False),
      mod, pre_g, w_q, qg2, _decay_selectors(query_side=True))


def _q_project(x, mod, pre_g, w_q, qg2, d3):
    b, s, d = x.shape
    tm = TOKEN_TILE
    return _pallas(
        _q_body,
        grid=(b, s // tm),
        in_specs=[
            pl.BlockSpec((None, tm, d), lambda i, j: (i, j, 0)),
            pl.BlockSpec((None, 6, d), lambda i, j: (i, 0, 0)),
            _const_spec((1, d)), _const_spec((d, d)), _const_spec((1, LANES)),
            pl.BlockSpec((None, tm, LANES), lambda i, j: (i, j, 0)),
            _const_spec((LANES, N_HEADS * LANES)),
        ],
        out_specs=pl.BlockSpec((None, N_HEADS, tm, LANES), lambda i, j: (i, 0, j, 0)),
        out_shape=jax.ShapeDtypeStruct((b, N_HEADS, s, LANES), BF16),
        scratch_shapes=[pltpu.VMEM((tm, d), BF16)],
        name="q_project",
    )(x, mod, pre_g, w_q, qg2, d3, _decay_selectors(query_side=True))


def _attn_body(q_ref, k_ref, vt_ref, o_ref, acc_scr, *s_scr):
    n_h = q_ref.shape[0]
    tk = vt_ref.shape[3]
    tq = 2 * tk
    nq = q_ref.shape[1] // tq
    heads = range(n_h)
    halves = (slice(0, tk), slice(tk, tq))
    kpos = lax.broadcasted_iota(jnp.int32, (tk, tk), 0)
    qpos = lax.broadcasted_iota(jnp.int32, (tk, tk), 1)
    tri = kpos <= qpos
    LO, HI = 0, 1

    def scores_head(h, qt, j, slot, g, valid):
        q = q_ref[h, pl.ds(pl.multiple_of(qt * tq + g * tk, tk), tk), :]
        k = k_ref[h, pl.ds(pl.multiple_of(j * tk, tk), tk), :]
        s = lax.dot_general(k, q, (((1,), (1,)), ((), ())), preferred_element_type=F32)
        if valid is not None:
            s = jnp.where(valid, s, -jnp.inf)
        s_scr[2 * h + slot][:, halves[g]] = s
        return jnp.max(s, axis=0, keepdims=True)

    def accumulate_head(h, j, slot, g, cmax, m):
        m_new = jnp.maximum(m, cmax)
        alpha = jnp.exp2(m - m_new)
        p = jnp.exp2(s_scr[2 * h + slot][:, halves[g]] - m_new)
        pv = jnp.dot(vt_ref[h, j], p.astype(BF16), preferred_element_type=F32)
        acc_scr[h, :, halves[g]] = alpha * acc_scr[h, :, halves[g]] + pv
        return m_new

    def scores(qt, j, slot, g, valid):
        return tuple(scores_head(h, qt, j, slot, g, valid) for h in heads)

    def accumulate(j, slot, g, cmax, ml):
        return tuple(accumulate_head(h, j, slot, g, cmax[h], ml[h]) for h in heads)

    def pair(qi, p, carry, diag):
        (ml_lo, ml_hi), (c0_lo, c0_hi) = carry
        out = []
        for h in heads:
            m_lo, m_hi = ml_lo[h], ml_hi[h]
            c1_lo = scores_head(h, qi, 2 * p + 1, 1, LO, None)
            m_lo = accumulate_head(h, 2 * p, 0, LO, c0_lo[h], m_lo)
            c1_hi = scores_head(h, qi, 2 * p + 1, 1, HI, None)
            m_hi = accumulate_head(h, 2 * p, 0, HI, c0_hi[h], m_hi)
            n_lo = scores_head(h, qi, 2 * p + 2, 0, LO, tri if diag else None)
            m_lo = accumulate_head(h, 2 * p + 1, 1, LO, c1_lo, m_lo)
            n_hi = scores_head(h, qi, 2 * p + 2, 0, HI, None)
            m_hi = accumulate_head(h, 2 * p + 1, 1, HI, c1_hi, m_hi)
            out.append((m_lo, m_hi, n_lo, n_hi))
        m_lo, m_hi, n_lo, n_hi = (tuple(o[i] for o in out) for i in range(4))
        return (m_lo, m_hi), (n_lo, n_hi)

    def query_tile(qi, cm0):
        acc_scr[...] = jnp.zeros_like(acc_scr)
        fresh = tuple(jnp.full((1, tk), -jnp.inf, F32) for _ in heads)
        n_loop = jnp.maximum(qi - 1, 0)

        def two_pairs(t, c):
            return pair(qi, 2 * t + 1, pair(qi, 2 * t, c, False), False)

        carry = lax.fori_loop(0, n_loop // 2, two_pairs, ((fresh, fresh), cm0))

        def finish(carry, n_front):
            if n_front == 2:
                carry = pair(qi, qi - 2, carry, False)
            if n_front >= 1:
                carry = pair(qi, qi - 1, carry, True)
            (ml_lo, ml_hi), (c0_lo, c0_hi) = carry
            ml_lo = accumulate(2 * qi, 0, LO, c0_lo, ml_lo)
            c1_hi = scores(qi, 2 * qi + 1, 1, HI, tri)
            ml_hi = accumulate(2 * qi, 0, HI, c0_hi, ml_hi)
            nxt = jnp.minimum(qi + 1, nq - 1)
            n_lo = scores(nxt, 0, 0, LO, None)
            accumulate(2 * qi + 1, 1, HI, c1_hi, ml_hi)
            n_hi = scores(nxt, 0, 0, HI, None)
            for g in (LO, HI):
                o_t = jnp.concatenate(
                    [acc_scr[h, 0:HEAD_DIM, halves[g]] / acc_scr[h, HEAD_DIM:HEAD_DIM + 1, halves[g]]
                     for h in heads], axis=0)
                o_ref[pl.ds(pl.multiple_of((2 * qi + g) * tk, tk), tk), :] = o_t.T
            return n_lo, n_hi

        n_front = jnp.minimum(qi, 1) + (n_loop & 1)
        return lax.switch(n_front, [functools.partial(finish, n_front=n) for n in range(3)], carry)

    lax.fori_loop(0, nq, query_tile, (scores(0, 0, 0, LO, tri), scores(0, 0, 0, HI, None)))


def _fox_attention(q_aug, k_aug, vt_aug):
    b, h, s, _ = k_aug.shape
    tk = TOKEN_TILE
    nk = s // tk
    hp = ATTN_HEADS_PER_STEP
    return _pallas(
        _attn_body,
        grid=(b, h // hp),
        in_specs=[
            pl.BlockSpec((None, hp, s, LANES), lambda i, j: (i, j, 0, 0)),
            pl.BlockSpec((None, hp, s, LANES), lambda i, j: (i, j, 0, 0)),
            pl.BlockSpec((None, hp, nk, V_ROWS, tk), lambda i, j: (i, j, 0, 0, 0)),
        ],
        out_specs=pl.BlockSpec((None, s, hp * HEAD_DIM), lambda i, j: (i, 0, j)),
        out_shape=jax.ShapeDtypeStruct((b, s, h * HEAD_DIM), F32),
        scratch_shapes=[pltpu.VMEM((hp, V_ROWS, ATTN_Q_TILE), F32)]
        + [pltpu.VMEM((tk, ATTN_Q_TILE), F32)] * (2 * hp),
        name="fox_attention",
    )(q_aug, k_aug, vt_aug)


def _out_body(x_ref, mod_ref, pre_ref, post_ref, a_ref, wg_ref, wo_ref, o_ref):
    x = x_ref[...]
    h = _prenorm(x, pre_ref[...], mod_ref[0:1, :], mod_ref[1:2, :]).astype(BF16)
    gate = _sigmoid(jnp.dot(h, wg_ref[...], preferred_element_type=F32))
    y = jnp.dot((a_ref[...] * gate).astype(BF16), wo_ref[...], preferred_element_type=F32)
    o_ref[...] = _gated_residual(x, y, post_ref[...], mod_ref[2:3, :])


def _fox_output(x, mod, pre_g, post_g, o_t, w_g, w_o):
    b, s, d = x.shape
    tm = WIDE_TOKEN_TILE
    tok = pl.BlockSpec((None, tm, d), lambda i, j: (i, j, 0))
    return _pallas(
        _out_body,
        grid=(b, s // tm),
        in_specs=[
            tok,
            pl.BlockSpec((None, 6, d), lambda i, j: (i, 0, 0)),
            _const_spec((1, d)), _const_spec((1, d)),
            tok,
            _const_spec((d, d)), _const_spec((d, d)),
        ],
        out_specs=tok,
        out_shape=jax.ShapeDtypeStruct((b, s, d), F32),
        name="fox_output",
    )(x, mod, pre_g, post_g, o_t, w_g, w_o)


def kernel(x, c, ada_w, ada_b, pre_mix_g, post_mix_g, pre_ffn_g, post_ffn_g, ffn_w_gu, ffn_w_down,
           a_w_in, a_b_in, a_ln_g, a_ln_b, a_w_s, a_b_s, a_w_out, kv_ada_w, kv_ada_b, kv_norm_g,
           kv_w, kv_b_f, k_norm_g, b_w_qg, b_q_norm_g, b_w_o):
    b, s, d = x.shape
    depth = ada_w.shape[0]
    n_a = a_w_in.shape[0]
    assert d == N_HEADS * HEAD_DIM and s % ATTN_Q_TILE == 0

    mods = _modulation(c, ada_w, ada_b).reshape(depth, b, 6, d)
    kvmod = _modulation(c, kv_ada_w[None], kv_ada_b[None]).reshape(b, 2, d)

    row = lambda v: v.reshape(1, -1)
    q_scale = LOG2E * HEAD_DIM ** -0.5
    q_gain = lambda j: row(jnp.tile(b_q_norm_g[j] * q_scale, 2))

    ffn_gu, ffn_down = ffn_w_gu.astype(BF16), ffn_w_down.astype(BF16)
    gmlp_in, gmlp_out = _lane_padded(a_w_in), _lane_padded(a_w_out)
    fox_qg, fox_o, kvf = b_w_qg.astype(BF16), b_w_o.astype(BF16), kv_w.astype(BF16)

    for layer in range(depth):
        mod = mods[layer]
        if layer < n_a:
            i = layer
            bs_full = jnp.repeat(a_b_s[i].T, CHUNK, axis=1)
            x = _gmlp_layer(x, mod, row(pre_mix_g[layer]), row(post_mix_g[layer]), gmlp_in,
                            row(a_b_in[i]), row(a_ln_g[i]), row(a_ln_b[i]), a_w_s[i], bs_full, gmlp_out, i)
        else:
            j = layer - n_a
            if j > 0:
                q_aug = _q_project(x, mod, row(pre_mix_g[layer]), fox_qg[j][:, :d], q_gain(j), d3)
            o_t = _fox_attention(q_aug, k_aug, vt_aug)
            x = _fox_output(x, mod, row(pre_mix_g[layer]), row(post_mix_g[layer]), o_t,
                            fox_qg[j][:, d:], fox_o[j])
        x = _ffn_layer(x, mod, row(pre_ffn_g[layer]), row(post_ffn_g[layer]), ffn_gu, ffn_down, layer)
        if layer == n_a - 1:
            w_f = kv_w[:, 2 * d:]
            pad = jnp.zeros((d, LANES - 3 * N_HEADS), F32)
            w_f3 = jnp.concatenate([w_f, w_f, w_f, pad], axis=1).astype(BF16)
            b_f3 = row(jnp.concatenate([kv_b_f, kv_b_f, kv_b_f, jnp.zeros((LANES - 3 * N_HEADS,), F32)]))
            k_aug, vt_aug, d3, q_aug = _kvq_project(
                x, kvmod, row(kv_norm_g), kvf[:, :d], kvf[:, d:2 * d], w_f3, b_f3,
                row(jnp.tile(k_norm_g, 2)),
                mods[n_a], row(pre_mix_g[n_a]), fox_qg[0][:, :d], q_gain(0))
    return x
```

```python
import functools
import math

import jax
import jax.numpy as jnp
from jax import lax
from jax.experimental import pallas as pl
from jax.experimental.pallas import tpu as pltpu

F32 = jnp.float32
BF16 = jnp.bfloat16

EPS = 1e-6
N_HEADS = 16
HEAD_DIM = 64
CHUNK = 128
GROUPS = 16
LOG2E = 1.4426950408889634
LANES = 128
MXU_N = 256
TOKEN_TILE = 512
WIDE_TOKEN_TILE = 2 * TOKEN_TILE
ATTN_Q_TILE = 2 * TOKEN_TILE
ATTN_HEADS_PER_STEP = 2
F32_SUBLANES = 8
BF16_SUBLANES = 16
MOD_COLS = 1024
V_ROWS = HEAD_DIM + BF16_SUBLANES
V7X_COMPILER_VMEM_RESERVE = 6 * 1024 * 1024


def _sigmoid(x):
    return 1.0 / (1.0 + jnp.exp(-x))


def _unit_rms(x):
    return x * lax.rsqrt(jnp.mean(x * x, axis=-1, keepdims=True) + EPS)


def _prenorm(x, g, shift, scale):
    return _unit_rms(x) * (g * (1.0 + scale)) + shift


def _gated_residual(x, y, g, gate):
    return x + _unit_rms(y) * (g * gate)


def _split3(x):
    hi = x.astype(BF16).astype(F32)
    r = x - hi
    mid = r.astype(BF16).astype(F32)
    lo = (r - mid).astype(BF16).astype(F32)
    return hi, mid, lo


def _const_spec(shape):
    n = len(shape)
    return pl.BlockSpec(shape, lambda *_: (0,) * n, pipeline_mode=pl.Buffered(1))


def _slab_spec(block, index):
    return pl.BlockSpec(block, lambda *_: index, pipeline_mode=pl.Buffered(1))


def _lane_padded(w):
    return jnp.pad(w, ((0, 0),) * (w.ndim - 1) + ((0, LANES),)).astype(BF16)


def _block_bytes(spec, dtype):
    buffers = 2 if spec.pipeline_mode is None else spec.pipeline_mode.buffer_count
    return math.prod(1 if n is None else n for n in spec.block_shape) * jnp.dtype(dtype).itemsize * buffers


def _pallas(body, *, grid, in_specs, out_specs, out_shape, scratch_shapes=(), name):
    multi = isinstance(out_shape, (list, tuple))
    outs = zip(out_specs, out_shape) if multi else [(out_specs, out_shape)]
    need = sum(_block_bytes(s, o.dtype) for s, o in outs)
    need += sum(math.prod(m.shape) * jnp.dtype(m.dtype).itemsize for m in scratch_shapes)

    def call(*inputs):
        windows = sum(_block_bytes(s, a.dtype) for s, a in zip(in_specs, inputs, strict=True))
        return pl.pallas_call(
            body, grid=grid, in_specs=in_specs, out_specs=out_specs, out_shape=out_shape,
            scratch_shapes=list(scratch_shapes),
            compiler_params=pltpu.CompilerParams(
                dimension_semantics=("arbitrary",) * len(grid),
                vmem_limit_bytes=need + windows + V7X_COMPILER_VMEM_RESERVE),
            name=name)(*inputs)

    return call


def _mod_body(c_ref, w_ref, b_ref, o_ref):
    c = c_ref[...]
    ca = (c * _sigmoid(c)).astype(BF16)
    o_ref[...] = jnp.dot(ca, w_ref[...].astype(BF16), preferred_element_type=F32) + b_ref[...]


def _modulation(c, w, b):
    n_l, d, n = w.shape
    batch = c.shape[0]
    rows = -(-batch // F32_SUBLANES) * F32_SUBLANES
    nb = MOD_COLS
    out = _pallas(
        _mod_body,
        grid=(n_l, n // nb),
        in_specs=[
            pl.BlockSpec((rows, d), lambda l, j: (0, 0)),
            pl.BlockSpec((None, d, nb), lambda l, j: (l, 0, j)),
            pl.BlockSpec((None, 1, nb), lambda l, j: (l, 0, j)),
        ],
        out_specs=pl.BlockSpec((None, rows, nb), lambda l, j: (l, 0, j)),
        out_shape=jax.ShapeDtypeStruct((n_l, rows, n), F32),
        name="modulation",
    )(jnp.pad(c, ((0, rows - batch), (0, 0))), w, b.reshape(n_l, 1, n))
    return out[:, :batch]


def _gmlp_body(x_ref, mod_ref, pre_ref, post_ref, win_ref, bin_ref, lng_ref, lnb_ref, ws_ref,
               bs_ref, wout_ref, o_ref, h_scr, u_scr, v_scr, vn_scr, y_scr):
    tm, d = x_ref.shape
    gw = u_scr.shape[1]
    x = x_ref[...]
    h_scr[...] = _prenorm(x, pre_ref[...], mod_ref[0:1, :], mod_ref[1:2, :]).astype(BF16)

    k_gelu = 0.7978845608028654
    nc = 2 * MXU_N
    for c0 in range(0, 2 * gw, nc):
        z = jnp.dot(h_scr[...], win_ref[:, c0:c0 + nc], preferred_element_type=F32)
        z = z + bin_ref[:, c0:c0 + nc]
        z = z * (0.5 * (1.0 + jnp.tanh(k_gelu * (z + 0.044715 * (z * z * z)))))
        if c0 < gw:
            u_scr[:, c0:c0 + nc] = z
        else:
            v_scr[:, c0 - gw:c0 - gw + nc] = z

    v = v_scr[...]
    mu = jnp.mean(v, axis=-1, keepdims=True)
    vc = v - mu
    rstd = lax.rsqrt(jnp.mean(vc * vc, axis=-1, keepdims=True) + EPS)
    vn_scr[...] = (vc * rstd * lng_ref[...] + lnb_ref[...]).astype(BF16)

    n_chunks = tm // CHUNK
    row = lax.broadcasted_iota(jnp.int32, (CHUNK, CHUNK), 0)
    col = lax.broadcasted_iota(jnp.int32, (CHUNK, CHUNK), 1)
    causal = col <= row
    for g in range(GROUPS):
        gs = slice(g * CHUNK, (g + 1) * CHUNK)
        ws = jnp.where(causal, ws_ref[g], 0.0).astype(BF16)
        rhs = jnp.concatenate(
            [vn_scr[c * CHUNK:(c + 1) * CHUNK, gs] for c in range(n_chunks)], axis=1)
        sp = jnp.dot(ws, rhs, preferred_element_type=F32)
        for c in range(n_chunks):
            cs = slice(c * CHUNK, (c + 1) * CHUNK)
            y_scr[cs, gs] = (u_scr[cs, gs] * (sp[:, cs] + bs_ref[:, gs])).astype(BF16)

    y = jnp.dot(y_scr[...], wout_ref[:, 0:d], preferred_element_type=F32)
    o_ref[...] = _gated_residual(x, y, post_ref[...], mod_ref[2:3, :])


def _gmlp_layer(x, mod, pre_g, post_g, w_in, b_in, ln_g, ln_b, w_s, bs_full, w_out, layer):
    b, s, d = x.shape
    gw = w_out.shape[1]
    tm = TOKEN_TILE
    tok = pl.BlockSpec((None, tm, d), lambda i, j: (i, j, 0))
    return _pallas(
        _gmlp_body,
        grid=(b, s // tm),
        in_specs=[
            tok,
            pl.BlockSpec((None, 6, d), lambda i, j: (i, 0, 0)),
            _const_spec((1, d)), _const_spec((1, d)),
            _slab_spec((None,) + w_in.shape[1:], (layer, 0, 0)), _const_spec((1, 2 * gw)),
            _const_spec((1, gw)), _const_spec((1, gw)),
            _const_spec((GROUPS, CHUNK, CHUNK)), _const_spec((CHUNK, gw)),
            _slab_spec((None,) + w_out.shape[1:], (layer, 0, 0)),
        ],
        out_specs=tok,
        out_shape=jax.ShapeDtypeStruct((b, s, d), F32),
        scratch_shapes=[
            pltpu.VMEM((tm, d), BF16), pltpu.VMEM((tm, gw), F32), pltpu.VMEM((tm, gw), F32),
            pltpu.VMEM((tm, gw), BF16), pltpu.VMEM((tm, gw), BF16),
        ],
        name="gmlp_mixer",
    )(x, mod, pre_g, post_g, w_in, b_in, ln_g, ln_b, w_s, bs_full, w_out)


def _ffn_body(x_ref, mod_ref, pre_ref, post_ref, wgu_ref, wd_ref, o_ref, h_scr, a_scr):
    f = a_scr.shape[1]
    x = x_ref[...]
    h_scr[...] = _prenorm(x, pre_ref[...], mod_ref[3:4, :], mod_ref[4:5, :]).astype(BF16)
    for c0 in range(0, f, MXU_N):
        g = jnp.dot(h_scr[...], wgu_ref[:, c0:c0 + MXU_N], preferred_element_type=F32)
        u = jnp.dot(h_scr[...], wgu_ref[:, f + c0:f + c0 + MXU_N], preferred_element_type=F32)
        a_scr[:, c0:c0 + MXU_N] = (g * _sigmoid(g) * u).astype(BF16)
    y = jnp.dot(a_scr[...], wd_ref[...], preferred_element_type=F32)
    o_ref[...] = _gated_residual(x, y, post_ref[...], mod_ref[5:6, :])


def _ffn_layer(x, mod, pre_g, post_g, w_gu, w_down, layer):
    b, s, d = x.shape
    f = w_down.shape[1]
    tm = WIDE_TOKEN_TILE
    tok = pl.BlockSpec((None, tm, d), lambda i, j: (i, j, 0))
    return _pallas(
        _ffn_body,
        grid=(b, s // tm),
        in_specs=[
            tok,
            pl.BlockSpec((None, 6, d), lambda i, j: (i, 0, 0)),
            _const_spec((1, d)), _const_spec((1, d)),
            _slab_spec((None, d, 2 * f), (layer, 0, 0)), _slab_spec((None, f, d), (layer, 0, 0)),
        ],
        out_specs=tok,
        out_shape=jax.ShapeDtypeStruct((b, s, d), F32),
        scratch_shapes=[pltpu.VMEM((tm, d), BF16), pltpu.VMEM((tm, f), BF16)],
        name="swiglu_ffn",
    )(x, mod, pre_g, post_g, w_gu, w_down)


def _head_pair_norm(t, gain2):
    lane = lax.broadcasted_iota(jnp.int32, t.shape, 1)
    sq = t * t
    first = lane < HEAD_DIM
    s0 = jnp.sum(jnp.where(first, sq, 0.0), axis=-1, keepdims=True)
    s1 = jnp.sum(jnp.where(first, 0.0, sq), axis=-1, keepdims=True)
    rs = jnp.where(first, lax.rsqrt(s0 / HEAD_DIM + EPS), lax.rsqrt(s1 / HEAD_DIM + EPS))
    return t * rs * gain2


def _decay_selectors(query_side):
    row = lax.broadcasted_iota(jnp.int32, (LANES, N_HEADS * LANES), 0)
    col = lax.broadcasted_iota(jnp.int32, (LANES, N_HEADS * LANES), 1)
    head, off = col // LANES, col % LANES - HEAD_DIM
    first, second = (off >= 0) & (off < 3), (off >= 3) & (off < 6)
    piece_cols, piece_idx = (first, off) if query_side else (second, off - 3)
    one_cols = second if query_side else first
    is_piece = piece_cols & (row == piece_idx * N_HEADS + head)
    is_one = one_cols & (row == 3 * N_HEADS)
    sel = jnp.where(is_piece, 1.0 if query_side else -1.0, jnp.where(is_one, 1.0, 0.0))
    return sel.astype(BF16)


def _with_tail(t, tail):
    lane = lax.broadcasted_iota(jnp.int32, t.shape, 1)
    return jnp.where(lane < HEAD_DIM, t, tail)


def _kv_body(x_ref, mod_ref, g_ref, wk_ref, wv_ref, wf_ref, bf_ref, kg_ref, sel_ref,
             k_out, vt_out, d_out, h_scr, carry_scr):
    tm, d = x_ref.shape

    @pl.when(pl.program_id(1) == 0)
    def _():
        carry_scr[...] = jnp.zeros_like(carry_scr)

    h_scr[...] = _prenorm(x_ref[...], g_ref[...], mod_ref[0:1, :], mod_ref[1:2, :]).astype(BF16)

    lane = lax.broadcasted_iota(jnp.int32, (tm, LANES), 1)
    fl = jnp.dot(h_scr[...], wf_ref[...], preferred_element_type=F32) + bf_ref[...]
    z = -fl
    ls = -(jnp.maximum(z, 0.0) + jnp.log1p(jnp.exp(-jnp.abs(z))))
    hi, mid, lo = _split3(ls)
    pieces = jnp.where(lane < N_HEADS, hi, jnp.where(lane < 2 * N_HEADS, mid, jnp.where(
        lane < 3 * N_HEADS, lo, 0.0))).astype(BF16)
    r = lax.broadcasted_iota(jnp.int32, (tm, tm), 0)
    c = lax.broadcasted_iota(jnp.int32, (tm, tm), 1)
    tri = jnp.where(c <= r, 1.0, 0.0).astype(BF16)
    cs = jnp.dot(tri, pieces, preferred_element_type=F32)
    tot = cs + pltpu.roll(cs, LANES - N_HEADS, axis=1) + pltpu.roll(cs, LANES - 2 * N_HEADS, axis=1)
    dc = jnp.where(lane < N_HEADS, tot + carry_scr[...], 0.0)
    carry_scr[...] = dc[tm - 1:tm, :]
    hi, mid, lo = _split3(dc * LOG2E)
    d3 = (hi + pltpu.roll(mid, N_HEADS, axis=1) + pltpu.roll(lo, 2 * N_HEADS, axis=1)
          + jnp.where(lane == 3 * N_HEADS, 1.0, 0.0)).astype(BF16)
    d_out[...] = d3

    kg2 = kg_ref[...]
    ones_rows = jnp.where(
        lax.broadcasted_iota(jnp.int32, (V_ROWS - HEAD_DIM, tm), 0) == 0, 1.0, 0.0)
    heads_per_chunk = MXU_N // HEAD_DIM
    for c4 in range(d // MXU_N):
        kc = jnp.dot(h_scr[...], wk_ref[:, c4 * MXU_N:(c4 + 1) * MXU_N], preferred_element_type=F32)
        vc = jnp.dot(h_scr[...], wv_ref[:, c4 * MXU_N:(c4 + 1) * MXU_N], preferred_element_type=F32)
        tails = jnp.dot(d3, sel_ref[:, c4 * heads_per_chunk * LANES:(c4 + 1) * heads_per_chunk * LANES],
                        preferred_element_type=F32)
        for half in range(2):
            kn = _head_pair_norm(kc[:, half * LANES:(half + 1) * LANES], kg2)
            vt = vc[:, half * LANES:(half + 1) * LANES].T
            for sub, src in ((0, kn), (1, pltpu.roll(kn, HEAD_DIM, axis=1))):
                loc = 2 * half + sub
                hd = c4 * heads_per_chunk + loc
                k_out[hd] = _with_tail(src, tails[:, loc * LANES:(loc + 1) * LANES]).astype(BF16)
                vt_out[hd] = jnp.concatenate(
                    [vt[sub * HEAD_DIM:(sub + 1) * HEAD_DIM, :], ones_rows], axis=0).astype(BF16)


def _q_body(x_ref, mod_ref, pre_ref, wq_ref, qg_ref, d_ref, sel_ref, q_out, h_scr):
    tm, d = x_ref.shape
    h_scr[...] = _prenorm(x_ref[...], pre_ref[...], mod_ref[0:1, :], mod_ref[1:2, :]).astype(BF16)
    d3 = d_ref[...]
    qg2 = qg_ref[...]
    heads_per_chunk = MXU_N // HEAD_DIM
    for c4 in range(d // MXU_N):
        qc = jnp.dot(h_scr[...], wq_ref[:, c4 * MXU_N:(c4 + 1) * MXU_N], preferred_element_type=F32)
        tails = jnp.dot(d3, sel_ref[:, c4 * heads_per_chunk * LANES:(c4 + 1) * heads_per_chunk * LANES],
                        preferred_element_type=F32)
        for half in range(2):
            qn = _head_pair_norm(qc[:, half * LANES:(half + 1) * LANES], qg2)
            for sub, src in ((0, qn), (1, pltpu.roll(qn, HEAD_DIM, axis=1))):
                loc = 2 * half + sub
                qa = _with_tail(src, tails[:, loc * LANES:(loc + 1) * LANES])
                q_out[c4 * heads_per_chunk + loc] = qa.astype(BF16)


def _kvq_body(x_ref, kvmod_ref, g_ref, wk_ref, wv_ref, wf_ref, bf_ref, kg_ref, selk_ref,
              mod_ref, pre_ref, wq_ref, qg_ref, selq_ref,
              k_out, vt_out, d_out, q_out, h_scr, carry_scr, hq_scr):
    _kv_body(x_ref, kvmod_ref, g_ref, wk_ref, wv_ref, wf_ref, bf_ref, kg_ref, selk_ref,
             k_out, vt_out, d_out, h_scr, carry_scr)
    _q_body(x_ref, mod_ref, pre_ref, wq_ref, qg_ref, d_out, selq_ref, q_out, hq_scr)


def _kvq_project(x, kvmod, norm_g, w_k, w_v, w_f3, b_f3, kg2, mod, pre_g, w_q, qg2):
    b, s, d = x.shape
    tm = TOKEN_TILE
    head_tiles = pl.BlockSpec((None, N_HEADS, tm, LANES), lambda i, j: (i, 0, j, 0))
    return _pallas(
        _kvq_body,
        grid=(b, s // tm),
        in_specs=[
            pl.BlockSpec((None, tm, d), lambda i, j: (i, j, 0)),
            pl.BlockSpec((None, 2, d), lambda i, j: (i, 0, 0)),
            _const_spec((1, d)),
            _const_spec((d, d)), _const_spec((d, d)), _const_spec((d, LANES)),
            _const_spec((1, LANES)), _const_spec((1, LANES)), _const_spec((LANES, N_HEADS * LANES)),
            pl.BlockSpec((None, 6, d), lambda i, j: (i, 0, 0)),
            _const_spec((1, d)), _const_spec((d, d)), _const_spec((1, LANES)),
            _const_spec((LANES, N_HEADS * LANES)),
        ],
        out_specs=[
            head_tiles,
            pl.BlockSpec((None, N_HEADS, None, V_ROWS, tm), lambda i, j: (i, 0, j, 0, 0)),
            pl.BlockSpec((None, tm, LANES), lambda i, j: (i, j, 0)),
            head_tiles,
        ],
        out_shape=[
            jax.ShapeDtypeStruct((b, N_HEADS, s, LANES), BF16),
            jax.ShapeDtypeStruct((b, N_HEADS, s // tm, V_ROWS, tm), BF16),
            jax.ShapeDtypeStruct((b, s, LANES), BF16),
            jax.ShapeDtypeStruct((b, N_HEADS, s, LANES), BF16),
        ],
        scratch_shapes=[pltpu.VMEM((tm, d), BF16), pltpu.VMEM((1, LANES), F32), pltpu.VMEM((tm, d), BF16)],
        name="kvq_project",
    )(x, kvmod, norm_g, w_k, w_v, w_f3, b_f3, kg2, _decay_selectors(query_side=False),
      mod, pre_g, w_q, qg2, _decay_selectors(query_side=True))


def _q_project(x, mod, pre_g, w_q, qg2, d3):
    b, s, d = x.shape
    tm = TOKEN_TILE
    return _pallas(
        _q_body,
        grid=(b, s // tm),
        in_specs=[
            pl.BlockSpec((None, tm, d), lambda i, j: (i, j, 0)),
            pl.BlockSpec((None, 6, d), lambda i, j: (i, 0, 0)),
            _const_spec((1, d)), _const_spec((d, d)), _const_spec((1, LANES)),
            pl.BlockSpec((None, tm, LANES), lambda i, j: (i, j, 0)),
            _const_spec((LANES, N_HEADS * LANES)),
        ],
        out_specs=pl.BlockSpec((None, N_HEADS, tm, LANES), lambda i, j: (i, 0, j, 0)),
        out_shape=jax.ShapeDtypeStruct((b, N_HEADS, s, LANES), BF16),
        scratch_shapes=[pltpu.VMEM((tm, d), BF16)],
        name="q_project",
    )(x, mod, pre_g, w_q, qg2, d3, _decay_selectors(query_side=True))


def _attn_body(q_ref, k_ref, vt_ref, o_ref, acc_scr, *s_scr):
    n_h = q_ref.shape[0]
    tk = vt_ref.shape[3]
    tq = 2 * tk
    nq = q_ref.shape[1] // tq
    heads = range(n_h)
    halves = (slice(0, tk), slice(tk, tq))
    kpos = lax.broadcasted_iota(jnp.int32, (tk, tk), 0)
    qpos = lax.broadcasted_iota(jnp.int32, (tk, tk), 1)
    tri = kpos <= qpos
    LO, HI = 0, 1

    def scores_head(h, qt, j, slot, g, valid):
        q = q_ref[h, pl.ds(pl.multiple_of(qt * tq + g * tk, tk), tk), :]
        k = k_ref[h, pl.ds(pl.multiple_of(j * tk, tk), tk), :]
        s = lax.dot_general(k, q, (((1,), (1,)), ((), ())), preferred_element_type=F32)
        if valid is not None:
            s = jnp.where(valid, s, -jnp.inf)
        s_scr[2 * h + slot][:, halves[g]] = s
        return jnp.max(s, axis=0, keepdims=True)

    def accumulate_head(h, j, slot, g, cmax, m):
        m_new = jnp.maximum(m, cmax)
        alpha = jnp.exp2(m - m_new)
        p = jnp.exp2(s_scr[2 * h + slot][:, halves[g]] - m_new)
        pv = jnp.dot(vt_ref[h, j], p.astype(BF16), preferred_element_type=F32)
        acc_scr[h, :, halves[g]] = alpha * acc_scr[h, :, halves[g]] + pv
        return m_new

    def scores(qt, j, slot, g, valid):
        return tuple(scores_head(h, qt, j, slot, g, valid) for h in heads)

    def accumulate(j, slot, g, cmax, ml):
        return tuple(accumulate_head(h, j, slot, g, cmax[h], ml[h]) for h in heads)

    def pair(qi, p, carry, diag):
        (ml_lo, ml_hi), (c0_lo, c0_hi) = carry
        out = []
        for h in heads:
            m_lo, m_hi = ml_lo[h], ml_hi[h]
            c1_lo = scores_head(h, qi, 2 * p + 1, 1, LO, None)
            m_lo = accumulate_head(h, 2 * p, 0, LO, c0_lo[h], m_lo)
            c1_hi = scores_head(h, qi, 2 * p + 1, 1, HI, None)
            m_hi = accumulate_head(h, 2 * p, 0, HI, c0_hi[h], m_hi)
            n_lo = scores_head(h, qi, 2 * p + 2, 0, LO, tri if diag else None)
            m_lo = accumulate_head(h, 2 * p + 1, 1, LO, c1_lo, m_lo)
            n_hi = scores_head(h, qi, 2 * p + 2, 0, HI, None)
            m_hi = accumulate_head(h, 2 * p + 1, 1, HI, c1_hi, m_hi)
            out.append((m_lo, m_hi, n_lo, n_hi))
        m_lo, m_hi, n_lo, n_hi = (tuple(o[i] for o in out) for i in range(4))
        return (m_lo, m_hi), (n_lo, n_hi)

    def query_tile(qi, cm0):
        acc_scr[...] = jnp.zeros_like(acc_scr)
        fresh = tuple(jnp.full((1, tk), -jnp.inf, F32) for _ in heads)
        n_loop = jnp.maximum(qi - 1, 0)

        def two_pairs(t, c):
            return pair(qi, 2 * t + 1, pair(qi, 2 * t, c, False), False)

        carry = lax.fori_loop(0, n_loop // 2, two_pairs, ((fresh, fresh), cm0))

        def finish(carry, n_front):
            if n_front == 2:
                carry = pair(qi, qi - 2, carry, False)
            if n_front >= 1:
                carry = pair(qi, qi - 1, carry, True)
            (ml_lo, ml_hi), (c0_lo, c0_hi) = carry
            ml_lo = accumulate(2 * qi, 0, LO, c0_lo, ml_lo)
            c1_hi = scores(qi, 2 * qi + 1, 1, HI, tri)
            ml_hi = accumulate(2 * qi, 0, HI, c0_hi, ml_hi)
            nxt = jnp.minimum(qi + 1, nq - 1)
            n_lo = scores(nxt, 0, 0, LO, None)
            accumulate(2 * qi + 1, 1, HI, c1_hi, ml_hi)
            n_hi = scores(nxt, 0, 0, HI, None)
            for g in (LO, HI):
                o_t = jnp.concatenate(
                    [acc_scr[h, 0:HEAD_DIM, halves[g]] / acc_scr[h, HEAD_DIM:HEAD_DIM + 1, halves[g]]
                     for h in heads], axis=0)
                o_ref[pl.ds(pl.multiple_of((2 * qi + g) * tk, tk), tk), :] = o_t.T
            return n_lo, n_hi

        n_front = jnp.minimum(qi, 1) + (n_loop & 1)
        return lax.switch(n_front, [functools.partial(finish, n_front=n) for n in range(3)], carry)

    lax.fori_loop(0, nq, query_tile, (scores(0, 0, 0, LO, tri), scores(0, 0, 0, HI, None)))


def _fox_attention(q_aug, k_aug, vt_aug):
    b, h, s, _ = k_aug.shape
    tk = TOKEN_TILE
    nk = s // tk
    hp = ATTN_HEADS_PER_STEP
    return _pallas(
        _attn_body,
        grid=(b, h // hp),
        in_specs=[
            pl.BlockSpec((None, hp, s, LANES), lambda i, j: (i, j, 0, 0)),
            pl.BlockSpec((None, hp, s, LANES), lambda i, j: (i, j, 0, 0)),
            pl.BlockSpec((None, hp, nk, V_ROWS, tk), lambda i, j: (i, j, 0, 0, 0)),
        ],
        out_specs=pl.BlockSpec((None, s, hp * HEAD_DIM), lambda i, j: (i, 0, j)),
        out_shape=jax.ShapeDtypeStruct((b, s, h * HEAD_DIM), F32),
        scratch_shapes=[pltpu.VMEM((hp, V_ROWS, ATTN_Q_TILE), F32)]
        + [pltpu.VMEM((tk, ATTN_Q_TILE), F32)] * (2 * hp),
        name="fox_attention",
    )(q_aug, k_aug, vt_aug)


def _out_body(x_ref, mod_ref, pre_ref, post_ref, a_ref, wg_ref, wo_ref, o_ref):
    x = x_ref[...]
    h = _prenorm(x, pre_ref[...], mod_ref[0:1, :], mod_ref[1:2, :]).astype(BF16)
    gate = _sigmoid(jnp.dot(h, wg_ref[...], preferred_element_type=F32))
    y = jnp.dot((a_ref[...] * gate).astype(BF16), wo_ref[...], preferred_element_type=F32)
    o_ref[...] = _gated_residual(x, y, post_ref[...], mod_ref[2:3, :])


def _fox_output(x, mod, pre_g, post_g, o_t, w_g, w_o):
    b, s, d = x.shape
    tm = WIDE_TOKEN_TILE
    tok = pl.BlockSpec((None, tm, d), lambda i, j: (i, j, 0))
    return _pallas(
        _out_body,
        grid=(b, s // tm),
        in_specs=[
            tok,
            pl.BlockSpec((None, 6, d), lambda i, j: (i, 0, 0)),
            _const_spec((1, d)), _const_spec((1, d)),
            tok,
            _const_spec((d, d)), _const_spec((d, d)),
        ],
        out_specs=tok,
        out_shape=jax.ShapeDtypeStruct((b, s, d), F32),
        name="fox_output",
    )(x, mod, pre_g, post_g, o_t, w_g, w_o)


def _out_ffn_body(x_ref, mod_ref, pre_ref, post_ref, a_ref, wg_ref, wo_ref, pre2_ref, post2_ref,
                  wgu_ref, wd_ref, o_ref, h_scr, a_scr):
    _out_body(x_ref, mod_ref, pre_ref, post_ref, a_ref, wg_ref, wo_ref, o_ref)
    _ffn_body(o_ref, mod_ref, pre2_ref, post2_ref, wgu_ref, wd_ref, o_ref, h_scr, a_scr)


def _fox_output_ffn(x, mod, pre_g, post_g, o_t, w_g, w_o, pre2_g, post2_g, w_gu, w_down, layer):
    b, s, d = x.shape
    f = w_down.shape[1]
    tm = WIDE_TOKEN_TILE
    tok = pl.BlockSpec((None, tm, d), lambda i, j: (i, j, 0))
    return _pallas(
        _out_ffn_body,
        grid=(b, s // tm),
        in_specs=[
            tok,
            pl.BlockSpec((None, 6, d), lambda i, j: (i, 0, 0)),
            _const_spec((1, d)), _const_spec((1, d)),
            tok,
            _const_spec((d, d)), _const_spec((d, d)),
            _const_spec((1, d)), _const_spec((1, d)),
            _slab_spec((None, d, 2 * f), (layer, 0, 0)), _slab_spec((None, f, d), (layer, 0, 0)),
        ],
        out_specs=tok,
        out_shape=jax.ShapeDtypeStruct((b, s, d), F32),
        scratch_shapes=[pltpu.VMEM((tm, d), BF16), pltpu.VMEM((tm, f), BF16)],
        name="fox_output_ffn",
    )(x, mod, pre_g, post_g, o_t, w_g, w_o, pre2_g, post2_g, w_gu, w_down)


def kernel(x, c, ada_w, ada_b, pre_mix_g, post_mix_g, pre_ffn_g, post_ffn_g, ffn_w_gu, ffn_w_down,
           a_w_in, a_b_in, a_ln_g, a_ln_b, a_w_s, a_b_s, a_w_out, kv_ada_w, kv_ada_b, kv_norm_g,
           kv_w, kv_b_f, k_norm_g, b_w_qg, b_q_norm_g, b_w_o):
    b, s, d = x.shape
    depth = ada_w.shape[0]
    n_a = a_w_in.shape[0]
    assert d == N_HEADS * HEAD_DIM and s % ATTN_Q_TILE == 0

    mods = _modulation(c, ada_w, ada_b).reshape(depth, b, 6, d)
    kvmod = _modulation(c, kv_ada_w[None], kv_ada_b[None]).reshape(b, 2, d)

    row = lambda v: v.reshape(1, -1)
    q_scale = LOG2E * HEAD_DIM ** -0.5
    q_gain = lambda j: row(jnp.tile(b_q_norm_g[j] * q_scale, 2))

    ffn_gu, ffn_down = ffn_w_gu.astype(BF16), ffn_w_down.astype(BF16)
    gmlp_in, gmlp_out = _lane_padded(a_w_in), _lane_padded(a_w_out)
    fox_qg, fox_o, kvf = b_w_qg.astype(BF16), b_w_o.astype(BF16), kv_w.astype(BF16)

    for layer in range(depth):
        mod = mods[layer]
        if layer < n_a:
            i = layer
            bs_full = jnp.repeat(a_b_s[i].T, CHUNK, axis=1)
            x = _gmlp_layer(x, mod, row(pre_mix_g[layer]), row(post_mix_g[layer]), gmlp_in,
                            row(a_b_in[i]), row(a_ln_g[i]), row(a_ln_b[i]), a_w_s[i], bs_full, gmlp_out, i)
        else:
            j = layer - n_a
            if j > 0:
                q_aug = _q_project(x, mod, row(pre_mix_g[layer]), fox_qg[j][:, :d], q_gain(j), d3)
            o_t = _fox_attention(q_aug, k_aug, vt_aug)
            x = _fox_output_ffn(x, mod, row(pre_mix_g[layer]), row(post_mix_g[layer]), o_t,
                                fox_qg[j][:, d:], fox_o[j], row(pre_ffn_g[layer]),
                                row(post_ffn_g[layer]), ffn_gu, ffn_down, layer)
            continue
        x = _ffn_layer(x, mod, row(pre_ffn_g[layer]), row(post_ffn_g[layer]), ffn_gu, ffn_down, layer)
        if layer == n_a - 1:
            w_f = kv_w[:, 2 * d:]
            pad = jnp.zeros((d, LANES - 3 * N_HEADS), F32)
            w_f3 = jnp.concatenate([w_f, w_f, w_f, pad], axis=1).astype(BF16)
            b_f3 = row(jnp.concatenate([kv_b_f, kv_b_f, kv_b_f, jnp.zeros((LANES - 3 * N_HEADS,), F32)]))
            k_aug, vt_aug, d3, q_aug = _kvq_project(
                x, kvmod, row(kv_norm_g), kvf[:, :d], kvf[:, d:2 * d], w_f3, b_f3,
                row(jnp.tile(k_norm_g, 2)),
                mods[n_a], row(pre_mix_g[n_a]), fox_qg[0][:, :d], q_gain(0))
    return x
```
